```python
import math
import jax, jax.numpy as jnp
from jax import lax
import numpy as np

D_MODEL = 1024
BATCH = 8
SEQ = 2048
DEPTH = 4

CHUNK = 64
Q_BLOCK = 128

ATTN_HEADS = 4
ATTN_WIDTH = D_MODEL // 2
ATTN_HEAD_DIM = ATTN_WIDTH // (2 * ATTN_HEADS)

POOL_WINDOWS = (2, 4, 8, 16)
POOL_GROUPS = len(POOL_WINDOWS)
POOL_WIDTH = D_MODEL // 4
POOL_GROUP_DIM = POOL_WIDTH // POOL_GROUPS

LRU_WIDTH = D_MODEL // 4
LRU_BLOCKS = 4
LRU_BLOCK_DIM = LRU_WIDTH // LRU_BLOCKS
LRU_C = 8.0
CONV_WIDTH = 4

MIX_WIDTH = ATTN_WIDTH + POOL_WIDTH + LRU_WIDTH
IN_WIDTH = 3 * ATTN_WIDTH + POOL_WIDTH + 2 * LRU_WIDTH
IN_SPLITS = (ATTN_WIDTH, 2 * ATTN_WIDTH, 3 * ATTN_WIDTH,
             3 * ATTN_WIDTH + POOL_WIDTH, 3 * ATTN_WIDTH + POOL_WIDTH + LRU_WIDTH)

D_FF = ((8 * D_MODEL // 3 + 127) // 128) * 128
N_EXPERTS = 8
TOP_K = 2
MOE_BLOCK = 256

LN_EPS = 1e-5
HEAD_NORM_EPS = 1e-5

kernel_name = 'hybrid_diffattn_pool_rglru_moe_deepnorm'


def _layer_norm(x, g, b):
    xf = x.astype(jnp.float32)
    mu = jnp.mean(xf, axis=-1, keepdims=True)
    var = jnp.mean(jnp.square(xf - mu), axis=-1, keepdims=True)
    return ((xf - mu) * lax.rsqrt(var + LN_EPS) * g + b).astype(x.dtype)


def _diff_attention(q, k, v, lam, subln_g, lambda_init):
    B, S = q.shape[0], q.shape[1]
    scale = ATTN_HEAD_DIM ** -0.5
    outs = []
    for blk in range(S // Q_BLOCK):
        q0 = blk * Q_BLOCK
        k_end = q0 + Q_BLOCK
        qb = q[:, q0:k_end]
        kb = k[:, :k_end]
        vb = v[:, :k_end]
        s = jnp.einsum('bqhmd,bkhmd->bhmqk', qb, kb).astype(jnp.float32) * scale
        q_chunk = (q0 + jnp.arange(Q_BLOCK)) // CHUNK
        k_chunk = jnp.arange(k_end) // CHUNK
        mask = k_chunk[None, :] <= q_chunk[:, None]
        s = jnp.where(mask, s, -jnp.inf)
        p = jax.nn.softmax(s, axis=-1)
        p_diff = p[:, :, 0] - lam * p[:, :, 1]
        outs.append(jnp.einsum('bhqk,bkhe->bqhe', p_diff.astype(v.dtype), vb))
    o = jnp.concatenate(outs, axis=1).astype(jnp.float32)
    o = o * lax.rsqrt(jnp.mean(jnp.square(o), axis=-1, keepdims=True) + HEAD_NORM_EPS)
    o = o * subln_g * (1.0 - lambda_init)
    return o.reshape(B, S, ATTN_WIDTH).astype(v.dtype)


def _pool_mixer(u, pool_w, pool_scale):
    B, S, _ = u.shape
    ug = u.reshape(B, S, POOL_GROUPS, POOL_GROUP_DIM)
    c = jnp.cumsum(ug.astype(jnp.float32), axis=1)
    c = jnp.pad(c, ((0, 0), (1, 0), (0, 0), (0, 0)))
    t = jnp.arange(S)
    win = jnp.array(POOL_WINDOWS, dtype=jnp.int32)
    lo = jnp.maximum(t[:, None] + 1 - win[None, :], 0)
    grp = jnp.arange(POOL_GROUPS)
    window_sum = c[:, 1:] - c[:, lo, grp]
    count = (t[:, None] + 1 - lo).astype(jnp.float32)
    pooled = window_sum / count[None, :, :, None] - ug.astype(jnp.float32)
    y = jnp.einsum('bsgc,gcd->bsgd', pooled.astype(u.dtype), pool_w)
    return y.reshape(B, S, POOL_WIDTH) * pool_scale


def _linear_combine(left, right):
    a_l, b_l = left
    a_r, b_r = right
    return a_l * a_r, a_r * b_l + b_r


def _rglru_mixer(xr, xg, conv_w, conv_b, w_a, b_a, w_i, b_i, lam):
    B, S, W = xr.shape
    xp = jnp.pad(xr, ((0, 0), (CONV_WIDTH - 1, 0), (0, 0)))
    xc = conv_b + xp[:, 0:S] * conv_w[0]
    for j in range(1, CONV_WIDTH):
        xc = xc + xp[:, j:j + S] * conv_w[j]
    xb = xc.reshape(B, S, LRU_BLOCKS, LRU_BLOCK_DIM)
    r = jax.nn.sigmoid((jnp.einsum('bshc,hcd->bshd', xb, w_a).reshape(B, S, W) + b_a).astype(jnp.float32))
    i = jax.nn.sigmoid((jnp.einsum('bshc,hcd->bshd', xb, w_i).reshape(B, S, W) + b_i).astype(jnp.float32))
    log_a = -LRU_C * r * jax.nn.softplus(-lam.astype(jnp.float32))
    a = jnp.exp(log_a)
    bterm = jnp.sqrt(-jnp.expm1(2.0 * log_a)) * (i * xc.astype(jnp.float32))
    _, h = lax.associative_scan(_linear_combine, (a, bterm), axis=1)
    return h.astype(xr.dtype) * jax.nn.gelu(xg)


def _swiglu(x, w_gate, w_up, w_down):
    return (jax.nn.silu(x @ w_gate) * (x @ w_up)) @ w_down


def _moe_swiglu(x, w_router, w_gate, w_up, w_down):
    B, S, D = x.shape
    xt = x.reshape(-1, D)
    T = xt.shape[0]
    P = T * TOP_K
    logits = (xt @ w_router).astype(jnp.float32)
    top_logit, top_e = lax.top_k(logits, TOP_K)
    gate = jax.nn.softmax(top_logit, axis=-1)
    flat_e = top_e.reshape(-1)
    order = jnp.argsort(flat_e)
    e_sorted = flat_e[order]
    tok = order // TOP_K
    sizes = jnp.bincount(flat_e, length=N_EXPERTS)
    padded = (sizes + MOE_BLOCK - 1) // MOE_BLOCK * MOE_BLOCK
    start = jnp.cumsum(sizes) - sizes
    pstart = jnp.cumsum(padded) - padded
    dest = pstart[e_sorted] + jnp.arange(P) - start[e_sorted]
    n_blocks = -(-P // MOE_BLOCK) + N_EXPERTS
    rows = jnp.zeros((n_blocks * MOE_BLOCK, D), x.dtype).at[dest].set(xt[tok])
    block_start = jnp.arange(n_blocks) * MOE_BLOCK
    block_e = jnp.minimum(jnp.sum(jnp.cumsum(padded)[None, :] <= block_start[:, None], axis=1), N_EXPERTS - 1)

    def expert_block(args):
        xb, e = args
        return _swiglu(xb, w_gate[e], w_up[e], w_down[e])

    ys = lax.map(expert_block, (rows.reshape(n_blocks, MOE_BLOCK, D), block_e)).reshape(-1, D)
    contrib = ys[dest] * gate.reshape(-1)[order][:, None].astype(x.dtype)
    return jnp.zeros_like(xt).at[tok].add(contrib).reshape(B, S, D)


def setup_inputs(seed: int = 0) -> dict:
    key = jax.random.key(seed)
    ks = iter(jax.random.split(key, 32))
    n_dense = (DEPTH + 1) // 2
    n_moe = DEPTH // 2
    beta = (8.0 * DEPTH) ** -0.25
    f32 = jnp.float32

    def nrm(shape, fan_in, scale=1.0):
        return jax.random.normal(next(ks), shape, f32) * (scale * fan_in ** -0.5)

    def noise(shape, s):
        return jax.random.normal(next(ks), shape, f32) * s

    x = jax.random.normal(next(ks), (BATCH, SEQ, D_MODEL), f32)
    col_scale = jnp.concatenate([
        jnp.ones((2 * ATTN_WIDTH,), f32),
        jnp.full((ATTN_WIDTH + POOL_WIDTH + LRU_WIDTH,), beta, f32),
        jnp.ones((LRU_WIDTH,), f32)])
    w_in = nrm((DEPTH, D_MODEL, IN_WIDTH), D_MODEL) * col_scale
    w_out = nrm((DEPTH, MIX_WIDTH, D_MODEL), MIX_WIDTH, beta)
    attn_lambda = noise((DEPTH, 4, ATTN_HEAD_DIM), 0.1)
    attn_subln_g = 1.0 + noise((DEPTH, 2 * ATTN_HEAD_DIM), 0.02)
    pool_w = nrm((DEPTH, POOL_GROUPS, POOL_GROUP_DIM, POOL_GROUP_DIM), POOL_GROUP_DIM)
    pool_scale = 1.0 + noise((DEPTH, POOL_WIDTH), 0.02)
    conv_w = nrm((DEPTH, CONV_WIDTH, LRU_WIDTH), CONV_WIDTH)
    conv_b = noise((DEPTH, LRU_WIDTH), 0.02)
    lru_wa = nrm((DEPTH, LRU_BLOCKS, LRU_BLOCK_DIM, LRU_BLOCK_DIM), LRU_BLOCK_DIM)
    lru_ba = noise((DEPTH, LRU_WIDTH), 0.02)
    lru_wi = nrm((DEPTH, LRU_BLOCKS, LRU_BLOCK_DIM, LRU_BLOCK_DIM), LRU_BLOCK_DIM)
    lru_bi = noise((DEPTH, LRU_WIDTH), 0.02)
    a_c = jax.random.uniform(next(ks), (DEPTH, LRU_WIDTH), f32, 0.9, 0.999)
    a0 = a_c ** (1.0 / LRU_C)
    lru_lambda = jnp.log(a0) - jnp.log1p(-a0)
    ln1_g = 1.0 + noise((DEPTH, D_MODEL), 0.02)
    ln1_b = noise((DEPTH, D_MODEL), 0.02)
    ln2_g = 1.0 + noise((DEPTH, D_MODEL), 0.02)
    ln2_b = noise((DEPTH, D_MODEL), 0.02)
    ffn_w_gate = nrm((n_dense, D_MODEL, D_FF), D_MODEL, beta)
    ffn_w_up = nrm((n_dense, D_MODEL, D_FF), D_MODEL, beta)
    ffn_w_down = nrm((n_dense, D_FF, D_MODEL), D_FF, beta)
    router_w = nrm((n_moe, D_MODEL, N_EXPERTS), D_MODEL)
    moe_w_gate = nrm((n_moe, N_EXPERTS, D_MODEL, D_FF), D_MODEL, beta)
    moe_w_up = nrm((n_moe, N_EXPERTS, D_MODEL, D_FF), D_MODEL, beta)
    moe_w_down = nrm((n_moe, N_EXPERTS, D_FF, D_MODEL), D_FF, beta)
    return {'x': x, 'w_in': w_in, 'w_out': w_out, 'attn_lambda': attn_lambda,
            'attn_subln_g': attn_subln_g, 'pool_w': pool_w, 'pool_scale': pool_scale,
            'conv_w': conv_w, 'conv_b': conv_b, 'lru_wa': lru_wa, 'lru_ba': lru_ba,
            'lru_wi': lru_wi, 'lru_bi': lru_bi, 'lru_lambda': lru_lambda,
            'ln1_g': ln1_g, 'ln1_b': ln1_b, 'ln2_g': ln2_g, 'ln2_b': ln2_b,
            'ffn_w_gate': ffn_w_gate, 'ffn_w_up': ffn_w_up, 'ffn_w_down': ffn_w_down,
            'router_w': router_w, 'moe_w_gate': moe_w_gate, 'moe_w_up': moe_w_up,
            'moe_w_down': moe_w_down}


def reference(x, w_in, w_out, attn_lambda, attn_subln_g, pool_w, pool_scale,
              conv_w, conv_b, lru_wa, lru_ba, lru_wi, lru_bi, lru_lambda,
              ln1_g, ln1_b, ln2_g, ln2_b, ffn_w_gate, ffn_w_up, ffn_w_down,
              router_w, moe_w_gate, moe_w_up, moe_w_down):
    alpha = (2.0 * DEPTH) ** 0.25
    B, S, _ = x.shape
    for l in range(DEPTH):
        h = x @ w_in[l]
        q, k, v, u, xr, xg = jnp.split(h, IN_SPLITS, axis=-1)
        q = q.reshape(B, S, ATTN_HEADS, 2, ATTN_HEAD_DIM)
        k = k.reshape(B, S, ATTN_HEADS, 2, ATTN_HEAD_DIM)
        v = v.reshape(B, S, ATTN_HEADS, 2 * ATTN_HEAD_DIM)
        lambda_init = 0.8 - 0.6 * math.exp(-0.3 * l)
        lv = attn_lambda[l].astype(jnp.float32)
        lam = jnp.exp(jnp.sum(lv[0] * lv[1])) - jnp.exp(jnp.sum(lv[2] * lv[3])) + lambda_init
        y_attn = _diff_attention(q, k, v, lam, attn_subln_g[l], lambda_init)
        y_pool = _pool_mixer(u, pool_w[l], pool_scale[l])
        y_lru = _rglru_mixer(xr, xg, conv_w[l], conv_b[l], lru_wa[l], lru_ba[l],
                             lru_wi[l], lru_bi[l], lru_lambda[l])
        mix = jnp.concatenate([y_attn, y_pool, y_lru], axis=-1) @ w_out[l]
        x = _layer_norm(alpha * x + mix, ln1_g[l], ln1_b[l])
        if l % 2 == 0:
            f = _swiglu(x, ffn_w_gate[l // 2], ffn_w_up[l // 2], ffn_w_down[l // 2])
        else:
            f = _moe_swiglu(x, router_w[l // 2], moe_w_gate[l // 2], moe_w_up[l // 2], moe_w_down[l // 2])
        x = _layer_norm(alpha * x + f, ln2_g[l], ln2_b[l])
    return x
```

```python
import functools
import math

import jax
import jax.numpy as jnp
from jax import lax
from jax.experimental import pallas as pl
from jax.experimental.pallas import tpu as pltpu

F32 = jnp.float32
BF16 = jnp.bfloat16

D_MODEL = 1024
CHUNK = 64
ATTN_HEADS = 4
ATTN_WIDTH = 512
ATTN_HEAD_DIM = 64
HEAD_WIDTH = 2 * ATTN_HEAD_DIM
POOL_WINDOWS = (2, 4, 8, 16)
POOL_WIDTH = 256
POOL_GROUP_DIM = 64
LRU_WIDTH = 256
LRU_C = 8.0
CONV_WIDTH = 4
REST_WIDTH = POOL_WIDTH + 2 * LRU_WIDTH
IN_WIDTH = 3 * ATTN_WIDTH + REST_WIDTH
MIX_WIDTH = POOL_WIDTH + LRU_WIDTH
D_FF = 2816
N_EXPERTS = 8
TOP_K = 2
LN_EPS = 1e-5
HEAD_NORM_EPS = 1e-5

LANES = 128
SUBLANES = 8
VMEM_LIMIT_BYTES = 56 * 1024 * 1024

ROW_TILE = 512
ATTN_TILE = 256
MIX_ROWS = 256
FFN_ROWS = 1024
FFN_COLS = 256
MOVE_ROWS = 256
NEG_BIG = -1e30


def _params(*semantics):
    return pltpu.CompilerParams(dimension_semantics=semantics, vmem_limit_bytes=VMEM_LIMIT_BYTES)


def _layer_norm(z, g, b):
    mu = jnp.mean(z, axis=-1, keepdims=True)
    zc = z - mu
    var = jnp.mean(zc * zc, axis=-1, keepdims=True)
    return zc * lax.rsqrt(var + LN_EPS) * g + b


def _inproj_body(x_ref, w_ref, q_ref, k_ref, v_ref, r_ref, wbf_ref):
    @pl.when(pl.program_id(0) == 0)
    def _cast_weights():
        for c in range(0, IN_WIDTH, 256):
            wbf_ref[:, c:c + 256] = w_ref[0, :, c:c + 256].astype(BF16)

    xb = x_ref[...].astype(BF16)

    def proj(c0, c1):
        return jnp.dot(xb, wbf_ref[:, c0:c1], preferred_element_type=F32)

    q_ref[...] = (proj(0, ATTN_WIDTH) * (ATTN_HEAD_DIM ** -0.5)).astype(BF16)
    k_ref[...] = proj(ATTN_WIDTH, 2 * ATTN_WIDTH).astype(BF16)
    v_ref[...] = proj(2 * ATTN_WIDTH, 3 * ATTN_WIDTH).astype(BF16)
    r_ref[...] = proj(3 * ATTN_WIDTH, IN_WIDTH)


def _inproj(x, w, layer):
    t = x.shape[0]
    row = lambda i: (i, 0)
    return pl.pallas_call(
        _inproj_body,
        grid=(t // ROW_TILE,),
        in_specs=[pl.BlockSpec((ROW_TILE, D_MODEL), row),
                  pl.BlockSpec((1, D_MODEL, IN_WIDTH), lambda i: (layer, 0, 0))],
        out_specs=[pl.BlockSpec((ROW_TILE, ATTN_WIDTH), row)] * 3
        + [pl.BlockSpec((ROW_TILE, REST_WIDTH), row)],
        out_shape=[jax.ShapeDtypeStruct((t, ATTN_WIDTH), BF16)] * 3
        + [jax.ShapeDtypeStruct((t, REST_WIDTH), F32)],
        scratch_shapes=[pltpu.VMEM((D_MODEL, IN_WIDTH), BF16)],
        compiler_params=_params("arbitrary"),
        name="inproj",
    )(x, w)


def _attn_body(lam_ref, g_ref, q_ref, k_ref, v_ref, o_ref, *, lambda_init, seq):
    tq = ATTN_TILE
    lv = lam_ref[...]
    lam = (jnp.exp(jnp.sum(lv[0:1] * lv[1:2], axis=-1, keepdims=True))
           - jnp.exp(jnp.sum(lv[2:3] * lv[3:4], axis=-1, keepdims=True)) + lambda_init)
    gain = g_ref[...] * (1.0 - lambda_init)

    lane = lax.broadcasted_iota(jnp.int32, (tq, HEAD_WIDTH), 1)
    first_map = lane < ATTN_HEAD_DIM
    row = lax.broadcasted_iota(jnp.int32, (2 * tq, tq), 0)
    col = lax.broadcasted_iota(jnp.int32, (2 * tq, tq), 1)
    q_chunk = jnp.where(row >= tq, row - tq, row) // CHUNK
    visible = (col // CHUNK) <= q_chunk

    def q_block(i, _):
        q = q_ref[pl.ds(pl.multiple_of(i * tq, tq), tq), :]
        zero = jnp.zeros_like(q)
        qq = jnp.concatenate([jnp.where(first_map, q, zero), jnp.where(first_map, zero, q)], axis=0)

        def step(j, carry, masked):
            m, l, acc = carry
            k0 = pl.multiple_of(j * tq, tq)
            s = lax.dot_general(qq, k_ref[pl.ds(k0, tq), :], (((1,), (1,)), ((), ())),
                                preferred_element_type=F32)
            if masked:
                s = jnp.where(visible, s, NEG_BIG)
            m_new = jnp.maximum(m, jnp.max(s, axis=-1, keepdims=True))
            p = jnp.exp(s - m_new)
            scale = jnp.exp(m - m_new)
            l_new = scale * l + jnp.sum(p, axis=-1, keepdims=True)
            pv = jnp.dot(p.astype(BF16), v_ref[pl.ds(k0, tq), :], preferred_element_type=F32)
            return m_new, l_new, scale * acc + pv

        init = (jnp.full((2 * tq, 1), NEG_BIG, F32), jnp.zeros((2 * tq, 1), F32),
                jnp.zeros((2 * tq, HEAD_WIDTH), F32))
        carry = lax.fori_loop(0, i, lambda j, c: step(j, c, False), init)
        _, l, acc = step(i, carry, True)
        o = acc[:tq] / l[:tq] - lam * (acc[tq:] / l[tq:])
        o = o * lax.rsqrt(jnp.mean(o * o, axis=-1, keepdims=True) + HEAD_NORM_EPS) * gain
        o_ref[pl.ds(pl.multiple_of(i * tq, tq), tq), :] = o.astype(BF16)
        return 0

    lax.fori_loop(0, seq // tq, q_block, 0)


def _attention(q, k, v, lam_params, subln_g, lambda_init, batch, seq):
    t = q.shape[0]
    blk = pl.BlockSpec((seq, HEAD_WIDTH), lambda b, h: (b, h))
    return pl.pallas_call(
        functools.partial(_attn_body, lambda_init=lambda_init, seq=seq),
        grid=(batch, ATTN_HEADS),
        in_specs=[pl.BlockSpec((4, ATTN_HEAD_DIM), lambda b, h: (0, 0)),
                  pl.BlockSpec((1, HEAD_WIDTH), lambda b, h: (0, 0)),
                  blk, blk, blk],
        out_specs=blk,
        out_shape=jax.ShapeDtypeStruct((t, ATTN_WIDTH), BF16),
        compiler_params=_params("arbitrary", "arbitrary"),
        name="diff_attention",
    )(lam_params, subln_g.reshape(1, HEAD_WIDTH), q, k, v)


def _mixer_body(r_ref, pw_ref, ps_ref, cw_ref, cb_ref, wa_ref, ba_ref, wi_ref, bi_ref, lam_ref, y_ref,
                *, seq):
    rows = MIX_ROWS
    pool_hist = max(POOL_WINDOWS)
    conv_hist = SUBLANES
    pw = pw_ref[...]
    wa = wa_ref[...]
    wi = wi_ref[...]
    ps, cb, ba, bi = ps_ref[...], cb_ref[...], ba_ref[...], bi_ref[...]
    cw = cw_ref[...]
    neg_c_softplus = -LRU_C * jax.nn.softplus(-lam_ref[...])

    lane_e = lax.broadcasted_iota(jnp.int32, (rows + pool_hist, POOL_WIDTH), 1)
    lane = lax.broadcasted_iota(jnp.int32, (rows, POOL_WIDTH), 1)
    row = lax.broadcasted_iota(jnp.int32, (rows, POOL_WIDTH), 0)
    win = jnp.where(lane < 64, 2, jnp.where(lane < 128, 4, jnp.where(lane < 192, 8, 16)))

    def chunk(c, carry):
        tail_u, tail_x, h_prev = carry
        r0 = pl.multiple_of(c * rows, rows)
        u = r_ref[pl.ds(r0, rows), 0:POOL_WIDTH]
        xr = r_ref[pl.ds(r0, rows), POOL_WIDTH:POOL_WIDTH + LRU_WIDTH]
        xg = r_ref[pl.ds(r0, rows), POOL_WIDTH + LRU_WIDTH:REST_WIDTH]

        ue = jnp.concatenate([tail_u, u], axis=0)
        w2 = ue + pltpu.roll(ue, 1, 0)
        w4 = w2 + pltpu.roll(w2, 2, 0)
        w8 = w4 + pltpu.roll(w4, 4, 0)
        w16 = w8 + pltpu.roll(w8, 8, 0)
        ws = jnp.where(lane_e < 64, w2, jnp.where(lane_e < 128, w4, jnp.where(lane_e < 192, w8, w16)))
        ws = ws[pool_hist:]
        count = jnp.minimum(r0 + row + 1, win).astype(F32)
        pooled = ws / count - u
        y_pool = jnp.dot(pooled.astype(BF16), pw, preferred_element_type=F32) * ps

        xe = jnp.concatenate([tail_x, xr], axis=0)
        xc = cb + pltpu.roll(xe, 3, 0)[conv_hist:] * cw[0:1]
        xc = xc + pltpu.roll(xe, 2, 0)[conv_hist:] * cw[1:2]
        xc = xc + pltpu.roll(xe, 1, 0)[conv_hist:] * cw[2:3]
        xc = xc + xr * cw[3:4]
        xcb = xc.astype(BF16)
        r_gate = jax.nn.sigmoid(jnp.dot(xcb, wa, preferred_element_type=F32) + ba)
        i_gate = jax.nn.sigmoid(jnp.dot(xcb, wi, preferred_element_type=F32) + bi)
        log_a = r_gate * neg_c_softplus
        a = jnp.exp(log_a)
        b = jnp.sqrt(-jnp.tanh(log_a) * (a * a + 1.0)) * (i_gate * xc)

        s = 1
        while s < rows:
            keep = row >= s
            a_prev = jnp.where(keep, pltpu.roll(a, s, 0), 1.0)
            b_prev = jnp.where(keep, pltpu.roll(b, s, 0), 0.0)
            b = a * b_prev + b
            a = a * a_prev
            s *= 2
        h = a * h_prev + b
        y_lru = h * jax.nn.gelu(xg)

        y_ref[pl.ds(r0, rows), 0:POOL_WIDTH] = y_pool.astype(BF16)
        y_ref[pl.ds(r0, rows), POOL_WIDTH:MIX_WIDTH] = y_lru.astype(BF16)
        return u[rows - pool_hist:], xr[rows - conv_hist:], h[rows - 1:rows]

    init = (jnp.zeros((pool_hist, POOL_WIDTH), F32), jnp.zeros((conv_hist, LRU_WIDTH), F32),
            jnp.zeros((1, LRU_WIDTH), F32))
    lax.fori_loop(0, seq // rows, chunk, init)


def _block_diag(w):
    g, c, d = w.shape
    eye = jnp.eye(g, dtype=w.dtype)
    return (eye[:, None, :, None] * w[:, :, None, :]).reshape(g * c, g * d)


def _mixer(rest, pool_w, pool_scale, conv_w, conv_b, wa, ba, wi, bi, lru_lambda, batch, seq):
    t = rest.shape[0]
    full = lambda shape: pl.BlockSpec(shape, lambda b: (0, 0))
    vec = lambda a: a.reshape(1, -1)
    return pl.pallas_call(
        functools.partial(_mixer_body, seq=seq),
        grid=(batch,),
        in_specs=[pl.BlockSpec((seq, REST_WIDTH), lambda b: (b, 0)),
                  full((POOL_WIDTH, POOL_WIDTH)), full((1, POOL_WIDTH)),
                  full((CONV_WIDTH, LRU_WIDTH)), full((1, LRU_WIDTH)),
                  full((LRU_WIDTH, LRU_WIDTH)), full((1, LRU_WIDTH)),
                  full((LRU_WIDTH, LRU_WIDTH)), full((1, LRU_WIDTH)),
                  full((1, LRU_WIDTH))],
        out_specs=pl.BlockSpec((seq, MIX_WIDTH), lambda b: (b, 0)),
        out_shape=jax.ShapeDtypeStruct((t, MIX_WIDTH), BF16),
        compiler_params=_params("arbitrary"),
        name="pool_lru_mixer",
    )(rest, _block_diag(pool_w).astype(BF16), vec(pool_scale), conv_w, vec(conv_b),
      _block_diag(wa).astype(BF16), vec(ba), _block_diag(wi).astype(BF16), vec(bi), vec(lru_lambda))


def _outproj_body(ya_ref, ym_ref, x_ref, w_ref, g_ref, b_ref, o_ref, wbf_ref, *, alpha):
    @pl.when(pl.program_id(0) == 0)
    def _cast_weights():
        for c in range(0, D_MODEL, 256):
            wbf_ref[:, c:c + 256] = w_ref[0, :, c:c + 256].astype(BF16)

    mix =jnp.dot(ya_ref[...], wbf_ref[0:ATTN_WIDTH, :], preferred_element_type=F32)
    mix = mix + jnp.dot(ym_ref[...], wbf_ref[ATTN_WIDTH:, :], preferred_element_type=F32)
    o_ref[...] = _layer_norm(alpha * x_ref[...] + mix, g_ref[...], b_ref[...])


def _outproj(y_attn, y_mix, x, w, layer, g, b, alpha):
    t = x.shape[0]
    row = lambda i: (i, 0)
    const = lambda i: (0, 0)
    return pl.pallas_call(
        functools.partial(_outproj_body, alpha=alpha),
        grid=(t // ROW_TILE,),
        in_specs=[pl.BlockSpec((ROW_TILE, ATTN_WIDTH), row), pl.BlockSpec((ROW_TILE, MIX_WIDTH), row),
                  pl.BlockSpec((ROW_TILE, D_MODEL), row),
                  pl.BlockSpec((1, D_MODEL, D_MODEL), lambda i: (layer, 0, 0)),
                  pl.BlockSpec((1, D_MODEL), const), pl.BlockSpec((1, D_MODEL), const)],
        out_specs=pl.BlockSpec((ROW_TILE, D_MODEL), row),
        out_shape=jax.ShapeDtypeStruct((t, D_MODEL), F32),
        scratch_shapes=[pltpu.VMEM((D_MODEL, D_MODEL), BF16)],
        compiler_params=_params("arbitrary"),
        name="outproj_ln",
    )(y_attn, y_mix, x, w, g.reshape(1, -1), b.reshape(1, -1))


def _ffn_body(be_ref, nu_ref, x_ref, wg_ref, wu_ref, wd_ref, *rest, alpha, fuse_ln):
    if fuse_ln:
        g_ref, b_ref, o_ref, xb_ref, acc_ref = rest
    else:
        o_ref, xb_ref, acc_ref = rest
    i = pl.program_id(0)
    j = pl.program_id(1)
    used = i < nu_ref[0]

    @pl.when(jnp.logical_and(used, j == 0))
    def _start():
        xb_ref[...] = x_ref[...].astype(BF16)
        acc_ref[...] = jnp.zeros_like(acc_ref)

    @pl.when(used)
    def _accumulate():
        xb = xb_ref[...]
        gate = jnp.dot(xb, wg_ref[0].astype(BF16), preferred_element_type=F32)
        up = jnp.dot(xb, wu_ref[0].astype(BF16), preferred_element_type=F32)
        hidden = (jax.nn.silu(gate) * up).astype(BF16)
        acc_ref[...] += jnp.dot(hidden, wd_ref[0].astype(BF16), preferred_element_type=F32)

    @pl.when(jnp.logical_and(used, j == pl.num_programs(1) - 1))
    def _finish():
        if fuse_ln:
            o_ref[...] = _layer_norm(alpha * x_ref[...] + acc_ref[...], g_ref[...], b_ref[...])
        else:
            o_ref[...] = acc_ref[...]

    @pl.when(jnp.logical_and(jnp.logical_not(used), j == 0))
    def _empty():
        o_ref[...] = jnp.zeros_like(o_ref)


def _ffn(block_e, n_used, rows, w_gate, w_up, w_down, ln=None, alpha=1.0):
    r = rows.shape[0]
    nj = D_FF // FFN_COLS

    def col(i, j, be, nu):
        return jnp.where(i < nu[0], j, nj - 1)

    def blk(i, be, nu):
        return jnp.minimum(i, nu[0] - 1)

    row = lambda i, j, be, nu: (i, 0)
    in_specs = [pl.BlockSpec((FFN_ROWS, D_MODEL), row),
                pl.BlockSpec((1, D_MODEL, FFN_COLS), lambda i, j, be, nu: (be[blk(i, be, nu)], 0, col(i, j, be, nu))),
                pl.BlockSpec((1, D_MODEL, FFN_COLS), lambda i, j, be, nu: (be[blk(i, be, nu)], 0, col(i, j, be, nu))),
                pl.BlockSpec((1, FFN_COLS, D_MODEL), lambda i, j, be, nu: (be[blk(i, be, nu)], col(i, j, be, nu), 0))]
    args = [rows, w_gate, w_up, w_down]
    if ln is not None:
        in_specs += [pl.BlockSpec((1, D_MODEL), lambda i, j, be, nu: (0, 0))] * 2
        args += [ln[0].reshape(1, -1), ln[1].reshape(1, -1)]
    return pl.pallas_call(
        functools.partial(_ffn_body, alpha=alpha, fuse_ln=ln is not None),
        grid_spec=pltpu.PrefetchScalarGridSpec(
            num_scalar_prefetch=2,
            grid=(r // FFN_ROWS, nj),
            in_specs=in_specs,
            out_specs=pl.BlockSpec((FFN_ROWS, D_MODEL), row),
            scratch_shapes=[pltpu.VMEM((FFN_ROWS, D_MODEL), BF16), pltpu.VMEM((FFN_ROWS, D_MODEL), F32)]),
        out_shape=jax.ShapeDtypeStruct((r, D_MODEL), F32),
        compiler_params=_params("arbitrary", "arbitrary"),
        name="swiglu_ln" if ln is not None else "swiglu_experts",
    )(block_e, n_used, *args)


def _router_body(x_ref, w_ref, meta_ref, gate_ref, cnt_ref, run_ref):
    tm = x_ref.shape[0]

    @pl.when(pl.program_id(0) == 0)
    def _init():
        run_ref[...] = jnp.zeros_like(run_ref)

    logits = jnp.dot(x_ref[...], w_ref[...], preferred_element_type=F32, precision=lax.Precision.HIGHEST)
    lane_i = lax.broadcasted_iota(jnp.int32, (tm, LANES), 1)
    lane = lane_i.astype(F32)
    logits = jnp.where(lane_i < N_EXPERTS, logits, -jnp.inf)
    m1 = jnp.max(logits, axis=-1, keepdims=True)
    e1 = jnp.min(jnp.where(logits == m1, lane, float(LANES)), axis=-1, keepdims=True)
    rest = jnp.where(lane == e1, -jnp.inf, logits)
    m2 = jnp.max(rest, axis=-1, keepdims=True)
    e2 = jnp.min(jnp.where(rest == m2, lane, float(LANES)), axis=-1, keepdims=True)
    ex = jnp.exp(m2 - m1)
    g1 = 1.0 / (1.0 + ex)
    g2 = ex / (1.0 + ex)

    chosen = jnp.logical_or(lane == e1, lane == e2)
    r_i = lax.broadcasted_iota(jnp.int32, (tm, tm), 0)
    c_i = lax.broadcasted_iota(jnp.int32, (tm, tm), 1)
    earlier = (c_i < r_i).astype(BF16)
    before = jnp.dot(earlier, chosen.astype(BF16), preferred_element_type=F32) + run_ref[...]
    rank1 = jnp.sum(jnp.where(lane == e1, before, 0.0), axis=-1, keepdims=True)
    rank2 = jnp.sum(jnp.where(lane == e2, before, 0.0), axis=-1, keepdims=True)
    total = run_ref[...] + jnp.sum(chosen.astype(F32), axis=0, keepdims=True)
    run_ref[...] = total
    cnt_ref[...] = jnp.broadcast_to(total, cnt_ref.shape)
    meta = jnp.where(lane_i == 0, e1, jnp.where(lane_i == 1, e2, jnp.where(lane_i == 2, rank1, rank2)))
    meta_ref[...] = meta.astype(jnp.int32)
    gate_ref[...] = jnp.where(lane_i == 0, g1, g2)


def _router(x, w_router):
    t = x.shape[0]
    w_pad = jnp.zeros((D_MODEL, LANES), F32).at[:, :N_EXPERTS].set(w_router)
    row = lambda i: (i, 0)
    return pl.pallas_call(
        _router_body,
        grid=(t // ROW_TILE,),
        in_specs=[pl.BlockSpec((ROW_TILE, D_MODEL), row), pl.BlockSpec((D_MODEL, LANES), lambda i: (0, 0))],
        out_specs=[pl.BlockSpec((ROW_TILE, LANES), row), pl.BlockSpec((ROW_TILE, LANES), row),
                   pl.BlockSpec((SUBLANES, LANES), lambda i: (0, 0))],
        out_shape=[jax.ShapeDtypeStruct((t, LANES), jnp.int32), jax.ShapeDtypeStruct((t, LANES), F32),
                   jax.ShapeDtypeStruct((SUBLANES, LANES), F32)],
        scratch_shapes=[pltpu.VMEM((1, LANES), F32)],
        compiler_params=_params("arbitrary"),
        name="router_top2",
    )(x, w_pad)


def _scatter_body(d1_ref, d2_ref, x_hbm, zero_hbm, rows_hbm, sem):
    del zero_hbm
    base = pl.program_id(0) * MOVE_ROWS

    def copy(t, dest):
        return pltpu.make_async_copy(x_hbm.at[pl.ds(base + t, 1)], rows_hbm.at[pl.ds(dest, 1)], sem)

    def start(t, _):
        copy(t, d1_ref[0, 0, t]).start()
        copy(t, d2_ref[0, 0, t]).start()
        return 0

    def wait(t, _):
        copy(t, d1_ref[0, 0, t]).wait()
        copy(t, d2_ref[0, 0, t]).wait()
        return 0

    lax.fori_loop(0, MOVE_ROWS, start, 0)
    lax.fori_loop(0, MOVE_ROWS, wait, 0)


def _scatter_rows(x, dest1, dest2, n_rows):
    t = x.shape[0]
    steps = t // MOVE_ROWS
    idx = pl.BlockSpec((1, 1, MOVE_ROWS), lambda i: (i, 0, 0), memory_space=pltpu.SMEM)
    any_spec = pl.BlockSpec(memory_space=pl.ANY)
    return pl.pallas_call(
        _scatter_body,
        grid=(steps,),
        in_specs=[idx, idx, any_spec, any_spec],
        out_specs=any_spec,
        out_shape=jax.ShapeDtypeStruct((n_rows, D_MODEL), F32),
        scratch_shapes=[pltpu.SemaphoreType.DMA(())],
        input_output_aliases={3: 0},
        compiler_params=_params("arbitrary"),
        name="scatter_rows",
    )(dest1.reshape(steps, 1, MOVE_ROWS), dest2.reshape(steps, 1, MOVE_ROWS), x,
      jnp.zeros((n_rows, D_MODEL), F32))


def _combine_body(d1_ref, d2_ref, ys_hbm, gate_ref, x_ref, g_ref, b_ref, o_ref, buf_ref, sem, *, alpha):
    def copy(t, dest, k):
        return pltpu.make_async_copy(ys_hbm.at[pl.ds(dest, 1)], buf_ref.at[k, pl.ds(t, 1)], sem)

    def start(t, _):
        copy(t, d1_ref[0, 0, t], 0).start()
        copy(t, d2_ref[0, 0, t], 1).start()
        return 0

    def wait(t, _):
        copy(t, d1_ref[0, 0, t], 0).wait()
        copy(t, d2_ref[0, 0, t], 1).wait()
        return 0

    lax.fori_loop(0, MOVE_ROWS, start, 0)
    lax.fori_loop(0, MOVE_ROWS, wait, 0)
    gates = gate_ref[...]
    f = gates[:, 0:1] * buf_ref[0] + gates[:, 1:2] * buf_ref[1]
    o_ref[...] = _layer_norm(alpha * x_ref[...] + f, g_ref[...], b_ref[...])


def _combine(ys, dest1, dest2, gates, x, g, b, alpha):
    t = x.shape[0]
    steps = t // MOVE_ROWS
    idx = pl.BlockSpec((1, 1, MOVE_ROWS), lambda i: (i, 0, 0), memory_space=pltpu.SMEM)
    row = lambda i: (i, 0)
    const = lambda i: (0, 0)
    return pl.pallas_call(
        functools.partial(_combine_body, alpha=alpha),
        grid=(steps,),
        in_specs=[idx, idx, pl.BlockSpec(memory_space=pl.ANY),
                  pl.BlockSpec((MOVE_ROWS, LANES), row), pl.BlockSpec((MOVE_ROWS, D_MODEL), row),
                  pl.BlockSpec((1, D_MODEL), const), pl.BlockSpec((1, D_MODEL), const)],
        out_specs=pl.BlockSpec((MOVE_ROWS, D_MODEL), row),
        out_shape=jax.ShapeDtypeStruct((t, D_MODEL), F32),
        scratch_shapes=[pltpu.VMEM((TOP_K, MOVE_ROWS, D_MODEL), F32), pltpu.SemaphoreType.DMA(())],
        compiler_params=_params("arbitrary"),
        name="combine_ln",
    )(dest1.reshape(steps, 1, MOVE_ROWS), dest2.reshape(steps, 1, MOVE_ROWS), ys, gates, x,
      g.reshape(1, -1), b.reshape(1, -1))


def _moe(x1, w_router, w_gate, w_up, w_down, first_expert, g, b, alpha):
    t = x1.shape[0]
    meta, gates, counts = _router(x1, w_router)
    e1, e2, rank1, rank2 = meta[:, 0], meta[:, 1], meta[:, 2], meta[:, 3]
    sizes = counts[0, :N_EXPERTS].astype(jnp.int32)
    padded = (sizes + FFN_ROWS - 1) // FFN_ROWS * FFN_ROWS
    group_end = jnp.cumsum(padded)
    group_start = group_end - padded
    dest1 = group_start[e1] + rank1
    dest2 = group_start[e2] + rank2
    n_blocks = (t * TOP_K) // FFN_ROWS + N_EXPERTS
    block_start = jnp.arange(n_blocks, dtype=jnp.int32) * FFN_ROWS
    block_e = jnp.minimum(jnp.sum(group_end[None, :] <= block_start[:, None], axis=1), N_EXPERTS - 1)
    n_used = (group_end[-1] // FFN_ROWS).reshape(1)
    rows = _scatter_rows(x1, dest1, dest2, n_blocks * FFN_ROWS)
    ys = _ffn((block_e + first_expert).astype(jnp.int32), n_used.astype(jnp.int32), rows, w_gate, w_up, w_down)
    return _combine(ys, dest1, dest2, gates, x1, g, b, alpha)


def kernel(x, w_in, w_out, attn_lambda, attn_subln_g, pool_w, pool_scale, conv_w, conv_b, lru_wa, lru_ba,
           lru_wi, lru_bi, lru_lambda, ln1_g, ln1_b, ln2_g, ln2_b, ffn_w_gate, ffn_w_up, ffn_w_down,
           router_w, moe_w_gate, moe_w_up, moe_w_down):
    batch, seq, d = x.shape
    depth = w_in.shape[0]
    assert d == D_MODEL and seq % ATTN_TILE == 0 and seq % MIX_ROWS == 0
    t = batch * seq
    assert t % FFN_ROWS == 0 and t % ROW_TILE == 0
    alpha = (2.0 * depth) ** 0.25
    dense_used = jnp.full((1,), t // FFN_ROWS, jnp.int32)
    moe_gate = moe_w_gate.reshape(-1, D_MODEL, D_FF)
    moe_up = moe_w_up.reshape(-1, D_MODEL, D_FF)
    moe_down = moe_w_down.reshape(-1, D_FF, D_MODEL)
    xt = x.reshape(t, d)
    for l in range(depth):
        lambda_init = 0.8 - 0.6 * math.exp(-0.3 * l)
        q, k, v, rest = _inproj(xt, w_in, l)
        y_attn = _attention(q, k, v, attn_lambda[l], attn_subln_g[l], lambda_init, batch, seq)
        y_mix = _mixer(rest, pool_w[l], pool_scale[l], conv_w[l], conv_b[l], lru_wa[l], lru_ba[l],
                       lru_wi[l], lru_bi[l], lru_lambda[l], batch, seq)
        x1 = _outproj(y_attn, y_mix, xt, w_out, l, ln1_g[l], ln1_b[l], alpha)
        if l % 2 == 0:
            dense_blocks = jnp.full((t // FFN_ROWS,), l // 2, jnp.int32)
            xt = _ffn(dense_blocks, dense_used, x1, ffn_w_gate, ffn_w_up, ffn_w_down,
                      ln=(ln2_g[l], ln2_b[l]), alpha=alpha)
        else:
            xt = _moe(x1, router_w[l // 2], moe_gate, moe_up, moe_down, (l // 2) * N_EXPERTS,
                      ln2_g[l], ln2_b[l], alpha)
    return xt.reshape(batch, seq, d)
```

```python
import functools
import math

import jax
import jax.numpy as jnp
from jax import lax
from jax.experimental import pallas as pl
from jax.experimental.pallas import tpu as pltpu

F32 = jnp.float32
BF16 = jnp.bfloat16

D_MODEL = 1024
CHUNK = 64
ATTN_HEADS = 4
ATTN_WIDTH = 512
ATTN_HEAD_DIM = 64
HEAD_WIDTH = 2 * ATTN_HEAD_DIM
POOL_WINDOWS = (2, 4, 8, 16)
POOL_WIDTH = 256
POOL_GROUP_DIM = 64
LRU_WIDTH = 256
LRU_C = 8.0
CONV_WIDTH = 4
REST_WIDTH = POOL_WIDTH + 2 * LRU_WIDTH
IN_WIDTH = 3 * ATTN_WIDTH + REST_WIDTH
MIX_WIDTH = POOL_WIDTH + LRU_WIDTH
D_FF = 2816
N_EXPERTS = 8
TOP_K = 2
LN_EPS = 1e-5
HEAD_NORM_EPS = 1e-5

LANES = 128
SUBLANES = 8
VMEM_LIMIT_BYTES = 56 * 1024 * 1024

ROW_TILE = 512
ATTN_TILE = 256
ATTN_HEADS_PER_STEP = 4
MIX_ROWS = 256
FFN_ROWS = 1024
FFN_COLS = 256
MOVE_ROWS = 256
NEG_BIG = -1e30


def _params(*semantics):
    return pltpu.CompilerParams(dimension_semantics=semantics, vmem_limit_bytes=VMEM_LIMIT_BYTES)


def _layer_norm(z, g, b):
    mu = jnp.mean(z, axis=-1, keepdims=True)
    zc = z - mu
    var = jnp.mean(zc * zc, axis=-1, keepdims=True)
    return zc * lax.rsqrt(var + LN_EPS) * g + b


def _inproj_body(x_ref, w_ref, q_ref, k_ref, v_ref, r_ref, wbf_ref):
    @pl.when(pl.program_id(0) == 0)
    def _cast_weights():
        for c in range(0, IN_WIDTH, 256):
            wbf_ref[:, c:c + 256] = w_ref[0, :, c:c + 256].astype(BF16)

    xb = x_ref[...].astype(BF16)

    def proj(c0, c1):
        return jnp.dot(xb, wbf_ref[:, c0:c1], preferred_element_type=F32)

    q_ref[...] = (proj(0, ATTN_WIDTH) * (ATTN_HEAD_DIM ** -0.5)).astype(BF16)
    k_ref[...] = proj(ATTN_WIDTH, 2 * ATTN_WIDTH).astype(BF16)
    v_ref[...] = proj(2 * ATTN_WIDTH, 3 * ATTN_WIDTH).astype(BF16)
    r_ref[...] = proj(3 * ATTN_WIDTH, IN_WIDTH)


def _inproj(x, w, layer):
    t = x.shape[0]
    row = lambda i: (i, 0)
    return pl.pallas_call(
        _inproj_body,
        grid=(t // ROW_TILE,),
        in_specs=[pl.BlockSpec((ROW_TILE, D_MODEL), row),
                  pl.BlockSpec((1, D_MODEL, IN_WIDTH), lambda i: (layer, 0, 0))],
        out_specs=[pl.BlockSpec((ROW_TILE, ATTN_WIDTH), row)] * 3
        + [pl.BlockSpec((ROW_TILE, REST_WIDTH), row)],
        out_shape=[jax.ShapeDtypeStruct((t, ATTN_WIDTH), BF16)] * 3
        + [jax.ShapeDtypeStruct((t, REST_WIDTH), F32)],
        scratch_shapes=[pltpu.VMEM((D_MODEL, IN_WIDTH), BF16)],
        compiler_params=_params("arbitrary"),
        name="inproj",
    )(x, w)


def _attn_body(lam_ref, g_ref, q_ref, k_ref, v_ref, o_ref, vt_ref, *, lambda_init, seq):
    tq = ATTN_TILE
    lv = lam_ref[...]
    lam = (jnp.exp(jnp.sum(lv[0:1] * lv[1:2], axis=-1, keepdims=True))
           - jnp.exp(jnp.sum(lv[2:3] * lv[3:4], axis=-1, keepdims=True)) + lambda_init)
    gain = g_ref[...] * (1.0 - lambda_init)

    heads = range(ATTN_HEADS_PER_STEP)
    cols = [slice(g * HEAD_WIDTH, (g + 1) * HEAD_WIDTH) for g in heads]
    for g in heads:
        for c in range(seq // tq):
            vt_ref[g, c] = v_ref[c * tq:(c + 1) * tq, cols[g]].astype(F32).T.astype(BF16)

    lane = lax.broadcasted_iota(jnp.int32, (tq, HEAD_WIDTH), 1)
    first_map = lane < ATTN_HEAD_DIM
    key = lax.broadcasted_iota(jnp.int32, (tq, 2 * tq), 0)
    qry = lax.broadcasted_iota(jnp.int32, (tq, 2 * tq), 1)
    q_chunk = jnp.where(qry >= tq, qry - tq, qry) // CHUNK
    visible = (key // CHUNK) <= q_chunk

    def q_block(i, _):
        q0 = pl.multiple_of(i * tq, tq)
        qqs = []
        for g in heads:
            q = q_ref[pl.ds(q0, tq), cols[g]]
            zero = jnp.zeros_like(q)
            qqs.append(jnp.concatenate([jnp.where(first_map, q, zero), jnp.where(first_map, zero, q)], axis=0))

        def step(j, carry, masked):
            k0 = pl.multiple_of(j * tq, tq)
            out = []
            scores = [lax.dot_general(k_ref[pl.ds(k0, tq), cols[g]], qqs[g], (((1,), (1,)), ((), ())),
                                      preferred_element_type=F32) for g in heads]
            for g in heads:
                m, l, acc = carry[g]
                s = scores[g]
                if masked:
                    s = jnp.where(visible, s, NEG_BIG)
                m_new = jnp.maximum(m, jnp.max(s, axis=0, keepdims=True))
                p = jnp.exp(s - m_new)
                scale = jnp.exp(m - m_new)
                l_new = scale * l + jnp.sum(p, axis=0, keepdims=True)
                pv = jnp.dot(vt_ref[g, j], p.astype(BF16), preferred_element_type=F32)
                out.append((m_new, l_new, scale * acc + pv))
            return tuple(out)

        init = tuple((jnp.full((1, 2 * tq), NEG_BIG, F32), jnp.zeros((1, 2 * tq), F32),
                      jnp.zeros((HEAD_WIDTH, 2 * tq), F32)) for _ in heads)
        carry = lax.fori_loop(0, i, lambda j, c: step(j, c, False), init)
        carry = step(i, carry, True)
        for g in heads:
            _, l, acc = carry[g]
            o = acc[:, :tq] / l[:, :tq] - lam * (acc[:, tq:] / l[:, tq:])
            o = o * lax.rsqrt(jnp.mean(o * o, axis=0, keepdims=True) + HEAD_NORM_EPS)
            o_ref[pl.ds(q0, tq), cols[g]] = (o.T * gain).astype(BF16)
        return 0

    lax.fori_loop(0, seq // tq, q_block, 0)


def _attention(q, k, v, lam_params, subln_g, lambda_init, batch, seq):
    t = q.shape[0]
    blk = pl.BlockSpec((seq, ATTN_HEADS_PER_STEP * HEAD_WIDTH), lambda b, h: (b, h))
    return pl.pallas_call(
        functools.partial(_attn_body, lambda_init=lambda_init, seq=seq),
        grid=(batch, ATTN_HEADS // ATTN_HEADS_PER_STEP),
        in_specs=[pl.BlockSpec((4, ATTN_HEAD_DIM), lambda b, h: (0, 0)),
                  pl.BlockSpec((1, HEAD_WIDTH), lambda b, h: (0, 0)),
                  blk, blk, blk],
        out_specs=blk,
        out_shape=jax.ShapeDtypeStruct((t, ATTN_WIDTH), BF16),
        scratch_shapes=[pltpu.VMEM((ATTN_HEADS_PER_STEP, seq // ATTN_TILE, HEAD_WIDTH, ATTN_TILE), BF16)],
        compiler_params=_params("arbitrary", "arbitrary"),
        name="diff_attention",
    )(lam_params, subln_g.reshape(1, HEAD_WIDTH), q, k, v)


def _mixer_body(r_ref, pw_ref, ps_ref, cw_ref, cb_ref, wa_ref, ba_ref, wi_ref, bi_ref, lam_ref, y_ref,
                *, seq):
    rows = MIX_ROWS
    pool_hist = max(POOL_WINDOWS)
    conv_hist = SUBLANES
    pw = pw_ref[...]
    wa = wa_ref[...]
    wi = wi_ref[...]
    ps, cb, ba, bi = ps_ref[...], cb_ref[...], ba_ref[...], bi_ref[...]
    cw = cw_ref[...]
    neg_c_softplus = -LRU_C * jax.nn.softplus(-lam_ref[...])

    lane_e = lax.broadcasted_iota(jnp.int32, (rows + pool_hist, POOL_WIDTH), 1)
    lane = lax.broadcasted_iota(jnp.int32, (rows, POOL_WIDTH), 1)
    row = lax.broadcasted_iota(jnp.int32, (rows, POOL_WIDTH), 0)
    win = jnp.where(lane < 64, 2, jnp.where(lane < 128, 4, jnp.where(lane < 192, 8, 16)))

    def chunk(c, carry):
        tail_u, tail_x, h_prev = carry
        r0 = pl.multiple_of(c * rows, rows)
        u = r_ref[pl.ds(r0, rows), 0:POOL_WIDTH]
        xr = r_ref[pl.ds(r0, rows), POOL_WIDTH:POOL_WIDTH + LRU_WIDTH]
        xg = r_ref[pl.ds(r0, rows), POOL_WIDTH + LRU_WIDTH:REST_WIDTH]

        ue = jnp.concatenate([tail_u, u], axis=0)
        w2 = ue + pltpu.roll(ue, 1, 0)
        w4 = w2 + pltpu.roll(w2, 2, 0)
        w8 = w4 + pltpu.roll(w4, 4, 0)
        w16 = w8 + pltpu.roll(w8, 8, 0)
        ws = jnp.where(lane_e < 64, w2, jnp.where(lane_e < 128, w4, jnp.where(lane_e < 192, w8, w16)))
        ws = ws[pool_hist:]
        count = jnp.minimum(r0 + row + 1, win).astype(F32)
        pooled = ws / count - u
        y_pool = jnp.dot(pooled.astype(BF16), pw, preferred_element_type=F32) * ps

        xe = jnp.concatenate([tail_x, xr], axis=0)
        xc = cb + pltpu.roll(xe, 3, 0)[conv_hist:] * cw[0:1]
        xc = xc + pltpu.roll(xe, 2, 0)[conv_hist:] * cw[1:2]
        xc = xc + pltpu.roll(xe, 1, 0)[conv_hist:] * cw[2:3]
        xc = xc + xr * cw[3:4]
        xcb = xc.astype(BF16)
        r_gate = jax.nn.sigmoid(jnp.dot(xcb, wa, preferred_element_type=F32) + ba)
        i_gate = jax.nn.sigmoid(jnp.dot(xcb, wi, preferred_element_type=F32) + bi)
        log_a = r_gate * neg_c_softplus
        a = jnp.exp(log_a)
        b = jnp.sqrt(-jnp.tanh(log_a) * (a * a + 1.0)) * (i_gate * xc)

        s = 1
        while s < rows:
            keep = row >= s
            a_prev = jnp.where(keep, pltpu.roll(a, s, 0), 1.0)
            b_prev = jnp.where(keep, pltpu.roll(b, s, 0), 0.0)
            b = a * b_prev + b
            a = a * a_prev
            s *= 2
        h = a * h_prev + b
        y_lru = h * jax.nn.gelu(xg)

        y_ref[pl.ds(r0, rows), 0:POOL_WIDTH] = y_pool.astype(BF16)
        y_ref[pl.ds(r0, rows), POOL_WIDTH:MIX_WIDTH] = y_lru.astype(BF16)
        return u[rows - pool_hist:], xr[rows - conv_hist:], h[rows - 1:rows]

    init = (jnp.zeros((pool_hist, POOL_WIDTH), F32), jnp.zeros((conv_hist, LRU_WIDTH), F32),
            jnp.zeros((1, LRU_WIDTH), F32))
    lax.fori_loop(0, seq // rows, chunk, init)


def _block_diag(w):
    g, c, d = w.shape
    eye = jnp.eye(g, dtype=w.dtype)
    return (eye[:, None, :, None] * w[:, :, None, :]).reshape(g * c, g * d)


def _mixer(rest, pool_w, pool_scale, conv_w, conv_b, wa, ba, wi, bi, lru_lambda, batch, seq):
    t = rest.shape[0]
    full = lambda shape: pl.BlockSpec(shape, lambda b: (0, 0))
    vec = lambda a: a.reshape(1, -1)
    return pl.pallas_call(
        functools.partial(_mixer_body, seq=seq),
        grid=(batch,),
        in_specs=[pl.BlockSpec((seq, REST_WIDTH), lambda b: (b, 0)),
                  full((POOL_WIDTH, POOL_WIDTH)), full((1, POOL_WIDTH)),
                  full((CONV_WIDTH, LRU_WIDTH)), full((1, LRU_WIDTH)),
                  full((LRU_WIDTH, LRU_WIDTH)), full((1, LRU_WIDTH)),
                  full((LRU_WIDTH, LRU_WIDTH)), full((1, LRU_WIDTH)),
                  full((1, LRU_WIDTH))],
        out_specs=pl.BlockSpec((seq, MIX_WIDTH), lambda b: (b, 0)),
        out_shape=jax.ShapeDtypeStruct((t, MIX_WIDTH), BF16),
        compiler_params=_params("arbitrary"),
        name="pool_lru_mixer",
    )(rest, _block_diag(pool_w).astype(BF16), vec(pool_scale), conv_w, vec(conv_b),
      _block_diag(wa).astype(BF16), vec(ba), _block_diag(wi).astype(BF16), vec(bi), vec(lru_lambda))


def _outproj_body(ya_ref, ym_ref, x_ref, w_ref, g_ref, b_ref, o_ref, wbf_ref, *, alpha):
    @pl.when(pl.program_id(0) == 0)
    def _cast_weights():
        for c in range(0, D_MODEL, 256):
            wbf_ref[:, c:c + 256] = w_ref[0, :, c:c + 256].astype(BF16)

    mix =jnp.dot(ya_ref[...], wbf_ref[0:ATTN_WIDTH, :], preferred_element_type=F32)
    mix = mix + jnp.dot(ym_ref[...], wbf_ref[ATTN_WIDTH:, :], preferred_element_type=F32)
    o_ref[...] = _layer_norm(alpha * x_ref[...] + mix, g_ref[...], b_ref[...])


def _outproj(y_attn, y_mix, x, w, layer, g, b, alpha):
    t = x.shape[0]
    row = lambda i: (i, 0)
    const = lambda i: (0, 0)
    return pl.pallas_call(
        functools.partial(_outproj_body, alpha=alpha),
        grid=(t // ROW_TILE,),
        in_specs=[pl.BlockSpec((ROW_TILE, ATTN_WIDTH), row), pl.BlockSpec((ROW_TILE, MIX_WIDTH), row),
                  pl.BlockSpec((ROW_TILE, D_MODEL), row),
                  pl.BlockSpec((1, D_MODEL, D_MODEL), lambda i: (layer, 0, 0)),
                  pl.BlockSpec((1, D_MODEL), const), pl.BlockSpec((1, D_MODEL), const)],
        out_specs=pl.BlockSpec((ROW_TILE, D_MODEL), row),
        out_shape=jax.ShapeDtypeStruct((t, D_MODEL), F32),
        scratch_shapes=[pltpu.VMEM((D_MODEL, D_MODEL), BF16)],
        compiler_params=_params("arbitrary"),
        name="outproj_ln",
    )(y_attn, y_mix, x, w, g.reshape(1, -1), b.reshape(1, -1))


def _gather_copies(src_ref, x_hbm, buf_ref, sem, slot):
    def copy(r):
        return pltpu.make_async_copy(x_hbm.at[pl.ds(src_ref[0, 0, r], 1)], buf_ref.at[slot, pl.ds(r, 1)],
                                     sem.at[slot])
    return copy


def _ffn_body(be_ref, nu_ref, *rest, alpha, fuse_ln, gather):
    if gather:
        src0_ref, src_next_ref, x_hbm, wg_ref, wu_ref, wd_ref, o_ref, xb_ref, acc_ref, buf_ref, sem = rest
    elif fuse_ln:
        x_ref, wg_ref, wu_ref, wd_ref, g_ref, b_ref, o_ref, xb_ref, acc_ref = rest
    else:
        x_ref, wg_ref, wu_ref, wd_ref, o_ref, xb_ref, acc_ref = rest
    i = pl.program_id(0)
    j = pl.program_id(1)
    n_used = nu_ref[0]
    used = i < n_used

    if gather:
        n_rows = buf_ref.shape[1]
        slot = lax.rem(i, 2)

        def start_block(src_ref, to_slot):
            copy = _gather_copies(src_ref, x_hbm, buf_ref, sem, to_slot)

            def body(r, _):
                copy(r).start()
                return 0
            lax.fori_loop(0, n_rows, body, 0)

        @pl.when(jnp.logical_and(used, jnp.logical_and(i == 0, j == 0)))
        def _first_rows():
            start_block(src0_ref, 0)

        @pl.when(jnp.logical_and(used, j == 0))
        def _start():
            pltpu.make_async_copy(buf_ref.at[slot], buf_ref.at[slot], sem.at[slot]).wait()

            @pl.when(i + 1 < n_used)
            def _next_rows():
                start_block(src_next_ref, 1 - slot)

            xb_ref[...] = buf_ref[slot].astype(BF16)
            acc_ref[...] = jnp.zeros_like(acc_ref)
    else:
        @pl.when(jnp.logical_and(used, j == 0))
        def _start():
            xb_ref[...] = x_ref[...].astype(BF16)
            acc_ref[...] = jnp.zeros_like(acc_ref)

    @pl.when(used)
    def _accumulate():
        xb = xb_ref[...]
        gate = jnp.dot(xb, wg_ref[0].astype(BF16), preferred_element_type=F32)
        up = jnp.dot(xb, wu_ref[0].astype(BF16), preferred_element_type=F32)
        hidden = (jax.nn.silu(gate) * up).astype(BF16)
        acc_ref[...] += jnp.dot(hidden, wd_ref[0].astype(BF16), preferred_element_type=F32)

    @pl.when(jnp.logical_and(used, j == pl.num_programs(1) - 1))
    def _finish():
        if fuse_ln:
            o_ref[...] = _layer_norm(alpha * x_ref[...] + acc_ref[...], g_ref[...], b_ref[...])
        else:
            o_ref[...] = acc_ref[...]

    @pl.when(jnp.logical_and(jnp.logical_not(used), j == 0))
    def _empty():
        o_ref[...] = jnp.zeros_like(o_ref)


def _ffn(block_e, n_used, x, w_gate, w_up, w_down, ln=None, alpha=1.0, src=None):
    gather = src is not None
    n_blocks = block_e.shape[0]
    nj = D_FF // FFN_COLS

    def col(i, j, be, nu):
        return jnp.where(i < nu[0], j, nj - 1)

    def blk(i, be, nu):
        return jnp.minimum(i, nu[0] - 1)

    row = lambda i, j, be, nu: (i, 0)
    w_specs = [pl.BlockSpec((1, D_MODEL, FFN_COLS), lambda i, j, be, nu: (be[blk(i, be, nu)], 0, col(i, j, be, nu))),
               pl.BlockSpec((1, D_MODEL, FFN_COLS), lambda i, j, be, nu: (be[blk(i, be, nu)], 0, col(i, j, be, nu))),
               pl.BlockSpec((1, FFN_COLS, D_MODEL), lambda i, j, be, nu: (be[blk(i, be, nu)], col(i, j, be, nu), 0))]
    scratch = [pltpu.VMEM((FFN_ROWS, D_MODEL), BF16), pltpu.VMEM((FFN_ROWS, D_MODEL), F32)]
    if gather:
        assert ln is None
        src3 = src.reshape(n_blocks, 1, FFN_ROWS)
        in_specs = [pl.BlockSpec((1, 1, FFN_ROWS), lambda i, j, be, nu: (0, 0, 0), memory_space=pltpu.SMEM),
                    pl.BlockSpec((1, 1, FFN_ROWS), lambda i, j, be, nu: (jnp.minimum(i + 1, n_blocks - 1), 0, 0),
                                 memory_space=pltpu.SMEM),
                    pl.BlockSpec(memory_space=pl.ANY)] + w_specs
        args = [src3, src3, x, w_gate, w_up, w_down]
        scratch += [pltpu.VMEM((2, FFN_ROWS, D_MODEL), F32), pltpu.SemaphoreType.DMA((2,))]
    else:
        in_specs = [pl.BlockSpec((FFN_ROWS, D_MODEL), row)] + w_specs
        args = [x, w_gate, w_up, w_down]
        if ln is not None:
            in_specs += [pl.BlockSpec((1, D_MODEL), lambda i, j, be, nu: (0, 0))] * 2
            args += [ln[0].reshape(1, -1), ln[1].reshape(1, -1)]
    return pl.pallas_call(
        functools.partial(_ffn_body, alpha=alpha, fuse_ln=ln is not None, gather=gather),
        grid_spec=pltpu.PrefetchScalarGridSpec(
            num_scalar_prefetch=2,
            grid=(n_blocks, nj),
            in_specs=in_specs,
            out_specs=pl.BlockSpec((FFN_ROWS, D_MODEL), row),
            scratch_shapes=scratch),
        out_shape=jax.ShapeDtypeStruct((n_blocks * FFN_ROWS, D_MODEL), F32),
        compiler_params=_params("arbitrary", "arbitrary"),
        name="swiglu_ln" if ln is not None else "swiglu_experts",
    )(block_e, n_used, *args)


def _router_body(x_ref, w_ref, meta_ref, gate_ref, cnt_ref, run_ref):
    tm = x_ref.shape[0]

    @pl.when(pl.program_id(0) == 0)
    def _init():
        run_ref[...] = jnp.zeros_like(run_ref)

    logits = jnp.dot(x_ref[...], w_ref[...], preferred_element_type=F32, precision=lax.Precision.HIGHEST)
    lane_i = lax.broadcasted_iota(jnp.int32, (tm, LANES), 1)
    lane = lane_i.astype(F32)
    logits = jnp.where(lane_i < N_EXPERTS, logits, -jnp.inf)
    m1 = jnp.max(logits, axis=-1, keepdims=True)
    e1 = jnp.min(jnp.where(logits == m1, lane, float(LANES)), axis=-1, keepdims=True)
    rest = jnp.where(lane == e1, -jnp.inf, logits)
    m2 = jnp.max(rest, axis=-1, keepdims=True)
    e2 = jnp.min(jnp.where(rest == m2, lane, float(LANES)), axis=-1, keepdims=True)
    ex = jnp.exp(m2 - m1)
    g1 = 1.0 / (1.0 + ex)
    g2 = ex / (1.0 + ex)

    chosen = jnp.logical_or(lane == e1, lane == e2)
    r_i = lax.broadcasted_iota(jnp.int32, (tm, tm), 0)
    c_i = lax.broadcasted_iota(jnp.int32, (tm, tm), 1)
    earlier = (c_i < r_i).astype(BF16)
    before = jnp.dot(earlier, chosen.astype(BF16), preferred_element_type=F32) + run_ref[...]
    rank1 = jnp.sum(jnp.where(lane == e1, before, 0.0), axis=-1, keepdims=True)
    rank2 = jnp.sum(jnp.where(lane == e2, before, 0.0), axis=-1, keepdims=True)
    total = run_ref[...] + jnp.sum(chosen.astype(F32), axis=0, keepdims=True)
    run_ref[...] = total
    cnt_ref[...] = jnp.broadcast_to(total, cnt_ref.shape)
    meta = jnp.where(lane_i == 0, e1, jnp.where(lane_i == 1, e2, jnp.where(lane_i == 2, rank1, rank2)))
    meta_ref[...] = meta.astype(jnp.int32)
    gate_ref[...] = jnp.where(lane_i == 0, g1, g2)


def _router(x, w_router):
    t = x.shape[0]
    w_pad = jnp.zeros((D_MODEL, LANES), F32).at[:, :N_EXPERTS].set(w_router)
    row = lambda i: (i, 0)
    return pl.pallas_call(
        _router_body,
        grid=(t // ROW_TILE,),
        in_specs=[pl.BlockSpec((ROW_TILE, D_MODEL), row), pl.BlockSpec((D_MODEL, LANES), lambda i: (0, 0))],
        out_specs=[pl.BlockSpec((ROW_TILE, LANES), row), pl.BlockSpec((ROW_TILE, LANES), row),
                   pl.BlockSpec((SUBLANES, LANES), lambda i: (0, 0))],
        out_shape=[jax.ShapeDtypeStruct((t, LANES), jnp.int32), jax.ShapeDtypeStruct((t, LANES), F32),
                   jax.ShapeDtypeStruct((SUBLANES, LANES), F32)],
        scratch_shapes=[pltpu.VMEM((1, LANES), F32)],
        compiler_params=_params("arbitrary"),
        name="router_top2",
    )(x, w_pad)


def _combine_body(d0_ref, dn_ref, ys_hbm, gate_ref, x_ref, g_ref, b_ref, o_ref, buf_ref, sem, *, alpha):
    i = pl.program_id(0)
    slot = lax.rem(i, 2)
    n_rows = buf_ref.shape[1]

    def start_step(d_ref, to_slot):
        def body(r, _):
            pltpu.make_async_copy(ys_hbm.at[pl.ds(d_ref[0, 0, r], 1)], buf_ref.at[to_slot, pl.ds(r, 1)],
                                  sem.at[to_slot]).start()
            return 0
        lax.fori_loop(0, n_rows, body, 0)

    @pl.when(i == 0)
    def _first_rows():
        start_step(d0_ref, 0)

    pltpu.make_async_copy(buf_ref.at[slot], buf_ref.at[slot], sem.at[slot]).wait()

    @pl.when(i + 1 < pl.num_programs(0))
    def _next_rows():
        start_step(dn_ref, 1 - slot)

    gates = gate_ref[...]
    f = gates[:, 0:1] * buf_ref[slot, 0:MOVE_ROWS, :] + gates[:, 1:2] * buf_ref[slot, MOVE_ROWS:, :]
    o_ref[...] = _layer_norm(alpha * x_ref[...] + f, g_ref[...], b_ref[...])


def _combine(ys, dest1, dest2, gates, x, g, b, alpha):
    t = x.shape[0]
    steps = t // MOVE_ROWS
    dest = jnp.concatenate([dest1.reshape(steps, 1, MOVE_ROWS), dest2.reshape(steps, 1, MOVE_ROWS)], axis=-1)
    idx_shape = (1, 1, TOP_K * MOVE_ROWS)
    row = lambda i: (i, 0)
    const = lambda i: (0, 0)
    return pl.pallas_call(
        functools.partial(_combine_body, alpha=alpha),
        grid=(steps,),
        in_specs=[pl.BlockSpec(idx_shape, lambda i: (0, 0, 0), memory_space=pltpu.SMEM),
                  pl.BlockSpec(idx_shape, lambda i: (jnp.minimum(i + 1, steps - 1), 0, 0), memory_space=pltpu.SMEM),
                  pl.BlockSpec(memory_space=pl.ANY),
                  pl.BlockSpec((MOVE_ROWS, LANES), row), pl.BlockSpec((MOVE_ROWS, D_MODEL), row),
                  pl.BlockSpec((1, D_MODEL), const), pl.BlockSpec((1, D_MODEL), const)],
        out_specs=pl.BlockSpec((MOVE_ROWS, D_MODEL), row),
        out_shape=jax.ShapeDtypeStruct((t, D_MODEL), F32),
        scratch_shapes=[pltpu.VMEM((2, TOP_K * MOVE_ROWS, D_MODEL), F32), pltpu.SemaphoreType.DMA((2,))],
        compiler_params=_params("arbitrary"),
        name="combine_ln",
    )(dest, dest, ys, gates, x, g.reshape(1, -1), b.reshape(1, -1))


def _moe(x1, w_router, w_gate, w_up, w_down, first_expert, g, b, alpha):
    t = x1.shape[0]
    meta, gates, counts = _router(x1, w_router)
    e1, e2, rank1, rank2 = meta[:, 0], meta[:, 1], meta[:, 2], meta[:, 3]
    sizes = counts[0, :N_EXPERTS].astype(jnp.int32)
    padded = (sizes + FFN_ROWS - 1) // FFN_ROWS * FFN_ROWS
    group_end = jnp.cumsum(padded)
    group_start = group_end - padded
    dest1 = group_start[e1] + rank1
    dest2 = group_start[e2] + rank2
    n_blocks = (t * TOP_K) // FFN_ROWS + N_EXPERTS
    block_start = jnp.arange(n_blocks, dtype=jnp.int32) * FFN_ROWS
    block_e = jnp.minimum(jnp.sum(group_end[None, :] <= block_start[:, None], axis=1), N_EXPERTS - 1)
    n_used = (group_end[-1] // FFN_ROWS).reshape(1)
    tok = jnp.arange(t, dtype=jnp.int32)
    src = jnp.zeros((n_blocks * FFN_ROWS,), jnp.int32).at[jnp.concatenate([dest1, dest2])].set(
        jnp.concatenate([tok, tok]), unique_indices=True, indices_are_sorted=False)
    ys = _ffn((block_e + first_expert).astype(jnp.int32), n_used.astype(jnp.int32), x1, w_gate, w_up, w_down,
              src=src)
    return _combine(ys, dest1, dest2, gates, x1, g, b, alpha)


def kernel(x, w_in, w_out, attn_lambda, attn_subln_g, pool_w, pool_scale, conv_w, conv_b, lru_wa, lru_ba,
           lru_wi, lru_bi, lru_lambda, ln1_g, ln1_b, ln2_g, ln2_b, ffn_w_gate, ffn_w_up, ffn_w_down,
           router_w, moe_w_gate, moe_w_up, moe_w_down):
    batch, seq, d = x.shape
    depth = w_in.shape[0]
    assert d == D_MODEL and seq % ATTN_TILE == 0 and seq % MIX_ROWS == 0
    t = batch * seq
    assert t % FFN_ROWS == 0 and t % ROW_TILE == 0
    alpha = (2.0 * depth) ** 0.25
    dense_used = jnp.full((1,), t // FFN_ROWS, jnp.int32)
    moe_gate = moe_w_gate.reshape(-1, D_MODEL, D_FF)
    moe_up = moe_w_up.reshape(-1, D_MODEL, D_FF)
    moe_down = moe_w_down.reshape(-1, D_FF, D_MODEL)
    xt = x.reshape(t, d)
    for l in range(depth):
        lambda_init = 0.8 - 0.6 * math.exp(-0.3 * l)
        q, k, v, rest = _inproj(xt, w_in, l)
        y_attn = _attention(q, k, v, attn_lambda[l], attn_subln_g[l], lambda_init, batch, seq)
        y_mix = _mixer(rest, pool_w[l], pool_scale[l], conv_w[l], conv_b[l], lru_wa[l], lru_ba[l],
                       lru_wi[l], lru_bi[l], lru_lambda[l], batch, seq)
        x1 = _outproj(y_attn, y_mix, xt, w_out, l, ln1_g[l], ln1_b[l], alpha)
        if l % 2 == 0:
            dense_blocks = jnp.full((t // FFN_ROWS,), l // 2, jnp.int32)
            xt = _ffn(dense_blocks, dense_used, x1, ffn_w_gate, ffn_w_up, ffn_w_down,
                      ln=(ln2_g[l], ln2_b[l]), alpha=alpha)
        else:
            xt = _moe(x1, router_w[l // 2], moe_gate, moe_up, moe_down, (l // 2) * N_EXPERTS,
                      ln2_g[l], ln2_b[l], alpha)
    return xt.reshape(batch, seq, d)
```

```python
import functools
import math

import jax
import jax.numpy as jnp
from jax import lax
from jax.experimental import pallas as pl
from jax.experimental.pallas import tpu as pltpu

F32 = jnp.float32
BF16 = jnp.bfloat16

D_MODEL = 1024
CHUNK = 64
ATTN_HEADS = 4
ATTN_WIDTH = 512
ATTN_HEAD_DIM = 64
HEAD_WIDTH = 2 * ATTN_HEAD_DIM
POOL_WINDOWS = (2, 4, 8, 16)
POOL_WIDTH = 256
POOL_GROUP_DIM = 64
LRU_WIDTH = 256
LRU_C = 8.0
CONV_WIDTH = 4
REST_WIDTH = POOL_WIDTH + 2 * LRU_WIDTH
IN_WIDTH = 3 * ATTN_WIDTH + REST_WIDTH
MIX_WIDTH = POOL_WIDTH + LRU_WIDTH
D_FF = 2816
N_EXPERTS = 8
TOP_K = 2
LN_EPS = 1e-5
HEAD_NORM_EPS = 1e-5

LANES = 128
SUBLANES = 8
VMEM_LIMIT_BYTES = 56 * 1024 * 1024

ROW_TILE = 512
ATTN_TILE = 256
ATTN_HEADS_PER_STEP = 4
ATTN_SUM_ROWS = 16
ATTN_SCORE_SCALE = ATTN_HEAD_DIM ** -0.5 * math.log2(math.e)
MIX_ROWS = 256
FFN_ROWS = 1024
FFN_COLS = 256
MOVE_ROWS = 256
ROUTE_META = 8
DMA_ISSUE_UNROLL = 8
NEG_BIG = -1e30
F32_TINY = float(jnp.finfo(jnp.float32).tiny)
assert MIX_ROWS >= max(POOL_WINDOWS)


def _params(*semantics):
    return pltpu.CompilerParams(dimension_semantics=semantics, vmem_limit_bytes=VMEM_LIMIT_BYTES)


def _layer_norm(z, g, b):
    mu = jnp.mean(z, axis=-1, keepdims=True)
    zc = z - mu
    var = jnp.mean(zc * zc, axis=-1, keepdims=True)
    return zc * lax.rsqrt(var + LN_EPS) * g + b


def _inproj_body(x_ref, w_ref, q_ref, k_ref, v_ref, r_ref, wbf_ref):
    @pl.when(pl.program_id(0) == 0)
    def _cast_weights():
        for c in range(0, IN_WIDTH, 256):
            wbf_ref[:, c:c + 256] = w_ref[0, :, c:c + 256].astype(BF16)

    xb = x_ref[...].astype(BF16)

    def proj(c0, c1):
        return jnp.dot(xb, wbf_ref[:, c0:c1], preferred_element_type=F32)

    q_ref[...] = (proj(0, ATTN_WIDTH) * ATTN_SCORE_SCALE).astype(BF16)
    k_ref[...] = proj(ATTN_WIDTH, 2 * ATTN_WIDTH).astype(BF16)
    v_ref[...] = proj(2 * ATTN_WIDTH, 3 * ATTN_WIDTH).astype(BF16)
    r_ref[...] = proj(3 * ATTN_WIDTH, IN_WIDTH)


def _inproj(x, w, layer):
    t = x.shape[0]
    row = lambda i: (i, 0)
    return pl.pallas_call(
        _inproj_body,
        grid=(t // ROW_TILE,),
        in_specs=[pl.BlockSpec((ROW_TILE, D_MODEL), row),
                  pl.BlockSpec((1, D_MODEL, IN_WIDTH), lambda i: (layer, 0, 0))],
        out_specs=[pl.BlockSpec((ROW_TILE, ATTN_WIDTH), row)] * 3
        + [pl.BlockSpec((ROW_TILE, REST_WIDTH), row)],
        out_shape=[jax.ShapeDtypeStruct((t, ATTN_WIDTH), BF16)] * 3
        + [jax.ShapeDtypeStruct((t, REST_WIDTH), F32)],
        scratch_shapes=[pltpu.VMEM((D_MODEL, IN_WIDTH), BF16)],
        compiler_params=_params("arbitrary"),
        name="inproj",
    )(x, w)


def _attn_body(lam_ref, g_ref, q_ref, k_ref, v_ref, o_ref, vt_ref, *, lambda_init, seq):
    tq = ATTN_TILE
    lv = lam_ref[...]
    lam = (jnp.exp(jnp.sum(lv[0:1] * lv[1:2], axis=-1, keepdims=True))
           - jnp.exp(jnp.sum(lv[2:3] * lv[3:4], axis=-1, keepdims=True)) + lambda_init)
    gain = g_ref[...] * (1.0 - lambda_init)

    heads = range(ATTN_HEADS_PER_STEP)
    cols = [slice(g * HEAD_WIDTH, (g + 1) * HEAD_WIDTH) for g in heads]
    for g in heads:
        for c in range(seq // tq):
            vt_ref[g, c, 0:HEAD_WIDTH, :] = v_ref[c * tq:(c + 1) * tq, cols[g]].astype(F32).T.astype(BF16)
            vt_ref[g, c, HEAD_WIDTH:, :] = jnp.ones((ATTN_SUM_ROWS, tq), BF16)

    lane = lax.broadcasted_iota(jnp.int32, (tq, HEAD_WIDTH), 1)
    first_map = lane < ATTN_HEAD_DIM
    key = lax.broadcasted_iota(jnp.int32, (tq, 2 * tq), 0)
    qry = lax.broadcasted_iota(jnp.int32, (tq, 2 * tq), 1)
    q_chunk = jnp.where(qry >= tq, qry - tq, qry) // CHUNK
    visible = (key // CHUNK) <= q_chunk

    def q_block(i, _):
        q0 = pl.multiple_of(i * tq, tq)
        qqs = []
        for g in heads:
            q = q_ref[pl.ds(q0, tq), cols[g]]
            zero = jnp.zeros_like(q)
            qqs.append(jnp.concatenate([jnp.where(first_map, q, zero), jnp.where(first_map, zero, q)], axis=0))

        def step(j, carry, masked):
            k0 = pl.multiple_of(j * tq, tq)
            out = []
            scores = [lax.dot_general(k_ref[pl.ds(k0, tq), cols[g]], qqs[g], (((1,), (1,)), ((), ())),
                                      preferred_element_type=F32) for g in heads]
            for g in heads:
                m, acc = carry[g]
                s = scores[g]
                if masked:
                    s = jnp.where(visible, s, NEG_BIG)
                m_new = jnp.maximum(m, jnp.max(s, axis=0, keepdims=True))
                p = jnp.exp2(s - m_new)
                scale = jnp.exp2(m - m_new)
                pv = jnp.dot(vt_ref[g, j], p.astype(BF16), preferred_element_type=F32)
                out.append((m_new, scale * acc + pv))
            return tuple(out)

        init = tuple((jnp.full((1, 2 * tq), NEG_BIG, F32),
                      jnp.zeros((HEAD_WIDTH + ATTN_SUM_ROWS, 2 * tq), F32)) for _ in heads)
        carry = lax.fori_loop(0, i, lambda j, c: step(j, c, False), init)
        carry = step(i, carry, True)
        for g in heads:
            _, acc = carry[g]
            l = acc[HEAD_WIDTH:HEAD_WIDTH + 1, :]
            acc = acc[:HEAD_WIDTH, :]
            o = acc[:, :tq] / l[:, :tq] - lam * (acc[:, tq:] / l[:, tq:])
            o = o * lax.rsqrt(jnp.mean(o * o, axis=0, keepdims=True) + HEAD_NORM_EPS)
            o_ref[pl.ds(q0, tq), cols[g]] = (o.T * gain).astype(BF16)
        return 0

    lax.fori_loop(0, seq // tq, q_block, 0)


def _attention(q, k, v, lam_params, subln_g, lambda_init, batch, seq):
    t = q.shape[0]
    blk = pl.BlockSpec((seq, ATTN_HEADS_PER_STEP * HEAD_WIDTH), lambda b, h: (b, h))
    return pl.pallas_call(
        functools.partial(_attn_body, lambda_init=lambda_init, seq=seq),
        grid=(batch, ATTN_HEADS // ATTN_HEADS_PER_STEP),
        in_specs=[pl.BlockSpec((4, ATTN_HEAD_DIM), lambda b, h: (0, 0)),
                  pl.BlockSpec((1, HEAD_WIDTH), lambda b, h: (0, 0)),
                  blk, blk, blk],
        out_specs=blk,
        out_shape=jax.ShapeDtypeStruct((t, ATTN_WIDTH), BF16),
        scratch_shapes=[pltpu.VMEM((ATTN_HEADS_PER_STEP, seq // ATTN_TILE, HEAD_WIDTH + ATTN_SUM_ROWS, ATTN_TILE),
                                   BF16)],
        compiler_params=_params("arbitrary", "arbitrary"),
        name="diff_attention",
    )(lam_params, subln_g.reshape(1, HEAD_WIDTH), q, k, v)


def _mixer_body(r_ref, pw_ref, ps_ref, cw_ref, cb_ref, wa_ref, ba_ref, wi_ref, bi_ref, lam_ref, y_ref,
                *, seq):
    rows = MIX_ROWS
    pool_hist = max(POOL_WINDOWS)
    conv_hist = SUBLANES
    pw = pw_ref[...]
    wa = wa_ref[...]
    wi = wi_ref[...]
    ps, cb, ba, bi = ps_ref[...], cb_ref[...], ba_ref[...], bi_ref[...]
    cw = cw_ref[...]
    neg_c_softplus = -LRU_C * jax.nn.softplus(-lam_ref[...])

    lane_e = lax.broadcasted_iota(jnp.int32, (rows + pool_hist, POOL_WIDTH), 1)
    lane = lax.broadcasted_iota(jnp.int32, (rows, POOL_WIDTH), 1)
    row = lax.broadcasted_iota(jnp.int32, (rows, POOL_WIDTH), 0)
    win = jnp.where(lane < 64, 2, jnp.where(lane < 128, 4, jnp.where(lane < 192, 8, 16)))
    inv_win = 1.0 / win.astype(F32)
    inv_first = 1.0 / (row + 1).astype(F32)

    def chunk(c, carry):
        tail_u, tail_x, h_prev = carry
        r0 = pl.multiple_of(c * rows, rows)
        u = r_ref[pl.ds(r0, rows), 0:POOL_WIDTH]
        xr = r_ref[pl.ds(r0, rows), POOL_WIDTH:POOL_WIDTH + LRU_WIDTH]
        xg = r_ref[pl.ds(r0, rows), POOL_WIDTH + LRU_WIDTH:REST_WIDTH]

        ue = jnp.concatenate([tail_u, u], axis=0)
        w2 = ue + pltpu.roll(ue, 1, 0)
        w4 = w2 + pltpu.roll(w2, 2, 0)
        w8 = w4 + pltpu.roll(w4, 4, 0)
        w16 = w8 + pltpu.roll(w8, 8, 0)
        ws = jnp.where(lane_e < 64, w2, jnp.where(lane_e < 128, w4, jnp.where(lane_e < 192, w8, w16)))
        ws = ws[pool_hist:]
        inv_count = jnp.where(r0 + row + 1 >= win, inv_win, inv_first)
        pooled = ws * inv_count - u
        y_pool = jnp.dot(pooled.astype(BF16), pw, preferred_element_type=F32) * ps

        xe = jnp.concatenate([tail_x, xr], axis=0)
        xc = cb + pltpu.roll(xe, 3, 0)[conv_hist:] * cw[0:1]
        xc = xc + pltpu.roll(xe, 2, 0)[conv_hist:] * cw[1:2]
        xc = xc + pltpu.roll(xe, 1, 0)[conv_hist:] * cw[2:3]
        xc = xc + xr * cw[3:4]
        xcb = xc.astype(BF16)
        r_gate = jax.nn.sigmoid(jnp.dot(xcb, wa, preferred_element_type=F32) + ba)
        i_gate = jax.nn.sigmoid(jnp.dot(xcb, wi, preferred_element_type=F32) + bi)
        log_a = r_gate * neg_c_softplus
        a = jnp.exp(log_a)
        gap = -jnp.tanh(log_a) * (a * a + 1.0)
        b = (gap * lax.rsqrt(jnp.maximum(gap, F32_TINY))) * (i_gate * xc)

        s = 1
        while s < rows:
            keep = row >= s
            a_prev = jnp.where(keep, pltpu.roll(a, s, 0), 1.0)
            b_prev = jnp.where(keep, pltpu.roll(b, s, 0), 0.0)
            b = a * b_prev + b
            a = a * a_prev
            s *= 2
        h = a * h_prev + b
        y_lru = h * jax.nn.gelu(xg)

        y_ref[pl.ds(r0, rows), 0:POOL_WIDTH] = y_pool.astype(BF16)
        y_ref[pl.ds(r0, rows), POOL_WIDTH:MIX_WIDTH] = y_lru.astype(BF16)
        return u[rows - pool_hist:], xr[rows - conv_hist:], h[rows - 1:rows]

    init = (jnp.zeros((pool_hist, POOL_WIDTH), F32), jnp.zeros((conv_hist, LRU_WIDTH), F32),
            jnp.zeros((1, LRU_WIDTH), F32))
    lax.fori_loop(0, seq // rows, chunk, init)


def _block_diag(w):
    g, c, d = w.shape
    eye = jnp.eye(g, dtype=w.dtype)
    return (eye[:, None, :, None] * w[:, :, None, :]).reshape(g * c, g * d)


def _mixer(rest, pool_w, pool_scale, conv_w, conv_b, wa, ba, wi, bi, lru_lambda, batch, seq):
    t = rest.shape[0]
    full = lambda shape: pl.BlockSpec(shape, lambda b: (0, 0))
    vec = lambda a: a.reshape(1, -1)
    return pl.pallas_call(
        functools.partial(_mixer_body, seq=seq),
        grid=(batch,),
        in_specs=[pl.BlockSpec((seq, REST_WIDTH), lambda b: (b, 0)),
                  full((POOL_WIDTH, POOL_WIDTH)), full((1, POOL_WIDTH)),
                  full((CONV_WIDTH, LRU_WIDTH)), full((1, LRU_WIDTH)),
                  full((LRU_WIDTH, LRU_WIDTH)), full((1, LRU_WIDTH)),
                  full((LRU_WIDTH, LRU_WIDTH)), full((1, LRU_WIDTH)),
                  full((1, LRU_WIDTH))],
        out_specs=pl.BlockSpec((seq, MIX_WIDTH), lambda b: (b, 0)),
        out_shape=jax.ShapeDtypeStruct((t, MIX_WIDTH), BF16),
        compiler_params=_params("arbitrary"),
        name="pool_lru_mixer",
    )(rest, _block_diag(pool_w).astype(BF16), vec(pool_scale), conv_w, vec(conv_b),
      _block_diag(wa).astype(BF16), vec(ba), _block_diag(wi).astype(BF16), vec(bi), vec(lru_lambda))


def _outproj_body(ya_ref, ym_ref, x_ref, w_ref, g_ref, b_ref, *rest, alpha, route):
    if route:
        wr_ref, o_ref, meta_ref, gate_ref, cnt_ref, wbf_ref, run_ref = rest
    else:
        o_ref, wbf_ref = rest

    @pl.when(pl.program_id(0) == 0)
    def _cast_weights():
        for c in range(0, D_MODEL, 256):
            wbf_ref[:, c:c + 256] = w_ref[0, :, c:c + 256].astype(BF16)

    mix = jnp.dot(ya_ref[...], wbf_ref[0:ATTN_WIDTH, :], preferred_element_type=F32)
    mix = mix + jnp.dot(ym_ref[...], wbf_ref[ATTN_WIDTH:, :], preferred_element_type=F32)
    x1 = _layer_norm(alpha * x_ref[...] + mix, g_ref[...], b_ref[...])
    o_ref[...] = x1
    if route:
        _route(x1, wr_ref, meta_ref, gate_ref, cnt_ref, run_ref)


def _outproj(y_attn, y_mix, x, w, layer, g, b, alpha, w_router=None):
    t = x.shape[0]
    route = w_router is not None
    row = lambda i: (i, 0)
    const = lambda i: (0, 0)
    in_specs = [pl.BlockSpec((ROW_TILE, ATTN_WIDTH), row), pl.BlockSpec((ROW_TILE, MIX_WIDTH), row),
                pl.BlockSpec((ROW_TILE, D_MODEL), row),
                pl.BlockSpec((1, D_MODEL, D_MODEL), lambda i: (layer, 0, 0)),
                pl.BlockSpec((1, D_MODEL), const), pl.BlockSpec((1, D_MODEL), const)]
    args = [y_attn, y_mix, x, w, g.reshape(1, -1), b.reshape(1, -1)]
    out_specs = [pl.BlockSpec((ROW_TILE, D_MODEL), row)]
    out_shape = [jax.ShapeDtypeStruct((t, D_MODEL), F32)]
    scratch = [pltpu.VMEM((D_MODEL, D_MODEL), BF16)]
    if route:
        in_specs.append(pl.BlockSpec((D_MODEL, LANES), const))
        args.append(jnp.zeros((D_MODEL, LANES), F32).at[:, :N_EXPERTS].set(w_router))
        out_specs += [pl.BlockSpec((ROW_TILE, ROUTE_META), row), pl.BlockSpec((ROW_TILE, LANES), row),
                      pl.BlockSpec((SUBLANES, LANES), const)]
        out_shape += [jax.ShapeDtypeStruct((t, ROUTE_META), jnp.int32), jax.ShapeDtypeStruct((t, LANES), F32),
                      jax.ShapeDtypeStruct((SUBLANES, LANES), F32)]
        scratch.append(pltpu.VMEM((1, LANES), F32))
    return pl.pallas_call(
        functools.partial(_outproj_body, alpha=alpha, route=route),
        grid=(t // ROW_TILE,),
        in_specs=in_specs,
        out_specs=out_specs,
        out_shape=out_shape,
        scratch_shapes=scratch,
        compiler_params=_params("arbitrary"),
        name="outproj_ln_route" if route else "outproj_ln",
    )(*args)


def _gather_copies(src_ref, x_hbm, buf_ref, sem, slot):
    def copy(r):
        return pltpu.make_async_copy(x_hbm.at[pl.ds(src_ref[0, 0, r], 1)], buf_ref.at[slot, pl.ds(r, 1)],
                                     sem.at[slot])
    return copy


def _ffn_body(be_ref, nu_ref, *rest, alpha, fuse_ln, gather):
    if gather:
        src0_ref, src_next_ref, x_hbm, wg_ref, wu_ref, wd_ref, o_ref, xb_ref, acc_ref, buf_ref, sem = rest
    elif fuse_ln:
        x_ref, wg_ref, wu_ref, wd_ref, g_ref, b_ref, o_ref, xb_ref, acc_ref = rest
    else:
        x_ref, wg_ref, wu_ref, wd_ref, o_ref, xb_ref, acc_ref = rest
    i = pl.program_id(0)
    j = pl.program_id(1)
    n_used = nu_ref[0]
    used = i < n_used

    if gather:
        n_rows = buf_ref.shape[1]
        slot = lax.rem(i, 2)

        def start_block(src_ref, to_slot):
            copy = _gather_copies(src_ref, x_hbm, buf_ref, sem, to_slot)

            def body(r, _):
                copy(r).start()
                return 0
            lax.fori_loop(0, n_rows, body, 0, unroll=DMA_ISSUE_UNROLL)

        @pl.when(jnp.logical_and(used, jnp.logical_and(i == 0, j == 0)))
        def _first_rows():
            start_block(src0_ref, 0)

        @pl.when(jnp.logical_and(used, j == 0))
        def _start():
            pltpu.make_async_copy(buf_ref.at[slot], buf_ref.at[slot], sem.at[slot]).wait()

            @pl.when(i + 1 < n_used)
            def _next_rows():
                start_block(src_next_ref, 1 - slot)

            xb_ref[...] = buf_ref[slot].astype(BF16)
            acc_ref[...] = jnp.zeros_like(acc_ref)
    else:
        @pl.when(jnp.logical_and(used, j == 0))
        def _start():
            xb_ref[...] = x_ref[...].astype(BF16)
            acc_ref[...] = jnp.zeros_like(acc_ref)

    @pl.when(used)
    def _accumulate():
        xb = xb_ref[...]
        gate = jnp.dot(xb, wg_ref[0].astype(BF16), preferred_element_type=F32)
        up = jnp.dot(xb, wu_ref[0].astype(BF16), preferred_element_type=F32)
        hidden = (jax.nn.silu(gate) * up).astype(BF16)
        acc_ref[...] += jnp.dot(hidden, wd_ref[0].astype(BF16), preferred_element_type=F32)

    @pl.when(jnp.logical_and(used, j == pl.num_programs(1) - 1))
    def _finish():
        if fuse_ln:
            o_ref[...] = _layer_norm(alpha * x_ref[...] + acc_ref[...], g_ref[...], b_ref[...])
        else:
            o_ref[...] = acc_ref[...]

    @pl.when(jnp.logical_and(jnp.logical_not(used), j == 0))
    def _empty():
        o_ref[...] = jnp.zeros_like(o_ref)


def _ffn(block_e, n_used, x, w_gate, w_up, w_down, ln=None, alpha=1.0, src=None):
    gather = src is not None
    n_blocks = block_e.shape[0]
    nj = D_FF // FFN_COLS

    def col(i, j, be, nu):
        return jnp.where(i < nu[0], j, nj - 1)

    def blk(i, be, nu):
        return jnp.minimum(i, nu[0] - 1)

    row = lambda i, j, be, nu: (i, 0)
    w_specs = [pl.BlockSpec((1, D_MODEL, FFN_COLS), lambda i, j, be, nu: (be[blk(i, be, nu)], 0, col(i, j, be, nu))),
               pl.BlockSpec((1, D_MODEL, FFN_COLS), lambda i, j, be, nu: (be[blk(i, be, nu)], 0, col(i, j, be, nu))),
               pl.BlockSpec((1, FFN_COLS, D_MODEL), lambda i, j, be, nu: (be[blk(i, be, nu)], col(i, j, be, nu), 0))]
    scratch = [pltpu.VMEM((FFN_ROWS, D_MODEL), BF16), pltpu.VMEM((FFN_ROWS, D_MODEL), F32)]
    if gather:
        assert ln is None
        src3 = src.reshape(n_blocks, 1, FFN_ROWS)
        in_specs = [pl.BlockSpec((1, 1, FFN_ROWS), lambda i, j, be, nu: (0, 0, 0), memory_space=pltpu.SMEM),
                    pl.BlockSpec((1, 1, FFN_ROWS), lambda i, j, be, nu: (jnp.minimum(i + 1, n_blocks - 1), 0, 0),
                                 memory_space=pltpu.SMEM),
                    pl.BlockSpec(memory_space=pl.ANY)] + w_specs
        args = [src3, src3, x, w_gate, w_up, w_down]
        scratch += [pltpu.VMEM((2, FFN_ROWS, D_MODEL), F32), pltpu.SemaphoreType.DMA((2,))]
    else:
        in_specs = [pl.BlockSpec((FFN_ROWS, D_MODEL), row)] + w_specs
        args = [x, w_gate, w_up, w_down]
        if ln is not None:
            in_specs += [pl.BlockSpec((1, D_MODEL), lambda i, j, be, nu: (0, 0))] * 2
            args += [ln[0].reshape(1, -1), ln[1].reshape(1, -1)]
    return pl.pallas_call(
        functools.partial(_ffn_body, alpha=alpha, fuse_ln=ln is not None, gather=gather),
        grid_spec=pltpu.PrefetchScalarGridSpec(
            num_scalar_prefetch=2,
            grid=(n_blocks, nj),
            in_specs=in_specs,
            out_specs=pl.BlockSpec((FFN_ROWS, D_MODEL), row),
            scratch_shapes=scratch),
        out_shape=jax.ShapeDtypeStruct((n_blocks * FFN_ROWS, D_MODEL), F32),
        compiler_params=_params("arbitrary", "arbitrary"),
        name="swiglu_ln" if ln is not None else "swiglu_experts",
    )(block_e, n_used, *args)


def _route(x, w_ref, meta_ref, gate_ref, cnt_ref, run_ref):
    tm = x.shape[0]

    @pl.when(pl.program_id(0) == 0)
    def _init():
        run_ref[...] = jnp.zeros_like(run_ref)

    logits = jnp.dot(x, w_ref[...], preferred_element_type=F32, precision=lax.Precision.HIGHEST)
    lane_i = lax.broadcasted_iota(jnp.int32, (tm, LANES), 1)
    lane = lane_i.astype(F32)
    logits = jnp.where(lane_i < N_EXPERTS, logits, -jnp.inf)
    m1 = jnp.max(logits, axis=-1, keepdims=True)
    e1 = jnp.min(jnp.where(logits == m1, lane, float(LANES)), axis=-1, keepdims=True)
    rest = jnp.where(lane == e1, -jnp.inf, logits)
    m2 = jnp.max(rest, axis=-1, keepdims=True)
    e2 = jnp.min(jnp.where(rest == m2, lane, float(LANES)), axis=-1, keepdims=True)
    ex = jnp.exp(m2 - m1)
    g1 = 1.0 / (1.0 + ex)
    g2 = ex / (1.0 + ex)

    chosen = jnp.logical_or(lane == e1, lane == e2)
    r_i = lax.broadcasted_iota(jnp.int32, (tm, tm), 0)
    c_i = lax.broadcasted_iota(jnp.int32, (tm, tm), 1)
    earlier = (c_i < r_i).astype(BF16)
    before = jnp.dot(earlier, chosen.astype(BF16), preferred_element_type=F32) + run_ref[...]
    rank1 = jnp.sum(jnp.where(lane == e1, before, 0.0), axis=-1, keepdims=True)
    rank2 = jnp.sum(jnp.where(lane == e2, before, 0.0), axis=-1, keepdims=True)
    total = run_ref[...] + jnp.sum(chosen.astype(F32), axis=0, keepdims=True)
    run_ref[...] = total
    cnt_ref[...] = jnp.broadcast_to(total, cnt_ref.shape)
    meta = jnp.where(lane_i == 0, e1, jnp.where(lane_i == 1, e2, jnp.where(lane_i == 2, rank1, rank2)))
    meta_ref[...] = meta[:, :ROUTE_META].astype(jnp.int32)
    gate_ref[...] = jnp.where(lane_i == 0, g1, g2)


def _combine_body(d0_ref, dn_ref, ys_hbm, gate_ref, x_ref, g_ref, b_ref, o_ref, buf_ref, sem, *, alpha):
    i = pl.program_id(0)
    slot = lax.rem(i, 2)
    n_rows = buf_ref.shape[1]

    def start_step(d_ref, to_slot):
        def body(r, _):
            pltpu.make_async_copy(ys_hbm.at[pl.ds(d_ref[0, 0, r], 1)], buf_ref.at[to_slot, pl.ds(r, 1)],
                                  sem.at[to_slot]).start()
            return 0
        lax.fori_loop(0, n_rows, body, 0, unroll=DMA_ISSUE_UNROLL)

    @pl.when(i == 0)
    def _first_rows():
        start_step(d0_ref, 0)

    pltpu.make_async_copy(buf_ref.at[slot], buf_ref.at[slot], sem.at[slot]).wait()

    @pl.when(i + 1 < pl.num_programs(0))
    def _next_rows():
        start_step(dn_ref, 1 - slot)

    gates = gate_ref[...]
    f = gates[:, 0:1] * buf_ref[slot, 0:MOVE_ROWS, :] + gates[:, 1:2] * buf_ref[slot, MOVE_ROWS:, :]
    o_ref[...] = _layer_norm(alpha * x_ref[...] + f, g_ref[...], b_ref[...])


def _combine(ys, dest1, dest2, gates, x, g, b, alpha):
    t = x.shape[0]
    steps = t // MOVE_ROWS
    dest = jnp.concatenate([dest1.reshape(steps, 1, MOVE_ROWS), dest2.reshape(steps, 1, MOVE_ROWS)], axis=-1)
    idx_shape = (1, 1, TOP_K * MOVE_ROWS)
    row = lambda i: (i, 0)
    const = lambda i: (0, 0)
    return pl.pallas_call(
        functools.partial(_combine_body, alpha=alpha),
        grid=(steps,),
        in_specs=[pl.BlockSpec(idx_shape, lambda i: (0, 0, 0), memory_space=pltpu.SMEM),
                  pl.BlockSpec(idx_shape, lambda i: (jnp.minimum(i + 1, steps - 1), 0, 0), memory_space=pltpu.SMEM),
                  pl.BlockSpec(memory_space=pl.ANY),
                  pl.BlockSpec((MOVE_ROWS, LANES), row), pl.BlockSpec((MOVE_ROWS, D_MODEL), row),
                  pl.BlockSpec((1, D_MODEL), const), pl.BlockSpec((1, D_MODEL), const)],
        out_specs=pl.BlockSpec((MOVE_ROWS, D_MODEL), row),
        out_shape=jax.ShapeDtypeStruct((t, D_MODEL), F32),
        scratch_shapes=[pltpu.VMEM((2, TOP_K * MOVE_ROWS, D_MODEL), F32), pltpu.SemaphoreType.DMA((2,))],
        compiler_params=_params("arbitrary"),
        name="combine_ln",
    )(dest, dest, ys, gates, x, g.reshape(1, -1), b.reshape(1, -1))


def _moe(x1, meta, gates, counts, w_gate, w_up, w_down, first_expert, g, b, alpha):
    t = x1.shape[0]
    e1, e2, rank1, rank2 = meta[:, 0], meta[:, 1], meta[:, 2], meta[:, 3]
    sizes = counts[0, :N_EXPERTS].astype(jnp.int32)
    padded = (sizes + FFN_ROWS - 1) // FFN_ROWS * FFN_ROWS
    group_end = jnp.cumsum(padded)
    group_start = group_end - padded
    dest1 = group_start[e1] + rank1
    dest2 = group_start[e2] + rank2
    n_blocks = (t * TOP_K) // FFN_ROWS + N_EXPERTS
    block_start = jnp.arange(n_blocks, dtype=jnp.int32) * FFN_ROWS
    block_e = jnp.minimum(jnp.sum(group_end[None, :] <= block_start[:, None], axis=1), N_EXPERTS - 1)
    n_used = (group_end[-1] // FFN_ROWS).reshape(1)
    tok = jnp.arange(t, dtype=jnp.int32)
    src = jnp.zeros((n_blocks * FFN_ROWS,), jnp.int32).at[jnp.concatenate([dest1, dest2])].set(
        jnp.concatenate([tok, tok]), unique_indices=True, indices_are_sorted=False)
    ys = _ffn((block_e + first_expert).astype(jnp.int32), n_used.astype(jnp.int32), x1, w_gate, w_up, w_down,
              src=src)
    return _combine(ys, dest1, dest2, gates, x1, g, b, alpha)


def kernel(x, w_in, w_out, attn_lambda, attn_subln_g, pool_w, pool_scale, conv_w, conv_b, lru_wa, lru_ba,
           lru_wi, lru_bi, lru_lambda, ln1_g, ln1_b, ln2_g, ln2_b, ffn_w_gate, ffn_w_up, ffn_w_down,
           router_w, moe_w_gate, moe_w_up, moe_w_down):
    batch, seq, d = x.shape
    depth = w_in.shape[0]
    assert d == D_MODEL and seq % ATTN_TILE == 0 and seq % MIX_ROWS == 0
    t = batch * seq
    assert t % FFN_ROWS == 0 and t % ROW_TILE == 0
    alpha = (2.0 * depth) ** 0.25
    dense_used = jnp.full((1,), t // FFN_ROWS, jnp.int32)
    moe_gate = moe_w_gate.reshape(-1, D_MODEL, D_FF)
    moe_up = moe_w_up.reshape(-1, D_MODEL, D_FF)
    moe_down = moe_w_down.reshape(-1, D_FF, D_MODEL)
    xt = x.reshape(t, d)
    for l in range(depth):
        lambda_init = 0.8 - 0.6 * math.exp(-0.3 * l)
        q, k, v, rest = _inproj(xt, w_in, l)
        y_attn = _attention(q, k, v, attn_lambda[l], attn_subln_g[l], lambda_init, batch, seq)
        y_mix = _mixer(rest, pool_w[l], pool_scale[l], conv_w[l], conv_b[l], lru_wa[l], lru_ba[l],
                       lru_wi[l], lru_bi[l], lru_lambda[l], batch, seq)
        if l % 2 == 0:
            x1, = _outproj(y_attn, y_mix, xt, w_out, l, ln1_g[l], ln1_b[l], alpha)
            dense_blocks = jnp.full((t // FFN_ROWS,), l // 2, jnp.int32)
            xt = _ffn(dense_blocks, dense_used, x1, ffn_w_gate, ffn_w_up, ffn_w_down,
                      ln=(ln2_g[l], ln2_b[l]), alpha=alpha)
        else:
            x1, meta, gates, counts = _outproj(y_attn, y_mix, xt, w_out, l, ln1_g[l], ln1_b[l], alpha,
                                               w_router=router_w[l // 2])
            xt = _moe(x1, meta, gates, counts, moe_gate, moe_up, moe_down, (l // 2) * N_EXPERTS,
                      ln2_g[l], ln2_b[l], alpha)
    return xt.reshape(batch, seq, d)
```

```python
import functools
import math

import jax
import jax.numpy as jnp
from jax import lax
from jax.experimental import pallas as pl
from jax.experimental.pallas import tpu as pltpu

F32 = jnp.float32
BF16 = jnp.bfloat16

D_MODEL = 1024
CHUNK = 64
ATTN_HEADS = 4
ATTN_WIDTH = 512
ATTN_HEAD_DIM = 64
HEAD_WIDTH = 2 * ATTN_HEAD_DIM
POOL_WINDOWS = (2, 4, 8, 16)
POOL_WIDTH = 256
POOL_GROUP_DIM = 64
LRU_WIDTH = 256
LRU_C = 8.0
CONV_WIDTH = 4
REST_WIDTH = POOL_WIDTH + 2 * LRU_WIDTH
IN_WIDTH = 3 * ATTN_WIDTH + REST_WIDTH
MIX_WIDTH = POOL_WIDTH + LRU_WIDTH
D_FF = 2816
N_EXPERTS = 8
TOP_K = 2
LN_EPS = 1e-5
HEAD_NORM_EPS = 1e-5

LANES = 128
SUBLANES = 8
VMEM_LIMIT_BYTES = 56 * 1024 * 1024

ROW_TILE = 512
ATTN_TILE = 256
ATTN_HEADS_PER_STEP = 4
ATTN_SUM_ROWS = 16
ATTN_SCORE_SCALE = ATTN_HEAD_DIM ** -0.5 * math.log2(math.e)
MIX_ROWS = 256
FFN_ROWS = 1024
FFN_COLS = 256
MOVE_ROWS = 256
ROUTE_META = 8
DMA_ISSUE_UNROLL = 8
NEG_BIG = -1e30
F32_TINY = float(jnp.finfo(jnp.float32).tiny)
assert MIX_ROWS >= max(POOL_WINDOWS)


def _params(*semantics):
    return pltpu.CompilerParams(dimension_semantics=semantics, vmem_limit_bytes=VMEM_LIMIT_BYTES)


def _layer_norm(z, g, b):
    mu = jnp.mean(z, axis=-1, keepdims=True)
    zc = z - mu
    var = jnp.mean(zc * zc, axis=-1, keepdims=True)
    return zc * lax.rsqrt(var + LN_EPS) * g + b


def _inproj_body(x_ref, w_ref, q_ref, k_ref, v_ref, r_ref, wbf_ref):
    @pl.when(pl.program_id(0) == 0)
    def _cast_weights():
        for c in range(0, IN_WIDTH, 256):
            wbf_ref[:, c:c + 256] = w_ref[0, :, c:c + 256].astype(BF16)

    xb = x_ref[...].astype(BF16)

    def proj(c0, c1):
        return jnp.dot(xb, wbf_ref[:, c0:c1], preferred_element_type=F32)

    q_ref[...] = (proj(0, ATTN_WIDTH) * ATTN_SCORE_SCALE).astype(BF16)
    k_ref[...] = proj(ATTN_WIDTH, 2 * ATTN_WIDTH).astype(BF16)
    v_ref[...] = proj(2 * ATTN_WIDTH, 3 * ATTN_WIDTH).astype(BF16)
    r_ref[...] = proj(3 * ATTN_WIDTH, IN_WIDTH)


def _inproj(x, w, layer):
    t = x.shape[0]
    row = lambda i: (i, 0)
    return pl.pallas_call(
        _inproj_body,
        grid=(t // ROW_TILE,),
        in_specs=[pl.BlockSpec((ROW_TILE, D_MODEL), row),
                  pl.BlockSpec((1, D_MODEL, IN_WIDTH), lambda i: (layer, 0, 0))],
        out_specs=[pl.BlockSpec((ROW_TILE, ATTN_WIDTH), row)] * 3
        + [pl.BlockSpec((ROW_TILE, REST_WIDTH), row)],
        out_shape=[jax.ShapeDtypeStruct((t, ATTN_WIDTH), BF16)] * 3
        + [jax.ShapeDtypeStruct((t, REST_WIDTH), F32)],
        scratch_shapes=[pltpu.VMEM((D_MODEL, IN_WIDTH), BF16)],
        compiler_params=_params("arbitrary"),
        name="inproj",
    )(x, w)


def _attn_body(lam_ref, g_ref, q_ref, k_ref, v_ref, o_ref, vt_ref, *, lambda_init, seq):
    tq = ATTN_TILE
    lv = lam_ref[...]
    lam = (jnp.exp(jnp.sum(lv[0:1] * lv[1:2], axis=-1, keepdims=True))
           - jnp.exp(jnp.sum(lv[2:3] * lv[3:4], axis=-1, keepdims=True)) + lambda_init)
    gain = g_ref[...] * (1.0 - lambda_init)

    heads = range(ATTN_HEADS_PER_STEP)
    cols = [slice(g * HEAD_WIDTH, (g + 1) * HEAD_WIDTH) for g in heads]
    for g in heads:
        for c in range(seq // tq):
            vt_ref[g, c, 0:HEAD_WIDTH, :] = v_ref[c * tq:(c + 1) * tq, cols[g]].astype(F32).T.astype(BF16)
            vt_ref[g, c, HEAD_WIDTH:, :] = jnp.ones((ATTN_SUM_ROWS, tq), BF16)

    lane = lax.broadcasted_iota(jnp.int32, (tq, HEAD_WIDTH), 1)
    first_map = lane < ATTN_HEAD_DIM
    key = lax.broadcasted_iota(jnp.int32, (tq, 2 * tq), 0)
    qry = lax.broadcasted_iota(jnp.int32, (tq, 2 * tq), 1)
    q_chunk = jnp.where(qry >= tq, qry - tq, qry) // CHUNK
    visible = (key // CHUNK) <= q_chunk

    def q_block(i, _):
        q0 = pl.multiple_of(i * tq, tq)
        qqs = []
        for g in heads:
            q = q_ref[pl.ds(q0, tq), cols[g]]
            zero = jnp.zeros_like(q)
            qqs.append(jnp.concatenate([jnp.where(first_map, q, zero), jnp.where(first_map, zero, q)], axis=0))

        def step(j, carry, masked):
            k0 = pl.multiple_of(j * tq, tq)
            out = []
            scores = [lax.dot_general(k_ref[pl.ds(k0, tq), cols[g]], qqs[g], (((1,), (1,)), ((), ())),
                                      preferred_element_type=F32) for g in heads]
            for g in heads:
                m, acc = carry[g]
                s = scores[g]
                if masked:
                    s = jnp.where(visible, s, NEG_BIG)
                m_new = jnp.maximum(m, jnp.max(s, axis=0, keepdims=True))
                p = jnp.exp2(s - m_new)
                scale = jnp.exp2(m - m_new)
                pv = jnp.dot(vt_ref[g, j], p.astype(BF16), preferred_element_type=F32)
                out.append((m_new, scale * acc + pv))
            return tuple(out)

        init = tuple((jnp.full((1, 2 * tq), NEG_BIG, F32),
                      jnp.zeros((HEAD_WIDTH + ATTN_SUM_ROWS, 2 * tq), F32)) for _ in heads)
        carry = lax.fori_loop(0, i, lambda j, c: step(j, c, False), init)
        carry = step(i, carry, True)
        for g in heads:
            _, acc = carry[g]
            l = acc[HEAD_WIDTH:HEAD_WIDTH + 1, :]
            acc = acc[:HEAD_WIDTH, :]
            o = acc[:, :tq] / l[:, :tq] - lam * (acc[:, tq:] / l[:, tq:])
            o = o * lax.rsqrt(jnp.mean(o * o, axis=0, keepdims=True) + HEAD_NORM_EPS)
            o_ref[pl.ds(q0, tq), cols[g]] = (o.T * gain).astype(BF16)
        return 0

    lax.fori_loop(0, seq // tq, q_block, 0)


def _attention(q, k, v, lam_params, subln_g, lambda_init, batch, seq):
    t = q.shape[0]
    blk = pl.BlockSpec((seq, ATTN_HEADS_PER_STEP * HEAD_WIDTH), lambda b, h: (b, h))
    return pl.pallas_call(
        functools.partial(_attn_body, lambda_init=lambda_init, seq=seq),
        grid=(batch, ATTN_HEADS // ATTN_HEADS_PER_STEP),
        in_specs=[pl.BlockSpec((4, ATTN_HEAD_DIM), lambda b, h: (0, 0)),
                  pl.BlockSpec((1, HEAD_WIDTH), lambda b, h: (0, 0)),
                  blk, blk, blk],
        out_specs=blk,
        out_shape=jax.ShapeDtypeStruct((t, ATTN_WIDTH), BF16),
        scratch_shapes=[pltpu.VMEM((ATTN_HEADS_PER_STEP, seq // ATTN_TILE, HEAD_WIDTH + ATTN_SUM_ROWS, ATTN_TILE),
                                   BF16)],
        compiler_params=_params("arbitrary", "arbitrary"),
        name="diff_attention",
    )(lam_params, subln_g.reshape(1, HEAD_WIDTH), q, k, v)


def _mixer_body(r_ref, pw_ref, ps_ref, cw_ref, cb_ref, wa_ref, ba_ref, wi_ref, bi_ref, lam_ref, y_ref,
                *, seq):
    rows = MIX_ROWS
    pool_hist = max(POOL_WINDOWS)
    conv_hist = SUBLANES
    pw = pw_ref[...]
    wa = wa_ref[...]
    wi = wi_ref[...]
    ps, cb, ba, bi = ps_ref[...], cb_ref[...], ba_ref[...], bi_ref[...]
    cw = cw_ref[...]
    neg_c_softplus = -LRU_C * jax.nn.softplus(-lam_ref[...])

    lane_e = lax.broadcasted_iota(jnp.int32, (rows + pool_hist, POOL_WIDTH), 1)
    lane = lax.broadcasted_iota(jnp.int32, (rows, POOL_WIDTH), 1)
    row = lax.broadcasted_iota(jnp.int32, (rows, POOL_WIDTH), 0)
    win = jnp.where(lane < 64, 2, jnp.where(lane < 128, 4, jnp.where(lane < 192, 8, 16)))
    inv_win = 1.0 / win.astype(F32)
    inv_first = 1.0 / (row + 1).astype(F32)

    def chunk(c, carry):
        tail_u, tail_x, h_prev = carry
        r0 = pl.multiple_of(c * rows, rows)
        u = r_ref[pl.ds(r0, rows), 0:POOL_WIDTH]
        xr = r_ref[pl.ds(r0, rows), POOL_WIDTH:POOL_WIDTH + LRU_WIDTH]
        xg = r_ref[pl.ds(r0, rows), POOL_WIDTH + LRU_WIDTH:REST_WIDTH]

        ue = jnp.concatenate([tail_u, u], axis=0)
        w2 = ue + pltpu.roll(ue, 1, 0)
        w4 = w2 + pltpu.roll(w2, 2, 0)
        w8 = w4 + pltpu.roll(w4, 4, 0)
        w16 = w8 + pltpu.roll(w8, 8, 0)
        ws = jnp.where(lane_e < 64, w2, jnp.where(lane_e < 128, w4, jnp.where(lane_e < 192, w8, w16)))
        ws = ws[pool_hist:]
        inv_count = jnp.where(r0 + row + 1 >= win, inv_win, inv_first)
        pooled = ws * inv_count - u
        y_pool = jnp.dot(pooled.astype(BF16), pw, preferred_element_type=F32) * ps

        xe = jnp.concatenate([tail_x, xr], axis=0)
        xc = cb + pltpu.roll(xe, 3, 0)[conv_hist:] * cw[0:1]
        xc = xc + pltpu.roll(xe, 2, 0)[conv_hist:] * cw[1:2]
        xc = xc + pltpu.roll(xe, 1, 0)[conv_hist:] * cw[2:3]
        xc = xc + xr * cw[3:4]
        xcb = xc.astype(BF16)
        r_gate = jax.nn.sigmoid(jnp.dot(xcb, wa, preferred_element_type=F32) + ba)
        i_gate = jax.nn.sigmoid(jnp.dot(xcb, wi, preferred_element_type=F32) + bi)
        log_a = r_gate * neg_c_softplus
        a = jnp.exp(log_a)
        gap = -jnp.tanh(log_a) * (a * a + 1.0)
        b = (gap * lax.rsqrt(jnp.maximum(gap, F32_TINY))) * (i_gate * xc)

        s = 1
        while s < rows:
            keep = row >= s
            a_prev = jnp.where(keep, pltpu.roll(a, s, 0), 1.0)
            b_prev = jnp.where(keep, pltpu.roll(b, s, 0), 0.0)
            b = a * b_prev + b
            a = a * a_prev
            s *= 2
        h = a * h_prev + b
        y_lru = h * jax.nn.gelu(xg)

        y_ref[pl.ds(r0, rows), 0:POOL_WIDTH] = y_pool.astype(BF16)
        y_ref[pl.ds(r0, rows), POOL_WIDTH:MIX_WIDTH] = y_lru.astype(BF16)
        return u[rows - pool_hist:], xr[rows - conv_hist:], h[rows - 1:rows]

    init = (jnp.zeros((pool_hist, POOL_WIDTH), F32), jnp.zeros((conv_hist, LRU_WIDTH), F32),
            jnp.zeros((1, LRU_WIDTH), F32))
    lax.fori_loop(0, seq // rows, chunk, init)


def _block_diag(w):
    g, c, d = w.shape
    eye = jnp.eye(g, dtype=w.dtype)
    return (eye[:, None, :, None] * w[:, :, None, :]).reshape(g * c, g * d)


def _mixer(rest, pool_w, pool_scale, conv_w, conv_b, wa, ba, wi, bi, lru_lambda, batch, seq):
    t = rest.shape[0]
    full = lambda shape: pl.BlockSpec(shape, lambda b: (0, 0))
    vec = lambda a: a.reshape(1, -1)
    return pl.pallas_call(
        functools.partial(_mixer_body, seq=seq),
        grid=(batch,),
        in_specs=[pl.BlockSpec((seq, REST_WIDTH), lambda b: (b, 0)),
                  full((POOL_WIDTH, POOL_WIDTH)), full((1, POOL_WIDTH)),
                  full((CONV_WIDTH, LRU_WIDTH)), full((1, LRU_WIDTH)),
                  full((LRU_WIDTH, LRU_WIDTH)), full((1, LRU_WIDTH)),
                  full((LRU_WIDTH, LRU_WIDTH)), full((1, LRU_WIDTH)),
                  full((1, LRU_WIDTH))],
        out_specs=pl.BlockSpec((seq, MIX_WIDTH), lambda b: (b, 0)),
        out_shape=jax.ShapeDtypeStruct((t, MIX_WIDTH), BF16),
        compiler_params=_params("arbitrary"),
        name="pool_lru_mixer",
    )(rest, _block_diag(pool_w).astype(BF16), vec(pool_scale), conv_w, vec(conv_b),
      _block_diag(wa).astype(BF16), vec(ba), _block_diag(wi).astype(BF16), vec(bi), vec(lru_lambda))


def _outproj_body(ya_ref, ym_ref, x_ref, w_ref, g_ref, b_ref, *rest, alpha, route):
    if route:
        wr_ref, o_ref, ot_ref, meta_ref, gate_ref, cnt_ref, wbf_ref, run_ref = rest
    else:
        o_ref, wbf_ref = rest

    @pl.when(pl.program_id(0) == 0)
    def _cast_weights():
        for c in range(0, D_MODEL, 256):
            wbf_ref[:, c:c + 256] = w_ref[0, :, c:c + 256].astype(BF16)

    mix = jnp.dot(ya_ref[...], wbf_ref[0:ATTN_WIDTH, :], preferred_element_type=F32)
    mix = mix + jnp.dot(ym_ref[...], wbf_ref[ATTN_WIDTH:, :], preferred_element_type=F32)
    x1 = _layer_norm(alpha * x_ref[...] + mix, g_ref[...], b_ref[...])
    o_ref[...] = x1
    if route:
        _store_row_tiles(ot_ref, x1)
        _route(x1, wr_ref, meta_ref, gate_ref, cnt_ref, run_ref)


def _outproj(y_attn, y_mix, x, w, layer, g, b, alpha, w_router=None):
    t = x.shape[0]
    route = w_router is not None
    row = lambda i: (i, 0)
    const = lambda i: (0, 0)
    in_specs = [pl.BlockSpec((ROW_TILE, ATTN_WIDTH), row), pl.BlockSpec((ROW_TILE, MIX_WIDTH), row),
                pl.BlockSpec((ROW_TILE, D_MODEL), row),
                pl.BlockSpec((1, D_MODEL, D_MODEL), lambda i: (layer, 0, 0)),
                pl.BlockSpec((1, D_MODEL), const), pl.BlockSpec((1, D_MODEL), const)]
    args = [y_attn, y_mix, x, w, g.reshape(1, -1), b.reshape(1, -1)]
    out_specs = [pl.BlockSpec((ROW_TILE, D_MODEL), row)]
    out_shape = [jax.ShapeDtypeStruct((t, D_MODEL), F32)]
    scratch = [pltpu.VMEM((D_MODEL, D_MODEL), BF16)]
    if route:
        in_specs.append(pl.BlockSpec((N_EXPERTS, D_MODEL), const))
        args.append(w_router.T)
        out_specs += [pl.BlockSpec((ROW_TILE * SUBLANES, LANES), row),
                      pl.BlockSpec((ROW_TILE, ROUTE_META), row), pl.BlockSpec((ROW_TILE, LANES), row),
                      pl.BlockSpec((SUBLANES, LANES), const)]
        out_shape += [jax.ShapeDtypeStruct((t * SUBLANES, LANES), F32),
                      jax.ShapeDtypeStruct((t, ROUTE_META), jnp.int32), jax.ShapeDtypeStruct((t, LANES), F32),
                      jax.ShapeDtypeStruct((SUBLANES, LANES), F32)]
        scratch.append(pltpu.VMEM((1, LANES), F32))
    return pl.pallas_call(
        functools.partial(_outproj_body, alpha=alpha, route=route),
        grid=(t // ROW_TILE,),
        in_specs=in_specs,
        out_specs=out_specs,
        out_shape=out_shape,
        scratch_shapes=scratch,
        compiler_params=_params("arbitrary"),
        name="outproj_ln_route" if route else "outproj_ln",
    )(*args)


def _store_row_tiles(o_ref, val):
    rows = val.shape[0]
    for s in range(SUBLANES):
        o_ref[pl.ds(s, rows, stride=SUBLANES), :] = val[:, s * LANES:(s + 1) * LANES]


def _load_row_tiles(ref, first_row, rows):
    return [ref[pl.ds(first_row * SUBLANES + s, rows, stride=SUBLANES), :] for s in range(SUBLANES)]


def _tile_window(row):
    return pl.ds(pl.multiple_of(row * SUBLANES, SUBLANES), SUBLANES)


def _gather_copies(src_ref, x_hbm, buf_ref, sem, slot):
    def copy(r):
        return pltpu.make_async_copy(x_hbm.at[_tile_window(src_ref[0, 0, r])], buf_ref.at[slot, _tile_window(r)],
                                     sem.at[slot])
    return copy


def _ffn_body(be_ref, nu_ref, *rest, alpha, fuse_ln, gather):
    if gather:
        src0_ref, src_next_ref, x_hbm, wg_ref, wu_ref, wd_ref, o_ref, xb_ref, acc_ref, buf_ref, sem = rest
    elif fuse_ln:
        x_ref, wg_ref, wu_ref, wd_ref, g_ref, b_ref, o_ref, xb_ref, acc_ref = rest
    else:
        x_ref, wg_ref, wu_ref, wd_ref, o_ref, xb_ref, acc_ref = rest
    i = pl.program_id(0)
    j = pl.program_id(1)
    n_used = nu_ref[0]
    used = i < n_used

    if gather:
        n_rows = buf_ref.shape[1] // SUBLANES
        slot = lax.rem(i, 2)

        def start_block(src_ref, to_slot):
            copy = _gather_copies(src_ref, x_hbm, buf_ref, sem, to_slot)

            def body(r, _):
                copy(r).start()
                return 0
            lax.fori_loop(0, n_rows, body, 0, unroll=DMA_ISSUE_UNROLL)

        @pl.when(jnp.logical_and(used, jnp.logical_and(i == 0, j == 0)))
        def _first_rows():
            start_block(src0_ref, 0)

        @pl.when(jnp.logical_and(used, j == 0))
        def _start():
            pltpu.make_async_copy(buf_ref.at[slot], buf_ref.at[slot], sem.at[slot]).wait()

            @pl.when(i + 1 < n_used)
            def _next_rows():
                start_block(src_next_ref, 1 - slot)

            for s, part in enumerate(_load_row_tiles(buf_ref.at[slot], 0, n_rows)):
                xb_ref[:, s * LANES:(s + 1) * LANES] = part.astype(BF16)
            acc_ref[...] = jnp.zeros_like(acc_ref)
    else:
        @pl.when(jnp.logical_and(used, j == 0))
        def _start():
            xb_ref[...] = x_ref[...].astype(BF16)
            acc_ref[...] = jnp.zeros_like(acc_ref)

    @pl.when(used)
    def _accumulate():
        xb = xb_ref[...]
        gate = jnp.dot(xb, wg_ref[0].astype(BF16), preferred_element_type=F32)
        up = jnp.dot(xb, wu_ref[0].astype(BF16), preferred_element_type=F32)
        hidden = (jax.nn.silu(gate) * up).astype(BF16)
        acc_ref[...] += jnp.dot(hidden, wd_ref[0].astype(BF16), preferred_element_type=F32)

    @pl.when(jnp.logical_and(used, j == pl.num_programs(1) - 1))
    def _finish():
        if fuse_ln:
            o_ref[...] = _layer_norm(alpha * x_ref[...] + acc_ref[...], g_ref[...], b_ref[...])
        elif gather:
            _store_row_tiles(o_ref, acc_ref[...])
        else:
            o_ref[...] = acc_ref[...]

    @pl.when(jnp.logical_and(jnp.logical_not(used), j == 0))
    def _empty():
        o_ref[...] = jnp.zeros_like(o_ref)


def _ffn(block_e, n_used, x, w_gate, w_up, w_down, ln=None, alpha=1.0, src=None):
    gather = src is not None
    n_blocks = block_e.shape[0]
    nj = D_FF // FFN_COLS

    def col(i, j, be, nu):
        return jnp.where(i < nu[0], j, nj - 1)

    def blk(i, be, nu):
        return jnp.minimum(i, nu[0] - 1)

    row = lambda i, j, be, nu: (i, 0)
    w_specs = [pl.BlockSpec((1, D_MODEL, FFN_COLS), lambda i, j, be, nu: (be[blk(i, be, nu)], 0, col(i, j, be, nu))),
               pl.BlockSpec((1, D_MODEL, FFN_COLS), lambda i, j, be, nu: (be[blk(i, be, nu)], 0, col(i, j, be, nu))),
               pl.BlockSpec((1, FFN_COLS, D_MODEL), lambda i, j, be, nu: (be[blk(i, be, nu)], col(i, j, be, nu), 0))]
    scratch = [pltpu.VMEM((FFN_ROWS, D_MODEL), BF16), pltpu.VMEM((FFN_ROWS, D_MODEL), F32)]
    if gather:
        assert ln is None
        src3 = src.reshape(n_blocks, 1, FFN_ROWS)
        in_specs = [pl.BlockSpec((1, 1, FFN_ROWS), lambda i, j, be, nu: (0, 0, 0), memory_space=pltpu.SMEM),
                    pl.BlockSpec((1, 1, FFN_ROWS), lambda i, j, be, nu: (jnp.minimum(i + 1, n_blocks - 1), 0, 0),
                                 memory_space=pltpu.SMEM),
                    pl.BlockSpec(memory_space=pl.ANY)] + w_specs
        args = [src3, src3, x, w_gate, w_up, w_down]
        scratch += [pltpu.VMEM((2, FFN_ROWS * SUBLANES, LANES), F32), pltpu.SemaphoreType.DMA((2,))]
        out_spec = pl.BlockSpec((FFN_ROWS * SUBLANES, LANES), row)
        out_shape = jax.ShapeDtypeStruct((n_blocks * FFN_ROWS * SUBLANES, LANES), F32)
    else:
        out_spec = pl.BlockSpec((FFN_ROWS, D_MODEL), row)
        out_shape = jax.ShapeDtypeStruct((n_blocks * FFN_ROWS, D_MODEL), F32)
        in_specs = [pl.BlockSpec((FFN_ROWS, D_MODEL), row)] + w_specs
        args = [x, w_gate, w_up, w_down]
        if ln is not None:
            in_specs += [pl.BlockSpec((1, D_MODEL), lambda i, j, be, nu: (0, 0))] * 2
            args += [ln[0].reshape(1, -1), ln[1].reshape(1, -1)]
    return pl.pallas_call(
        functools.partial(_ffn_body, alpha=alpha, fuse_ln=ln is not None, gather=gather),
        grid_spec=pltpu.PrefetchScalarGridSpec(
            num_scalar_prefetch=2,
            grid=(n_blocks, nj),
            in_specs=in_specs,
            out_specs=out_spec,
            scratch_shapes=scratch),
        out_shape=out_shape,
        compiler_params=_params("arbitrary", "arbitrary"),
        name="swiglu_ln" if ln is not None else "swiglu_experts",
    )(block_e, n_used, *args)


def _route(x, w_ref, meta_ref, gate_ref, cnt_ref, run_ref):
    tm = x.shape[0]

    @pl.when(pl.program_id(0) == 0)
    def _init():
        run_ref[...] = jnp.zeros_like(run_ref)

    lane_i = lax.broadcasted_iota(jnp.int32, (tm, LANES), 1)
    lane = lane_i.astype(F32)
    logits = jnp.full((tm, LANES), -jnp.inf, F32)
    for e in range(N_EXPERTS):
        logit_e = jnp.sum(x * w_ref[e:e + 1, :], axis=-1, keepdims=True)
        logits = jnp.where(lane_i == e, logit_e, logits)
    m1 = jnp.max(logits, axis=-1, keepdims=True)
    e1 = jnp.min(jnp.where(logits == m1, lane, float(LANES)), axis=-1, keepdims=True)
    rest = jnp.where(lane == e1, -jnp.inf, logits)
    m2 = jnp.max(rest, axis=-1, keepdims=True)
    e2 = jnp.min(jnp.where(rest == m2, lane, float(LANES)), axis=-1, keepdims=True)
    ex = jnp.exp(m2 - m1)
    g1 = 1.0 / (1.0 + ex)
    g2 = ex / (1.0 + ex)

    chosen = jnp.logical_or(lane == e1, lane == e2)
    r_i = lax.broadcasted_iota(jnp.int32, (tm, tm), 0)
    c_i = lax.broadcasted_iota(jnp.int32, (tm, tm), 1)
    earlier = (c_i < r_i).astype(BF16)
    before = jnp.dot(earlier, chosen.astype(BF16), preferred_element_type=F32) + run_ref[...]
    rank1 = jnp.sum(jnp.where(lane == e1, before, 0.0), axis=-1, keepdims=True)
    rank2 = jnp.sum(jnp.where(lane == e2, before, 0.0), axis=-1, keepdims=True)
    total = run_ref[...] + jnp.sum(chosen.astype(F32), axis=0, keepdims=True)
    run_ref[...] = total
    cnt_ref[...] = jnp.broadcast_to(total, cnt_ref.shape)
    meta = jnp.where(lane_i == 0, e1, jnp.where(lane_i == 1, e2, jnp.where(lane_i == 2, rank1, rank2)))
    meta_ref[...] = meta[:, :ROUTE_META].astype(jnp.int32)
    gate_ref[...] = jnp.where(lane_i == 0, g1, g2)


def _combine_body(d0_ref, dn_ref, ys_hbm, gate_ref, x_ref, g_ref, b_ref, o_ref, buf_ref, sem, *, alpha):
    i = pl.program_id(0)
    slot = lax.rem(i, 2)
    n_rows = buf_ref.shape[1] // SUBLANES

    def start_step(d_ref, to_slot):
        def body(r, _):
            pltpu.make_async_copy(ys_hbm.at[_tile_window(d_ref[0, 0, r])], buf_ref.at[to_slot, _tile_window(r)],
                                  sem.at[to_slot]).start()
            return 0
        lax.fori_loop(0, n_rows, body, 0, unroll=DMA_ISSUE_UNROLL)

    @pl.when(i == 0)
    def _first_rows():
        start_step(d0_ref, 0)

    pltpu.make_async_copy(buf_ref.at[slot], buf_ref.at[slot], sem.at[slot]).wait()

    @pl.when(i + 1 < pl.num_programs(0))
    def _next_rows():
        start_step(dn_ref, 1 - slot)

    gates = gate_ref[...]
    g1, g2 = gates[:, 0:1], gates[:, 1:2]
    first = _load_row_tiles(buf_ref.at[slot], 0, MOVE_ROWS)
    second = _load_row_tiles(buf_ref.at[slot], MOVE_ROWS, MOVE_ROWS)
    f = jnp.concatenate([g1 * a + g2 * b for a, b in zip(first, second)], axis=-1)
    o_ref[...] = _layer_norm(alpha * x_ref[...] + f, g_ref[...], b_ref[...])


def _combine(ys, dest1, dest2, gates, x, g, b, alpha):
    t = x.shape[0]
    steps = t // MOVE_ROWS
    dest = jnp.concatenate([dest1.reshape(steps, 1, MOVE_ROWS), dest2.reshape(steps, 1, MOVE_ROWS)], axis=-1)
    idx_shape = (1, 1, TOP_K * MOVE_ROWS)
    row = lambda i: (i, 0)
    const = lambda i: (0, 0)
    return pl.pallas_call(
        functools.partial(_combine_body, alpha=alpha),
        grid=(steps,),
        in_specs=[pl.BlockSpec(idx_shape, lambda i: (0, 0, 0), memory_space=pltpu.SMEM),
                  pl.BlockSpec(idx_shape, lambda i: (jnp.minimum(i + 1, steps - 1), 0, 0), memory_space=pltpu.SMEM),
                  pl.BlockSpec(memory_space=pl.ANY),
                  pl.BlockSpec((MOVE_ROWS, LANES), row), pl.BlockSpec((MOVE_ROWS, D_MODEL), row),
                  pl.BlockSpec((1, D_MODEL), const), pl.BlockSpec((1, D_MODEL), const)],
        out_specs=pl.BlockSpec((MOVE_ROWS, D_MODEL), row),
        out_shape=jax.ShapeDtypeStruct((t, D_MODEL), F32),
        scratch_shapes=[pltpu.VMEM((2, TOP_K * MOVE_ROWS * SUBLANES, LANES), F32), pltpu.SemaphoreType.DMA((2,))],
        compiler_params=_params("arbitrary"),
        name="combine_ln",
    )(dest, dest, ys, gates, x, g.reshape(1, -1), b.reshape(1, -1))


def _moe(x1, x1_tiles, meta, gates, counts, w_gate, w_up, w_down, first_expert, g, b, alpha):
    t = x1.shape[0]
    e1, e2, rank1, rank2 = meta[:, 0], meta[:, 1], meta[:, 2], meta[:, 3]
    sizes = counts[0, :N_EXPERTS].astype(jnp.int32)
    padded = (sizes + FFN_ROWS - 1) // FFN_ROWS * FFN_ROWS
    group_end = jnp.cumsum(padded)
    group_start = group_end - padded
    dest1 = group_start[e1] + rank1
    dest2 = group_start[e2] + rank2
    n_blocks = (t * TOP_K) // FFN_ROWS + N_EXPERTS
    block_start = jnp.arange(n_blocks, dtype=jnp.int32) * FFN_ROWS
    block_e = jnp.minimum(jnp.sum(group_end[None, :] <= block_start[:, None], axis=1), N_EXPERTS - 1)
    n_used = (group_end[-1] // FFN_ROWS).reshape(1)
    tok = jnp.arange(t, dtype=jnp.int32)
    src = jnp.zeros((n_blocks * FFN_ROWS,), jnp.int32).at[jnp.concatenate([dest1, dest2])].set(
        jnp.concatenate([tok, tok]), unique_indices=True, indices_are_sorted=False)
    ys = _ffn((block_e + first_expert).astype(jnp.int32), n_used.astype(jnp.int32), x1_tiles, w_gate, w_up,
              w_down, src=src)
    return _combine(ys, dest1, dest2, gates, x1, g, b, alpha)


def kernel(x, w_in, w_out, attn_lambda, attn_subln_g, pool_w, pool_scale, conv_w, conv_b, lru_wa, lru_ba,
           lru_wi, lru_bi, lru_lambda, ln1_g, ln1_b, ln2_g, ln2_b, ffn_w_gate, ffn_w_up, ffn_w_down,
           router_w, moe_w_gate, moe_w_up, moe_w_down):
    batch, seq, d = x.shape
    depth = w_in.shape[0]
    assert d == D_MODEL and seq % ATTN_TILE == 0 and seq % MIX_ROWS == 0
    t = batch * seq
    assert t % FFN_ROWS == 0 and t % ROW_TILE == 0
    alpha = (2.0 * depth) ** 0.25
    dense_used = jnp.full((1,), t // FFN_ROWS, jnp.int32)
    moe_gate = moe_w_gate.reshape(-1, D_MODEL, D_FF)
    moe_up = moe_w_up.reshape(-1, D_MODEL, D_FF)
    moe_down = moe_w_down.reshape(-1, D_FF, D_MODEL)
    xt = x.reshape(t, d)
    for l in range(depth):
        lambda_init = 0.8 - 0.6 * math.exp(-0.3 * l)
        q, k, v, rest = _inproj(xt, w_in, l)
        y_attn = _attention(q, k, v, attn_lambda[l], attn_subln_g[l], lambda_init, batch, seq)
        y_mix = _mixer(rest, pool_w[l], pool_scale[l], conv_w[l], conv_b[l], lru_wa[l], lru_ba[l],
                       lru_wi[l], lru_bi[l], lru_lambda[l], batch, seq)
        if l % 2 == 0:
            x1, = _outproj(y_attn, y_mix, xt, w_out, l, ln1_g[l], ln1_b[l], alpha)
            dense_blocks = jnp.full((t // FFN_ROWS,), l // 2, jnp.int32)
            xt = _ffn(dense_blocks, dense_used, x1, ffn_w_gate, ffn_w_up, ffn_w_down,
                      ln=(ln2_g[l], ln2_b[l]), alpha=alpha)
        else:
            x1, x1_tiles, meta, gates, counts = _outproj(y_attn, y_mix, xt, w_out, l, ln1_g[l], ln1_b[l], alpha,
                                                         w_router=router_w[l // 2])
            xt = _moe(x1, x1_tiles, meta, gates, counts, moe_gate, moe_up, moe_down, (l // 2) * N_EXPERTS,
                      ln2_g[l], ln2_b[l], alpha)
    return xt.reshape(batch, seq, d)
```

```python
import functools
import math

import jax
import jax.numpy as jnp
from jax import lax
from jax.experimental import pallas as pl
from jax.experimental.pallas import tpu as pltpu

F32 = jnp.float32
BF16 = jnp.bfloat16

D_MODEL = 1024
CHUNK = 64
ATTN_HEADS = 4
ATTN_WIDTH = 512
ATTN_HEAD_DIM = 64
HEAD_WIDTH = 2 * ATTN_HEAD_DIM
POOL_WINDOWS = (2, 4, 8, 16)
POOL_WIDTH = 256
POOL_GROUP_DIM = 64
LRU_WIDTH = 256
LRU_C = 8.0
CONV_WIDTH = 4
REST_WIDTH = POOL_WIDTH + 2 * LRU_WIDTH
IN_WIDTH = 3 * ATTN_WIDTH + REST_WIDTH
MIX_WIDTH = POOL_WIDTH + LRU_WIDTH
D_FF = 2816
N_EXPERTS = 8
TOP_K = 2
LN_EPS = 1e-5
HEAD_NORM_EPS = 1e-5

LANES = 128
SUBLANES = 8
VMEM_LIMIT_BYTES = 56 * 1024 * 1024

ROW_TILE = 512
ATTN_TILE = 256
ATTN_HEADS_PER_STEP = 4
ATTN_SUM_ROWS = 16
ATTN_SCORE_SCALE = ATTN_HEAD_DIM ** -0.5 * math.log2(math.e)
MIX_ROWS = 256
FFN_ROWS = 1024
FFN_COLS = 256
FFN_STEPS = D_FF // FFN_COLS
FFN_GATHER_ROWS_PER_STEP = -(-FFN_ROWS // (FFN_STEPS * SUBLANES)) * SUBLANES
FFN_GATHER_ROWS = FFN_GATHER_ROWS_PER_STEP * FFN_STEPS
MOVE_ROWS = 256
ROUTE_META = 8
DMA_ISSUE_UNROLL = 8
NEG_BIG = -1e30
F32_TINY = float(jnp.finfo(jnp.float32).tiny)
assert MIX_ROWS >= max(POOL_WINDOWS)


def _params(*semantics):
    return pltpu.CompilerParams(dimension_semantics=semantics, vmem_limit_bytes=VMEM_LIMIT_BYTES)


def _layer_norm(z, g, b):
    mu = jnp.mean(z, axis=-1, keepdims=True)
    zc = z - mu
    var = jnp.mean(zc * zc, axis=-1, keepdims=True)
    return zc * lax.rsqrt(var + LN_EPS) * g + b


def _inproj_body(x_ref, w_ref, q_ref, k_ref, v_ref, r_ref, wbf_ref):
    @pl.when(pl.program_id(0) == 0)
    def _cast_weights():
        for c in range(0, IN_WIDTH, 256):
            wbf_ref[:, c:c + 256] = w_ref[0, :, c:c + 256].astype(BF16)

    xb = x_ref[...].astype(BF16)

    def proj(c0, c1):
        return jnp.dot(xb, wbf_ref[:, c0:c1], preferred_element_type=F32)

    q_ref[...] = (proj(0, ATTN_WIDTH) * ATTN_SCORE_SCALE).astype(BF16)
    k_ref[...] = proj(ATTN_WIDTH, 2 * ATTN_WIDTH).astype(BF16)
    v_ref[...] = proj(2 * ATTN_WIDTH, 3 * ATTN_WIDTH).astype(BF16)
    r_ref[...] = proj(3 * ATTN_WIDTH, IN_WIDTH)


def _inproj(x, w, layer):
    t = x.shape[0]
    row = lambda i: (i, 0)
    return pl.pallas_call(
        _inproj_body,
        grid=(t // ROW_TILE,),
        in_specs=[pl.BlockSpec((ROW_TILE, D_MODEL), row),
                  pl.BlockSpec((1, D_MODEL, IN_WIDTH), lambda i: (layer, 0, 0))],
        out_specs=[pl.BlockSpec((ROW_TILE, ATTN_WIDTH), row)] * 3
        + [pl.BlockSpec((ROW_TILE, REST_WIDTH), row)],
        out_shape=[jax.ShapeDtypeStruct((t, ATTN_WIDTH), BF16)] * 3
        + [jax.ShapeDtypeStruct((t, REST_WIDTH), F32)],
        scratch_shapes=[pltpu.VMEM((D_MODEL, IN_WIDTH), BF16)],
        compiler_params=_params("arbitrary"),
        name="inproj",
    )(x, w)


def _attn_body(lam_ref, g_ref, q_ref, k_ref, v_ref, o_ref, vt_ref, *, lambda_init, seq):
    tq = ATTN_TILE
    lv = lam_ref[...]
    lam = (jnp.exp(jnp.sum(lv[0:1] * lv[1:2], axis=-1, keepdims=True))
           - jnp.exp(jnp.sum(lv[2:3] * lv[3:4], axis=-1, keepdims=True)) + lambda_init)
    gain = g_ref[...] * (1.0 - lambda_init)

    heads = range(ATTN_HEADS_PER_STEP)
    cols = [slice(g * HEAD_WIDTH, (g + 1) * HEAD_WIDTH) for g in heads]
    for g in heads:
        for c in range(seq // tq):
            vt_ref[g, c, 0:HEAD_WIDTH, :] = v_ref[c * tq:(c + 1) * tq, cols[g]].astype(F32).T.astype(BF16)
            vt_ref[g, c, HEAD_WIDTH:, :] = jnp.ones((ATTN_SUM_ROWS, tq), BF16)

    lane = lax.broadcasted_iota(jnp.int32, (tq, HEAD_WIDTH), 1)
    first_map = lane < ATTN_HEAD_DIM
    key = lax.broadcasted_iota(jnp.int32, (tq, 2 * tq), 0)
    qry = lax.broadcasted_iota(jnp.int32, (tq, 2 * tq), 1)
    q_chunk = jnp.where(qry >= tq, qry - tq, qry) // CHUNK
    visible = (key // CHUNK) <= q_chunk

    def q_block(i, _):
        q0 = pl.multiple_of(i * tq, tq)
        qqs = []
        for g in heads:
            q = q_ref[pl.ds(q0, tq), cols[g]]
            zero = jnp.zeros_like(q)
            qqs.append(jnp.concatenate([jnp.where(first_map, q, zero), jnp.where(first_map, zero, q)], axis=0))

        def step(j, carry, masked):
            k0 = pl.multiple_of(j * tq, tq)
            out = []
            scores = [lax.dot_general(k_ref[pl.ds(k0, tq), cols[g]], qqs[g], (((1,), (1,)), ((), ())),
                                      preferred_element_type=F32) for g in heads]
            for g in heads:
                m, acc = carry[g]
                s = scores[g]
                if masked:
                    s = jnp.where(visible, s, NEG_BIG)
                m_new = jnp.maximum(m, jnp.max(s, axis=0, keepdims=True))
                p = jnp.exp2(s - m_new)
                scale = jnp.exp2(m - m_new)
                pv = jnp.dot(vt_ref[g, j], p.astype(BF16), preferred_element_type=F32)
                out.append((m_new, scale * acc + pv))
            return tuple(out)

        init = tuple((jnp.full((1, 2 * tq), NEG_BIG, F32),
                      jnp.zeros((HEAD_WIDTH + ATTN_SUM_ROWS, 2 * tq), F32)) for _ in heads)
        carry = lax.fori_loop(0, i, lambda j, c: step(j, c, False), init)
        carry = step(i, carry, True)
        for g in heads:
            _, acc = carry[g]
            l = acc[HEAD_WIDTH:HEAD_WIDTH + 1, :]
            acc = acc[:HEAD_WIDTH, :]
            o = acc[:, :tq] / l[:, :tq] - lam * (acc[:, tq:] / l[:, tq:])
            o = o * lax.rsqrt(jnp.mean(o * o, axis=0, keepdims=True) + HEAD_NORM_EPS)
            o_ref[pl.ds(q0, tq), cols[g]] = (o.T * gain).astype(BF16)
        return 0

    lax.fori_loop(0, seq // tq, q_block, 0)


def _attention(q, k, v, lam_params, subln_g, lambda_init, batch, seq):
    t = q.shape[0]
    blk = pl.BlockSpec((seq, ATTN_HEADS_PER_STEP * HEAD_WIDTH), lambda b, h: (b, h))
    return pl.pallas_call(
        functools.partial(_attn_body, lambda_init=lambda_init, seq=seq),
        grid=(batch, ATTN_HEADS // ATTN_HEADS_PER_STEP),
        in_specs=[pl.BlockSpec((4, ATTN_HEAD_DIM), lambda b, h: (0, 0)),
                  pl.BlockSpec((1, HEAD_WIDTH), lambda b, h: (0, 0)),
                  blk, blk, blk],
        out_specs=blk,
        out_shape=jax.ShapeDtypeStruct((t, ATTN_WIDTH), BF16),
        scratch_shapes=[pltpu.VMEM((ATTN_HEADS_PER_STEP, seq // ATTN_TILE, HEAD_WIDTH + ATTN_SUM_ROWS, ATTN_TILE),
                                   BF16)],
        compiler_params=_params("arbitrary", "arbitrary"),
        name="diff_attention",
    )(lam_params, subln_g.reshape(1, HEAD_WIDTH), q, k, v)


def _mixer_body(r_ref, pw_ref, ps_ref, cw_ref, cb_ref, wa_ref, ba_ref, wi_ref, bi_ref, lam_ref, y_ref,
                *, seq):
    rows = MIX_ROWS
    pool_hist = max(POOL_WINDOWS)
    conv_hist = SUBLANES
    pw = pw_ref[...]
    wa = wa_ref[...]
    wi = wi_ref[...]
    ps, cb, ba, bi = ps_ref[...], cb_ref[...], ba_ref[...], bi_ref[...]
    cw = cw_ref[...]
    neg_c_softplus = -LRU_C * jax.nn.softplus(-lam_ref[...])

    lane_e = lax.broadcasted_iota(jnp.int32, (rows + pool_hist, POOL_WIDTH), 1)
    lane = lax.broadcasted_iota(jnp.int32, (rows, POOL_WIDTH), 1)
    row = lax.broadcasted_iota(jnp.int32, (rows, POOL_WIDTH), 0)
    win = jnp.where(lane < 64, 2, jnp.where(lane < 128, 4, jnp.where(lane < 192, 8, 16)))
    inv_win = 1.0 / win.astype(F32)
    inv_first = 1.0 / (row + 1).astype(F32)

    def chunk(c, carry):
        tail_u, tail_x, h_prev = carry
        r0 = pl.multiple_of(c * rows, rows)
        u = r_ref[pl.ds(r0, rows), 0:POOL_WIDTH]
        xr = r_ref[pl.ds(r0, rows), POOL_WIDTH:POOL_WIDTH + LRU_WIDTH]
        xg = r_ref[pl.ds(r0, rows), POOL_WIDTH + LRU_WIDTH:REST_WIDTH]

        ue = jnp.concatenate([tail_u, u], axis=0)
        w2 = ue + pltpu.roll(ue, 1, 0)
        w4 = w2 + pltpu.roll(w2, 2, 0)
        w8 = w4 + pltpu.roll(w4, 4, 0)
        w16 = w8 + pltpu.roll(w8, 8, 0)
        ws = jnp.where(lane_e < 64, w2, jnp.where(lane_e < 128, w4, jnp.where(lane_e < 192, w8, w16)))
        ws = ws[pool_hist:]
        inv_count = jnp.where(r0 + row + 1 >= win, inv_win, inv_first)
        pooled = ws * inv_count - u
        y_pool = jnp.dot(pooled.astype(BF16), pw, preferred_element_type=F32) * ps

        xe = jnp.concatenate([tail_x, xr], axis=0)
        xc = cb + pltpu.roll(xe, 3, 0)[conv_hist:] * cw[0:1]
        xc = xc + pltpu.roll(xe, 2, 0)[conv_hist:] * cw[1:2]
        xc = xc + pltpu.roll(xe, 1, 0)[conv_hist:] * cw[2:3]
        xc = xc + xr * cw[3:4]
        xcb = xc.astype(BF16)
        r_gate = jax.nn.sigmoid(jnp.dot(xcb, wa, preferred_element_type=F32) + ba)
        i_gate = jax.nn.sigmoid(jnp.dot(xcb, wi, preferred_element_type=F32) + bi)
        log_a = r_gate * neg_c_softplus
        a = jnp.exp(log_a)
        gap = -jnp.tanh(log_a) * (a * a + 1.0)
        b = (gap * lax.rsqrt(jnp.maximum(gap, F32_TINY))) * (i_gate * xc)

        s = 1
        while s < rows:
            keep = row >= s
            a_prev = jnp.where(keep, pltpu.roll(a, s, 0), 1.0)
            b_prev = jnp.where(keep, pltpu.roll(b, s, 0), 0.0)
            b = a * b_prev + b
            a = a * a_prev
            s *= 2
        h = a * h_prev + b
        y_lru = h * jax.nn.gelu(xg)

        y_ref[pl.ds(r0, rows), 0:POOL_WIDTH] = y_pool.astype(BF16)
        y_ref[pl.ds(r0, rows), POOL_WIDTH:MIX_WIDTH] = y_lru.astype(BF16)
        return u[rows - pool_hist:], xr[rows - conv_hist:], h[rows - 1:rows]

    init = (jnp.zeros((pool_hist, POOL_WIDTH), F32), jnp.zeros((conv_hist, LRU_WIDTH), F32),
            jnp.zeros((1, LRU_WIDTH), F32))
    lax.fori_loop(0, seq // rows, chunk, init)


def _block_diag(w):
    g, c, d = w.shape
    eye = jnp.eye(g, dtype=w.dtype)
    return (eye[:, None, :, None] * w[:, :, None, :]).reshape(g * c, g * d)


def _mixer(rest, pool_w, pool_scale, conv_w, conv_b, wa, ba, wi, bi, lru_lambda, batch, seq):
    t = rest.shape[0]
    full = lambda shape: pl.BlockSpec(shape, lambda b: (0, 0))
    vec = lambda a: a.reshape(1, -1)
    return pl.pallas_call(
        functools.partial(_mixer_body, seq=seq),
        grid=(batch,),
        in_specs=[pl.BlockSpec((seq, REST_WIDTH), lambda b: (b, 0)),
                  full((POOL_WIDTH, POOL_WIDTH)), full((1, POOL_WIDTH)),
                  full((CONV_WIDTH, LRU_WIDTH)), full((1, LRU_WIDTH)),
                  full((LRU_WIDTH, LRU_WIDTH)), full((1, LRU_WIDTH)),
                  full((LRU_WIDTH, LRU_WIDTH)), full((1, LRU_WIDTH)),
                  full((1, LRU_WIDTH))],
        out_specs=pl.BlockSpec((seq, MIX_WIDTH), lambda b: (b, 0)),
        out_shape=jax.ShapeDtypeStruct((t, MIX_WIDTH), BF16),
        compiler_params=_params("arbitrary"),
        name="pool_lru_mixer",
    )(rest, _block_diag(pool_w).astype(BF16), vec(pool_scale), conv_w, vec(conv_b),
      _block_diag(wa).astype(BF16), vec(ba), _block_diag(wi).astype(BF16), vec(bi), vec(lru_lambda))


def _outproj_body(ya_ref, ym_ref, x_ref, w_ref, g_ref, b_ref, *rest, alpha, route):
    if route:
        wr_ref, o_ref, ot_ref, meta_ref, gate_ref, cnt_ref, wbf_ref, run_ref = rest
    else:
        o_ref, wbf_ref = rest

    @pl.when(pl.program_id(0) == 0)
    def _cast_weights():
        for c in range(0, D_MODEL, 256):
            wbf_ref[:, c:c + 256] = w_ref[0, :, c:c + 256].astype(BF16)

    mix = jnp.dot(ya_ref[...], wbf_ref[0:ATTN_WIDTH, :], preferred_element_type=F32)
    mix = mix + jnp.dot(ym_ref[...], wbf_ref[ATTN_WIDTH:, :], preferred_element_type=F32)
    x1 = _layer_norm(alpha * x_ref[...] + mix, g_ref[...], b_ref[...])
    o_ref[...] = x1
    if route:
        _store_row_tiles(ot_ref, x1)
        _route(x1, wr_ref, meta_ref, gate_ref, cnt_ref, run_ref)


def _outproj(y_attn, y_mix, x, w, layer, g, b, alpha, w_router=None):
    t = x.shape[0]
    route = w_router is not None
    row = lambda i: (i, 0)
    const = lambda i: (0, 0)
    in_specs = [pl.BlockSpec((ROW_TILE, ATTN_WIDTH), row), pl.BlockSpec((ROW_TILE, MIX_WIDTH), row),
                pl.BlockSpec((ROW_TILE, D_MODEL), row),
                pl.BlockSpec((1, D_MODEL, D_MODEL), lambda i: (layer, 0, 0)),
                pl.BlockSpec((1, D_MODEL), const), pl.BlockSpec((1, D_MODEL), const)]
    args = [y_attn, y_mix, x, w, g.reshape(1, -1), b.reshape(1, -1)]
    out_specs = [pl.BlockSpec((ROW_TILE, D_MODEL), row)]
    out_shape = [jax.ShapeDtypeStruct((t, D_MODEL), F32)]
    scratch = [pltpu.VMEM((D_MODEL, D_MODEL), BF16)]
    if route:
        in_specs.append(pl.BlockSpec((N_EXPERTS, D_MODEL), const))
        args.append(w_router.T)
        out_specs += [pl.BlockSpec((ROW_TILE * SUBLANES, LANES), row),
                      pl.BlockSpec((ROW_TILE, ROUTE_META), row), pl.BlockSpec((ROW_TILE, LANES), row),
                      pl.BlockSpec((SUBLANES, LANES), const)]
        out_shape += [jax.ShapeDtypeStruct((t * SUBLANES, LANES), F32),
                      jax.ShapeDtypeStruct((t, ROUTE_META), jnp.int32), jax.ShapeDtypeStruct((t, LANES), F32),
                      jax.ShapeDtypeStruct((SUBLANES, LANES), F32)]
        scratch.append(pltpu.VMEM((1, LANES), F32))
    return pl.pallas_call(
        functools.partial(_outproj_body, alpha=alpha, route=route),
        grid=(t // ROW_TILE,),
        in_specs=in_specs,
        out_specs=out_specs,
        out_shape=out_shape,
        scratch_shapes=scratch,
        compiler_params=_params("arbitrary"),
        name="outproj_ln_route" if route else "outproj_ln",
    )(*args)


def _store_row_tiles(o_ref, val):
    rows = val.shape[0]
    for s in range(SUBLANES):
        o_ref[pl.ds(s, rows, stride=SUBLANES), :] = val[:, s * LANES:(s + 1) * LANES]


def _load_row_tiles(ref, first_row, rows):
    return [ref[pl.ds(first_row * SUBLANES + s, rows, stride=SUBLANES), :] for s in range(SUBLANES)]


def _tile_window(row):
    return pl.ds(pl.multiple_of(row * SUBLANES, SUBLANES), SUBLANES)


def _start_row_copies(n_rows, make_copy, priorities):
    k = len(priorities)

    def body(q, _):
        for p, priority in enumerate(priorities):
            make_copy(q * k + p).start(priority=priority)
        return 0
    lax.fori_loop(0, n_rows // k, body, 0, unroll=DMA_ISSUE_UNROLL // k)


def _gather_copies(src_ref, x_hbm, buf_ref, sem, slot):
    def copy(r):
        return pltpu.make_async_copy(x_hbm.at[_tile_window(src_ref[0, 0, r])], buf_ref.at[slot, _tile_window(r)],
                                     sem.at[slot])
    return copy


def _ffn_body(be_ref, nu_ref, *rest, alpha, fuse_ln, gather):
    if gather:
        src0_ref, src_next_ref, x_hbm, wg_ref, wu_ref, wd_ref, o_ref, xb_ref, acc_ref, buf_ref, sem = rest
    elif fuse_ln:
        x_ref, wg_ref, wu_ref, wd_ref, g_ref, b_ref, o_ref, xb_ref, acc_ref = rest
    else:
        x_ref, wg_ref, wu_ref, wd_ref, o_ref, xb_ref, acc_ref = rest
    i = pl.program_id(0)
    j = pl.program_id(1)
    n_used = nu_ref[0]
    used = i < n_used

    if gather:
        slot = lax.rem(i, 2)
        priority = 1

        def wait_slot(which):
            pltpu.make_async_copy(buf_ref.at[which], buf_ref.at[which], sem.at[which]).wait()

        @pl.when(jnp.logical_and(used, jnp.logical_and(i == 0, j == 0)))
        def _first_rows():
            _start_row_copies(FFN_GATHER_ROWS, _gather_copies(src0_ref, x_hbm, buf_ref, sem, 0),
                              priorities=(priority,))

        @pl.when(jnp.logical_and(used, j == 0))
        def _start():
            wait_slot(slot)
            for s, part in enumerate(_load_row_tiles(buf_ref.at[slot], 0, FFN_ROWS)):
                xb_ref[:, s * LANES:(s + 1) * LANES] = part.astype(BF16)
            acc_ref[...] = jnp.zeros_like(acc_ref)
    else:
        @pl.when(jnp.logical_and(used, j == 0))
        def _start():
            xb_ref[...] = x_ref[...].astype(BF16)
            acc_ref[...] = jnp.zeros_like(acc_ref)

    @pl.when(used)
    def _accumulate():
        if gather:
            copy = _gather_copies(src_next_ref, x_hbm, buf_ref, sem, 1 - slot)
            for r in range(FFN_GATHER_ROWS_PER_STEP):
                copy(j * FFN_GATHER_ROWS_PER_STEP + r).start(priority=priority)
        xb = xb_ref[...]
        gate = jnp.dot(xb, wg_ref[0].astype(BF16), preferred_element_type=F32)
        up = jnp.dot(xb, wu_ref[0].astype(BF16), preferred_element_type=F32)
        hidden = (jax.nn.silu(gate) * up).astype(BF16)
        acc_ref[...] += jnp.dot(hidden, wd_ref[0].astype(BF16), preferred_element_type=F32)

    @pl.when(jnp.logical_and(used, j == pl.num_programs(1) - 1))
    def _finish():
        if gather:
            @pl.when(i == n_used - 1)
            def _drain():
                wait_slot(1 - slot)
        if fuse_ln:
            o_ref[...] = _layer_norm(alpha * x_ref[...] + acc_ref[...], g_ref[...], b_ref[...])
        elif gather:
            _store_row_tiles(o_ref, acc_ref[...])
        else:
            o_ref[...] = acc_ref[...]

    @pl.when(jnp.logical_and(jnp.logical_not(used), j == 0))
    def _empty():
        o_ref[...] = jnp.zeros_like(o_ref)


def _ffn(block_e, n_used, x, w_gate, w_up, w_down, ln=None, alpha=1.0, src=None):
    gather = src is not None
    n_blocks = block_e.shape[0]
    nj = D_FF // FFN_COLS

    def col(i, j, be, nu):
        return jnp.where(i < nu[0], j, nj - 1)

    def blk(i, be, nu):
        return jnp.minimum(i, nu[0] - 1)

    row = lambda i, j, be, nu: (i, 0)
    w_specs = [pl.BlockSpec((1, D_MODEL, FFN_COLS), lambda i, j, be, nu: (be[blk(i, be, nu)], 0, col(i, j, be, nu))),
               pl.BlockSpec((1, D_MODEL, FFN_COLS), lambda i, j, be, nu: (be[blk(i, be, nu)], 0, col(i, j, be, nu))),
               pl.BlockSpec((1, FFN_COLS, D_MODEL), lambda i, j, be, nu: (be[blk(i, be, nu)], col(i, j, be, nu), 0))]
    scratch = [pltpu.VMEM((FFN_ROWS, D_MODEL), BF16), pltpu.VMEM((FFN_ROWS, D_MODEL), F32)]
    if gather:
        assert ln is None
        src3 = jnp.pad(src.reshape(n_blocks, 1, FFN_ROWS), ((0, 0), (0, 0), (0, FFN_GATHER_ROWS - FFN_ROWS)))
        in_specs = [pl.BlockSpec((1, 1, FFN_GATHER_ROWS), lambda i, j, be, nu: (0, 0, 0), memory_space=pltpu.SMEM),
                    pl.BlockSpec((1, 1, FFN_GATHER_ROWS),
                                 lambda i, j, be, nu: (jnp.minimum(i + 1, n_blocks - 1), 0, 0),
                                 memory_space=pltpu.SMEM),
                    pl.BlockSpec(memory_space=pl.ANY)] + w_specs
        args = [src3, src3, x, w_gate, w_up, w_down]
        scratch += [pltpu.VMEM((2, FFN_GATHER_ROWS * SUBLANES, LANES), F32), pltpu.SemaphoreType.DMA((2,))]
        out_spec = pl.BlockSpec((FFN_ROWS * SUBLANES, LANES), row)
        out_shape = jax.ShapeDtypeStruct((n_blocks * FFN_ROWS * SUBLANES, LANES), F32)
    else:
        out_spec = pl.BlockSpec((FFN_ROWS, D_MODEL), row)
        out_shape = jax.ShapeDtypeStruct((n_blocks * FFN_ROWS, D_MODEL), F32)
        in_specs = [pl.BlockSpec((FFN_ROWS, D_MODEL), row)] + w_specs
        args = [x, w_gate, w_up, w_down]
        if ln is not None:
            in_specs += [pl.BlockSpec((1, D_MODEL), lambda i, j, be, nu: (0, 0))] * 2
            args += [ln[0].reshape(1, -1), ln[1].reshape(1, -1)]
    return pl.pallas_call(
        functools.partial(_ffn_body, alpha=alpha, fuse_ln=ln is not None, gather=gather),
        grid_spec=pltpu.PrefetchScalarGridSpec(
            num_scalar_prefetch=2,
            grid=(n_blocks, nj),
            in_specs=in_specs,
            out_specs=out_spec,
            scratch_shapes=scratch),
        out_shape=out_shape,
        compiler_params=_params("arbitrary", "arbitrary"),
        name="swiglu_ln" if ln is not None else "swiglu_experts",
    )(block_e, n_used, *args)


def _route(x, w_ref, meta_ref, gate_ref, cnt_ref, run_ref):
    tm = x.shape[0]

    @pl.when(pl.program_id(0) == 0)
    def _init():
        run_ref[...] = jnp.zeros_like(run_ref)

    lane_i = lax.broadcasted_iota(jnp.int32, (tm, LANES), 1)
    lane = lane_i.astype(F32)
    logits = jnp.full((tm, LANES), -jnp.inf, F32)
    for e in range(N_EXPERTS):
        logit_e = jnp.sum(x * w_ref[e:e + 1, :], axis=-1, keepdims=True)
        logits = jnp.where(lane_i == e, logit_e, logits)
    m1 = jnp.max(logits, axis=-1, keepdims=True)
    e1 = jnp.min(jnp.where(logits == m1, lane, float(LANES)), axis=-1, keepdims=True)
    rest = jnp.where(lane == e1, -jnp.inf, logits)
    m2 = jnp.max(rest, axis=-1, keepdims=True)
    e2 = jnp.min(jnp.where(rest == m2, lane, float(LANES)), axis=-1, keepdims=True)
    ex = jnp.exp(m2 - m1)
    g1 = 1.0 / (1.0 + ex)
    g2 = ex / (1.0 + ex)

    chosen = jnp.logical_or(lane == e1, lane == e2)
    r_i = lax.broadcasted_iota(jnp.int32, (tm, tm), 0)
    c_i = lax.broadcasted_iota(jnp.int32, (tm, tm), 1)
    earlier = (c_i < r_i).astype(BF16)
    before = jnp.dot(earlier, chosen.astype(BF16), preferred_element_type=F32) + run_ref[...]
    rank1 = jnp.sum(jnp.where(lane == e1, before, 0.0), axis=-1, keepdims=True)
    rank2 = jnp.sum(jnp.where(lane == e2, before, 0.0), axis=-1, keepdims=True)
    total = run_ref[...] + jnp.sum(chosen.astype(F32), axis=0, keepdims=True)
    run_ref[...] = total
    cnt_ref[...] = jnp.broadcast_to(total, cnt_ref.shape)
    meta = jnp.where(lane_i == 0, e1, jnp.where(lane_i == 1, e2, jnp.where(lane_i == 2, rank1, rank2)))
    meta_ref[...] = meta[:, :ROUTE_META].astype(jnp.int32)
    gate_ref[...] = jnp.where(lane_i == 0, g1, g2)


def _combine_body(d0_ref, dn_ref, ys_hbm, gate_ref, x_ref, g_ref, b_ref, o_ref, buf_ref, sem, *, alpha):
    i = pl.program_id(0)
    slot = lax.rem(i, 2)
    n_rows = buf_ref.shape[1] // SUBLANES

    def row_copy(d_ref, to_slot):
        def copy(r):
            return pltpu.make_async_copy(ys_hbm.at[_tile_window(d_ref[0, 0, r])],
                                         buf_ref.at[to_slot, _tile_window(r)], sem.at[to_slot])
        return copy

    def wait_slot(which):
        pltpu.make_async_copy(buf_ref.at[which], buf_ref.at[which], sem.at[which]).wait()

    @pl.when(i == 0)
    def _first_rows():
        _start_row_copies(n_rows, row_copy(d0_ref, 0), priorities=(0, 1))

    wait_slot(slot)
    next_copy = row_copy(dn_ref, 1 - slot)
    for r in range(n_rows):
        next_copy(r).start(priority=r % 2)

    gates = gate_ref[...]
    g1, g2 = gates[:, 0:1], gates[:, 1:2]
    first = _load_row_tiles(buf_ref.at[slot], 0, MOVE_ROWS)
    second = _load_row_tiles(buf_ref.at[slot], MOVE_ROWS, MOVE_ROWS)
    f = jnp.concatenate([g1 * a + g2 * b for a, b in zip(first, second)], axis=-1)
    o_ref[...] = _layer_norm(alpha * x_ref[...] + f, g_ref[...], b_ref[...])

    @pl.when(i == pl.num_programs(0) - 1)
    def _drain():
        wait_slot(1 - slot)


def _combine(ys, dest1, dest2, gates, x, g, b, alpha):
    t = x.shape[0]
    steps = t // MOVE_ROWS
    dest = jnp.concatenate([dest1.reshape(steps, 1, MOVE_ROWS), dest2.reshape(steps, 1, MOVE_ROWS)], axis=-1)
    idx_shape = (1, 1, TOP_K * MOVE_ROWS)
    row = lambda i: (i, 0)
    const = lambda i: (0, 0)
    return pl.pallas_call(
        functools.partial(_combine_body, alpha=alpha),
        grid=(steps,),
        in_specs=[pl.BlockSpec(idx_shape, lambda i: (0, 0, 0), memory_space=pltpu.SMEM),
                  pl.BlockSpec(idx_shape, lambda i: (jnp.minimum(i + 1, steps - 1), 0, 0), memory_space=pltpu.SMEM),
                  pl.BlockSpec(memory_space=pl.ANY),
                  pl.BlockSpec((MOVE_ROWS, LANES), row), pl.BlockSpec((MOVE_ROWS, D_MODEL), row),
                  pl.BlockSpec((1, D_MODEL), const), pl.BlockSpec((1, D_MODEL), const)],
        out_specs=pl.BlockSpec((MOVE_ROWS, D_MODEL), row),
        out_shape=jax.ShapeDtypeStruct((t, D_MODEL), F32),
        scratch_shapes=[pltpu.VMEM((2, TOP_K * MOVE_ROWS * SUBLANES, LANES), F32), pltpu.SemaphoreType.DMA((2,))],
        compiler_params=_params("arbitrary"),
        name="combine_ln",
    )(dest, dest, ys, gates, x, g.reshape(1, -1), b.reshape(1, -1))


def _moe(x1, x1_tiles, meta, gates, counts, w_gate, w_up, w_down, first_expert, g, b, alpha):
    t = x1.shape[0]
    e1, e2, rank1, rank2 = meta[:, 0], meta[:, 1], meta[:, 2], meta[:, 3]
    sizes = counts[0, :N_EXPERTS].astype(jnp.int32)
    padded = (sizes + FFN_ROWS - 1) // FFN_ROWS * FFN_ROWS
    group_end = jnp.cumsum(padded)
    group_start = group_end - padded
    dest1 = group_start[e1] + rank1
    dest2 = group_start[e2] + rank2
    n_blocks = (t * TOP_K) // FFN_ROWS + N_EXPERTS
    block_start = jnp.arange(n_blocks, dtype=jnp.int32) * FFN_ROWS
    block_e = jnp.minimum(jnp.sum(group_end[None, :] <= block_start[:, None], axis=1), N_EXPERTS - 1)
    n_used = (group_end[-1] // FFN_ROWS).reshape(1)
    tok = jnp.arange(t, dtype=jnp.int32)
    src = jnp.zeros((n_blocks * FFN_ROWS,), jnp.int32).at[jnp.concatenate([dest1, dest2])].set(
        jnp.concatenate([tok, tok]), unique_indices=True, indices_are_sorted=False)
    ys = _ffn((block_e + first_expert).astype(jnp.int32), n_used.astype(jnp.int32), x1_tiles, w_gate, w_up,
              w_down, src=src)
    return _combine(ys, dest1, dest2, gates, x1, g, b, alpha)


def kernel(x, w_in, w_out, attn_lambda, attn_subln_g, pool_w, pool_scale, conv_w, conv_b, lru_wa, lru_ba,
           lru_wi, lru_bi, lru_lambda, ln1_g, ln1_b, ln2_g, ln2_b, ffn_w_gate, ffn_w_up, ffn_w_down,
           router_w, moe_w_gate, moe_w_up, moe_w_down):
    batch, seq, d = x.shape
    depth = w_in.shape[0]
    assert d == D_MODEL and seq % ATTN_TILE == 0 and seq % MIX_ROWS == 0
    t = batch * seq
    assert t % FFN_ROWS == 0 and t % ROW_TILE == 0
    alpha = (2.0 * depth) ** 0.25
    dense_used = jnp.full((1,), t // FFN_ROWS, jnp.int32)
    moe_gate = moe_w_gate.reshape(-1, D_MODEL, D_FF)
    moe_up = moe_w_up.reshape(-1, D_MODEL, D_FF)
    moe_down = moe_w_down.reshape(-1, D_FF, D_MODEL)
    xt = x.reshape(t, d)
    for l in range(depth):
        lambda_init = 0.8 - 0.6 * math.exp(-0.3 * l)
        q, k, v, rest = _inproj(xt, w_in, l)
        y_attn = _attention(q, k, v, attn_lambda[l], attn_subln_g[l], lambda_init, batch, seq)
        y_mix = _mixer(rest, pool_w[l], pool_scale[l], conv_w[l], conv_b[l], lru_wa[l], lru_ba[l],
                       lru_wi[l], lru_bi[l], lru_lambda[l], batch, seq)
        if l % 2 == 0:
            x1, = _outproj(y_attn, y_mix, xt, w_out, l, ln1_g[l], ln1_b[l], alpha)
            dense_blocks = jnp.full((t // FFN_ROWS,), l // 2, jnp.int32)
            xt = _ffn(dense_blocks, dense_used, x1, ffn_w_gate, ffn_w_up, ffn_w_down,
                      ln=(ln2_g[l], ln2_b[l]), alpha=alpha)
        else:
            x1, x1_tiles, meta, gates, counts = _outproj(y_attn, y_mix, xt, w_out, l, ln1_g[l], ln1_b[l], alpha,
                                                         w_router=router_w[l // 2])
            xt = _moe(x1, x1_tiles, meta, gates, counts, moe_gate, moe_up, moe_down, (l // 2) * N_EXPERTS,
                      ln2_g[l], ln2_b[l], alpha)
    return xt.reshape(batch, seq, d)
```

```python
import functools
import math

import jax
import jax.numpy as jnp
from jax import lax
from jax.experimental import pallas as pl
from jax.experimental.pallas import tpu as pltpu

F32 = jnp.float32
BF16 = jnp.bfloat16

D_MODEL = 1024
CHUNK = 64
ATTN_HEADS = 4
ATTN_WIDTH = 512
ATTN_HEAD_DIM = 64
HEAD_WIDTH = 2 * ATTN_HEAD_DIM
POOL_WINDOWS = (2, 4, 8, 16)
POOL_WIDTH = 256
POOL_GROUP_DIM = 64
LRU_WIDTH = 256
LRU_C = 8.0
CONV_WIDTH = 4
REST_WIDTH = POOL_WIDTH + 2 * LRU_WIDTH
IN_WIDTH = 3 * ATTN_WIDTH + REST_WIDTH
MIX_WIDTH = POOL_WIDTH + LRU_WIDTH
D_FF = 2816
N_EXPERTS = 8
TOP_K = 2
LN_EPS = 1e-5
HEAD_NORM_EPS = 1e-5

LANES = 128
SUBLANES = 8
VMEM_LIMIT_BYTES = 56 * 1024 * 1024

ROW_TILE = 512
ATTN_TILE = 256
ATTN_HEADS_PER_STEP = 4
ATTN_SUM_ROWS = 16
ATTN_SCORE_SCALE = ATTN_HEAD_DIM ** -0.5 * math.log2(math.e)
MIX_ROWS = 256
FFN_ROWS = 1024
FFN_COLS = 256
FFN_STEPS = D_FF // FFN_COLS
FFN_GATHER_ROWS_PER_STEP = -(-FFN_ROWS // ((FFN_STEPS - 1) * SUBLANES)) * SUBLANES
FFN_GATHER_ROWS = FFN_GATHER_ROWS_PER_STEP * FFN_STEPS
MOVE_ROWS = 256
ROUTE_META = 8
SCALAR_LOOP_UNROLL = 8
DMA_ISSUE_UNROLL = 8
NEG_BIG = -1e30
F32_TINY = float(jnp.finfo(jnp.float32).tiny)
assert MIX_ROWS >= max(POOL_WINDOWS)


def _params(*semantics):
    return pltpu.CompilerParams(dimension_semantics=semantics, vmem_limit_bytes=VMEM_LIMIT_BYTES)


def _layer_norm(z, g, b):
    mu = jnp.mean(z, axis=-1, keepdims=True)
    zc = z - mu
    var = jnp.mean(zc * zc, axis=-1, keepdims=True)
    return zc * lax.rsqrt(var + LN_EPS) * g + b


def _inproj_body(x_ref, w_ref, q_ref, k_ref, v_ref, r_ref, wbf_ref):
    @pl.when(pl.program_id(0) == 0)
    def _cast_weights():
        for c in range(0, IN_WIDTH, 256):
            wbf_ref[:, c:c + 256] = w_ref[0, :, c:c + 256].astype(BF16)

    xb = x_ref[...].astype(BF16)

    def proj(c0, c1):
        return jnp.dot(xb, wbf_ref[:, c0:c1], preferred_element_type=F32)

    q_ref[...] = (proj(0, ATTN_WIDTH) * ATTN_SCORE_SCALE).astype(BF16)
    k_ref[...] = proj(ATTN_WIDTH, 2 * ATTN_WIDTH).astype(BF16)
    v_ref[...] = proj(2 * ATTN_WIDTH, 3 * ATTN_WIDTH).astype(BF16)
    r_ref[...] = proj(3 * ATTN_WIDTH, IN_WIDTH)


def _inproj(x, w, layer):
    t = x.shape[0]
    row = lambda i: (i, 0)
    return pl.pallas_call(
        _inproj_body,
        grid=(t // ROW_TILE,),
        in_specs=[pl.BlockSpec((ROW_TILE, D_MODEL), row),
                  pl.BlockSpec((1, D_MODEL, IN_WIDTH), lambda i: (layer, 0, 0))],
        out_specs=[pl.BlockSpec((ROW_TILE, ATTN_WIDTH), row)] * 3
        + [pl.BlockSpec((ROW_TILE, REST_WIDTH), row)],
        out_shape=[jax.ShapeDtypeStruct((t, ATTN_WIDTH), BF16)] * 3
        + [jax.ShapeDtypeStruct((t, REST_WIDTH), F32)],
        scratch_shapes=[pltpu.VMEM((D_MODEL, IN_WIDTH), BF16)],
        compiler_params=_params("arbitrary"),
        name="inproj",
    )(x, w)


def _attn_body(lam_ref, g_ref, q_ref, k_ref, v_ref, o_ref, vt_ref, *, lambda_init, seq):
    tq = ATTN_TILE
    lv = lam_ref[...]
    lam = (jnp.exp(jnp.sum(lv[0:1] * lv[1:2], axis=-1, keepdims=True))
           - jnp.exp(jnp.sum(lv[2:3] * lv[3:4], axis=-1, keepdims=True)) + lambda_init)
    gain = g_ref[...] * (1.0 - lambda_init)

    heads = range(ATTN_HEADS_PER_STEP)
    cols = [slice(g * HEAD_WIDTH, (g + 1) * HEAD_WIDTH) for g in heads]
    for g in heads:
        for c in range(seq // tq):
            vt_ref[g, c, 0:HEAD_WIDTH, :] = v_ref[c * tq:(c + 1) * tq, cols[g]].astype(F32).T.astype(BF16)
            vt_ref[g, c, HEAD_WIDTH:, :] = jnp.ones((ATTN_SUM_ROWS, tq), BF16)

    lane = lax.broadcasted_iota(jnp.int32, (tq, HEAD_WIDTH), 1)
    first_map = lane < ATTN_HEAD_DIM
    key = lax.broadcasted_iota(jnp.int32, (tq, 2 * tq), 0)
    qry = lax.broadcasted_iota(jnp.int32, (tq, 2 * tq), 1)
    q_chunk = jnp.where(qry >= tq, qry - tq, qry) // CHUNK
    visible = (key // CHUNK) <= q_chunk

    def q_block(i, _):
        q0 = pl.multiple_of(i * tq, tq)
        qqs = []
        for g in heads:
            q = q_ref[pl.ds(q0, tq), cols[g]]
            zero = jnp.zeros_like(q)
            qqs.append(jnp.concatenate([jnp.where(first_map, q, zero), jnp.where(first_map, zero, q)], axis=0))

        def step(j, carry, masked):
            k0 = pl.multiple_of(j * tq, tq)
            out = []
            scores = [lax.dot_general(k_ref[pl.ds(k0, tq), cols[g]], qqs[g], (((1,), (1,)), ((), ())),
                                      preferred_element_type=F32) for g in heads]
            for g in heads:
                m, acc = carry[g]
                s = scores[g]
                if masked:
                    s = jnp.where(visible, s, NEG_BIG)
                m_new = jnp.maximum(m, jnp.max(s, axis=0, keepdims=True))
                p = jnp.exp2(s - m_new)
                scale = jnp.exp2(m - m_new)
                pv = jnp.dot(vt_ref[g, j], p.astype(BF16), preferred_element_type=F32)
                out.append((m_new, scale * acc + pv))
            return tuple(out)

        init = tuple((jnp.full((1, 2 * tq), NEG_BIG, F32),
                      jnp.zeros((HEAD_WIDTH + ATTN_SUM_ROWS, 2 * tq), F32)) for _ in heads)
        carry = lax.fori_loop(0, i, lambda j, c: step(j, c, False), init)
        carry = step(i, carry, True)
        for g in heads:
            _, acc = carry[g]
            l = acc[HEAD_WIDTH:HEAD_WIDTH + 1, :]
            acc = acc[:HEAD_WIDTH, :]
            o = acc[:, :tq] / l[:, :tq] - lam * (acc[:, tq:] / l[:, tq:])
            o = o * lax.rsqrt(jnp.mean(o * o, axis=0, keepdims=True) + HEAD_NORM_EPS)
            o_ref[pl.ds(q0, tq), cols[g]] = (o.T * gain).astype(BF16)
        return 0

    lax.fori_loop(0, seq // tq, q_block, 0)


def _attention(q, k, v, lam_params, subln_g, lambda_init, batch, seq):
    t = q.shape[0]
    blk = pl.BlockSpec((seq, ATTN_HEADS_PER_STEP * HEAD_WIDTH), lambda b, h: (b, h))
    return pl.pallas_call(
        functools.partial(_attn_body, lambda_init=lambda_init, seq=seq),
        grid=(batch, ATTN_HEADS // ATTN_HEADS_PER_STEP),
        in_specs=[pl.BlockSpec((4, ATTN_HEAD_DIM), lambda b, h: (0, 0)),
                  pl.BlockSpec((1, HEAD_WIDTH), lambda b, h: (0, 0)),
                  blk, blk, blk],
        out_specs=blk,
        out_shape=jax.ShapeDtypeStruct((t, ATTN_WIDTH), BF16),
        scratch_shapes=[pltpu.VMEM((ATTN_HEADS_PER_STEP, seq // ATTN_TILE, HEAD_WIDTH + ATTN_SUM_ROWS, ATTN_TILE),
                                   BF16)],
        compiler_params=_params("arbitrary", "arbitrary"),
        name="diff_attention",
    )(lam_params, subln_g.reshape(1, HEAD_WIDTH), q, k, v)


def _mixer_body(r_ref, pw_ref, ps_ref, cw_ref, cb_ref, wa_ref, ba_ref, wi_ref, bi_ref, lam_ref, y_ref,
                *, seq):
    rows = MIX_ROWS
    pool_hist = max(POOL_WINDOWS)
    conv_hist = SUBLANES
    pw = pw_ref[...]
    wa = wa_ref[...]
    wi = wi_ref[...]
    ps, cb, ba, bi = ps_ref[...], cb_ref[...], ba_ref[...], bi_ref[...]
    cw = cw_ref[...]
    neg_c_softplus = -LRU_C * jax.nn.softplus(-lam_ref[...])

    lane_e = lax.broadcasted_iota(jnp.int32, (rows + pool_hist, POOL_WIDTH), 1)
    lane = lax.broadcasted_iota(jnp.int32, (rows, POOL_WIDTH), 1)
    row = lax.broadcasted_iota(jnp.int32, (rows, POOL_WIDTH), 0)
    win = jnp.where(lane < 64, 2, jnp.where(lane < 128, 4, jnp.where(lane < 192, 8, 16)))
    inv_win = 1.0 / win.astype(F32)
    inv_first = 1.0 / (row + 1).astype(F32)

    def chunk(c, carry):
        tail_u, tail_x, h_prev = carry
        r0 = pl.multiple_of(c * rows, rows)
        u = r_ref[pl.ds(r0, rows), 0:POOL_WIDTH]
        xr = r_ref[pl.ds(r0, rows), POOL_WIDTH:POOL_WIDTH + LRU_WIDTH]
        xg = r_ref[pl.ds(r0, rows), POOL_WIDTH + LRU_WIDTH:REST_WIDTH]

        ue = jnp.concatenate([tail_u, u], axis=0)
        w2 = ue + pltpu.roll(ue, 1, 0)
        w4 = w2 + pltpu.roll(w2, 2, 0)
        w8 = w4 + pltpu.roll(w4, 4, 0)
        w16 = w8 + pltpu.roll(w8, 8, 0)
        ws = jnp.where(lane_e < 64, w2, jnp.where(lane_e < 128, w4, jnp.where(lane_e < 192, w8, w16)))
        ws = ws[pool_hist:]
        inv_count = jnp.where(r0 + row + 1 >= win, inv_win, inv_first)
        pooled = ws * inv_count - u
        y_pool = jnp.dot(pooled.astype(BF16), pw, preferred_element_type=F32) * ps

        xe = jnp.concatenate([tail_x, xr], axis=0)
        xc = cb + pltpu.roll(xe, 3, 0)[conv_hist:] * cw[0:1]
        xc = xc + pltpu.roll(xe, 2, 0)[conv_hist:] * cw[1:2]
        xc = xc + pltpu.roll(xe, 1, 0)[conv_hist:] * cw[2:3]
        xc = xc + xr * cw[3:4]
        xcb = xc.astype(BF16)
        r_gate = jax.nn.sigmoid(jnp.dot(xcb, wa, preferred_element_type=F32) + ba)
        i_gate = jax.nn.sigmoid(jnp.dot(xcb, wi, preferred_element_type=F32) + bi)
        log_a = r_gate * neg_c_softplus
        a = jnp.exp(log_a)
        gap = -jnp.tanh(log_a) * (a * a + 1.0)
        b = (gap * lax.rsqrt(jnp.maximum(gap, F32_TINY))) * (i_gate * xc)

        s = 1
        while s < rows:
            keep = row >= s
            a_prev = jnp.where(keep, pltpu.roll(a, s, 0), 1.0)
            b_prev = jnp.where(keep, pltpu.roll(b, s, 0), 0.0)
            b = a * b_prev + b
            a = a * a_prev
            s *= 2
        h = a * h_prev + b
        y_lru = h * jax.nn.gelu(xg)

        y_ref[pl.ds(r0, rows), 0:POOL_WIDTH] = y_pool.astype(BF16)
        y_ref[pl.ds(r0, rows), POOL_WIDTH:MIX_WIDTH] = y_lru.astype(BF16)
        return u[rows - pool_hist:], xr[rows - conv_hist:], h[rows - 1:rows]

    init = (jnp.zeros((pool_hist, POOL_WIDTH), F32), jnp.zeros((conv_hist, LRU_WIDTH), F32),
            jnp.zeros((1, LRU_WIDTH), F32))
    lax.fori_loop(0, seq // rows, chunk, init)


def _block_diag(w):
    g, c, d = w.shape
    eye = jnp.eye(g, dtype=w.dtype)
    return (eye[:, None, :, None] * w[:, :, None, :]).reshape(g * c, g * d)


def _mixer(rest, pool_w, pool_scale, conv_w, conv_b, wa, ba, wi, bi, lru_lambda, batch, seq):
    t = rest.shape[0]
    full = lambda shape: pl.BlockSpec(shape, lambda b: (0, 0))
    vec = lambda a: a.reshape(1, -1)
    return pl.pallas_call(
        functools.partial(_mixer_body, seq=seq),
        grid=(batch,),
        in_specs=[pl.BlockSpec((seq, REST_WIDTH), lambda b: (b, 0)),
                  full((POOL_WIDTH, POOL_WIDTH)), full((1, POOL_WIDTH)),
                  full((CONV_WIDTH, LRU_WIDTH)), full((1, LRU_WIDTH)),
                  full((LRU_WIDTH, LRU_WIDTH)), full((1, LRU_WIDTH)),
                  full((LRU_WIDTH, LRU_WIDTH)), full((1, LRU_WIDTH)),
                  full((1, LRU_WIDTH))],
        out_specs=pl.BlockSpec((seq, MIX_WIDTH), lambda b: (b, 0)),
        out_shape=jax.ShapeDtypeStruct((t, MIX_WIDTH), BF16),
        compiler_params=_params("arbitrary"),
        name="pool_lru_mixer",
    )(rest, _block_diag(pool_w).astype(BF16), vec(pool_scale), conv_w, vec(conv_b),
      _block_diag(wa).astype(BF16), vec(ba), _block_diag(wi).astype(BF16), vec(bi), vec(lru_lambda))


def _outproj_body(ya_ref, ym_ref, x_ref, w_ref, g_ref, b_ref, *rest, alpha, route):
    if route:
        wr_ref, o_ref, ot_ref, meta_ref, gate_ref, cnt_ref, wbf_ref, run_ref = rest
    else:
        o_ref, wbf_ref = rest

    @pl.when(pl.program_id(0) == 0)
    def _cast_weights():
        for c in range(0, D_MODEL, 256):
            wbf_ref[:, c:c + 256] = w_ref[0, :, c:c + 256].astype(BF16)

    mix = jnp.dot(ya_ref[...], wbf_ref[0:ATTN_WIDTH, :], preferred_element_type=F32)
    mix = mix + jnp.dot(ym_ref[...], wbf_ref[ATTN_WIDTH:, :], preferred_element_type=F32)
    x1 = _layer_norm(alpha * x_ref[...] + mix, g_ref[...], b_ref[...])
    o_ref[...] = x1
    if route:
        _store_row_tiles(ot_ref, x1)
        _route(x1, wr_ref, meta_ref, gate_ref, cnt_ref, run_ref)


def _outproj(y_attn, y_mix, x, w, layer, g, b, alpha, w_router=None):
    t = x.shape[0]
    route = w_router is not None
    row = lambda i: (i, 0)
    const = lambda i: (0, 0)
    in_specs = [pl.BlockSpec((ROW_TILE, ATTN_WIDTH), row), pl.BlockSpec((ROW_TILE, MIX_WIDTH), row),
                pl.BlockSpec((ROW_TILE, D_MODEL), row),
                pl.BlockSpec((1, D_MODEL, D_MODEL), lambda i: (layer, 0, 0)),
                pl.BlockSpec((1, D_MODEL), const), pl.BlockSpec((1, D_MODEL), const)]
    args = [y_attn, y_mix, x, w, g.reshape(1, -1), b.reshape(1, -1)]
    out_specs = [pl.BlockSpec((ROW_TILE, D_MODEL), row)]
    out_shape = [jax.ShapeDtypeStruct((t, D_MODEL), F32)]
    scratch = [pltpu.VMEM((D_MODEL, D_MODEL), BF16)]
    if route:
        in_specs.append(pl.BlockSpec((N_EXPERTS, D_MODEL), const))
        args.append(w_router.T)
        out_specs += [pl.BlockSpec((ROW_TILE * SUBLANES, LANES), row),
                      pl.BlockSpec((ROW_TILE, ROUTE_META), row), pl.BlockSpec((ROW_TILE, LANES), row),
                      pl.BlockSpec((SUBLANES, LANES), const)]
        out_shape += [jax.ShapeDtypeStruct((t * SUBLANES, LANES), F32),
                      jax.ShapeDtypeStruct((t, ROUTE_META), jnp.int32), jax.ShapeDtypeStruct((t, LANES), F32),
                      jax.ShapeDtypeStruct((SUBLANES, LANES), F32)]
        scratch.append(pltpu.VMEM((1, LANES), F32))
    return pl.pallas_call(
        functools.partial(_outproj_body, alpha=alpha, route=route),
        grid=(t // ROW_TILE,),
        in_specs=in_specs,
        out_specs=out_specs,
        out_shape=out_shape,
        scratch_shapes=scratch,
        compiler_params=_params("arbitrary"),
        name="outproj_ln_route" if route else "outproj_ln",
    )(*args)


def _store_row_tiles(o_ref, val):
    rows = val.shape[0]
    for s in range(SUBLANES):
        o_ref[pl.ds(s, rows, stride=SUBLANES), :] = val[:, s * LANES:(s + 1) * LANES]


def _load_row_tiles(ref, first_row, rows):
    return [ref[pl.ds(first_row * SUBLANES + s, rows, stride=SUBLANES), :] for s in range(SUBLANES)]


def _tile_window(row):
    return pl.ds(pl.multiple_of(row * SUBLANES, SUBLANES), SUBLANES)


def _start_row_copies(n_rows, make_copy, priorities):
    k = len(priorities)

    def body(q, _):
        for p, priority in enumerate(priorities):
            make_copy(q * k + p).start(priority=priority)
        return 0
    lax.fori_loop(0, n_rows // k, body, 0, unroll=DMA_ISSUE_UNROLL // k)


def _gather_copies(src_ref, x_hbm, buf_ref, sem, slot):
    def copy(r):
        return pltpu.make_async_copy(x_hbm.at[_tile_window(src_ref[0, 0, r])], buf_ref.at[slot, _tile_window(r)],
                                     sem.at[slot])
    return copy


def _ffn_body(be_ref, nu_ref, *rest, alpha, fuse_ln, gather):
    if gather:
        src0_ref, src_next_ref, x_hbm, wg_ref, wu_ref, wd_ref, o_ref, xb_ref, acc_ref, buf_ref, sem = rest
    elif fuse_ln:
        x_ref, wg_ref, wu_ref, wd_ref, g_ref, b_ref, o_ref, xb_ref, acc_ref = rest
    else:
        x_ref, wg_ref, wu_ref, wd_ref, o_ref, xb_ref, acc_ref = rest
    i = pl.program_id(0)
    j = pl.program_id(1)
    n_used = nu_ref[0]
    used = i < n_used

    if gather:
        slot = lax.rem(i, 2)
        priority = 1

        def wait_slot(which):
            pltpu.make_async_copy(buf_ref.at[which], buf_ref.at[which], sem.at[which]).wait()

        @pl.when(jnp.logical_and(used, jnp.logical_and(i == 0, j == 0)))
        def _first_rows():
            _start_row_copies(FFN_GATHER_ROWS, _gather_copies(src0_ref, x_hbm, buf_ref, sem, 0),
                              priorities=(priority,))

        @pl.when(jnp.logical_and(used, j == 0))
        def _start():
            wait_slot(slot)
            for s, part in enumerate(_load_row_tiles(buf_ref.at[slot], 0, FFN_ROWS)):
                xb_ref[:, s * LANES:(s + 1) * LANES] = part.astype(BF16)
            acc_ref[...] = jnp.zeros_like(acc_ref)
    else:
        @pl.when(jnp.logical_and(used, j == 0))
        def _start():
            xb_ref[...] = x_ref[...].astype(BF16)
            acc_ref[...] = jnp.zeros_like(acc_ref)

    @pl.when(used)
    def _accumulate():
        if gather:
            copy = _gather_copies(src_next_ref, x_hbm, buf_ref, sem, 1 - slot)
            for r in range(FFN_GATHER_ROWS_PER_STEP):
                copy(j * FFN_GATHER_ROWS_PER_STEP + r).start(priority=priority)
        xb = xb_ref[...]
        gate = jnp.dot(xb, wg_ref[0].astype(BF16), preferred_element_type=F32)
        up = jnp.dot(xb, wu_ref[0].astype(BF16), preferred_element_type=F32)
        hidden = (jax.nn.silu(gate) * up).astype(BF16)
        acc_ref[...] += jnp.dot(hidden, wd_ref[0].astype(BF16), preferred_element_type=F32)

    @pl.when(jnp.logical_and(used, j == pl.num_programs(1) - 1))
    def _finish():
        if gather:
            @pl.when(i == n_used - 1)
            def _drain():
                wait_slot(1 - slot)
        if fuse_ln:
            o_ref[...] = _layer_norm(alpha * x_ref[...] + acc_ref[...], g_ref[...], b_ref[...])
        elif gather:
            _store_row_tiles(o_ref, acc_ref[...])
        else:
            o_ref[...] = acc_ref[...]

    @pl.when(jnp.logical_and(jnp.logical_not(used), j == 0))
    def _empty():
        o_ref[...] = jnp.zeros_like(o_ref)


def _ffn(block_e, n_used, x, w_gate, w_up, w_down, ln=None, alpha=1.0, src=None):
    gather = src is not None
    n_blocks = block_e.shape[0]
    nj = D_FF // FFN_COLS

    def col(i, j, be, nu):
        return jnp.where(i < nu[0], j, nj - 1)

    def blk(i, be, nu):
        return jnp.minimum(i, nu[0] - 1)

    row = lambda i, j, be, nu: (i, 0)
    w_specs = [pl.BlockSpec((1, D_MODEL, FFN_COLS), lambda i, j, be, nu: (be[blk(i, be, nu)], 0, col(i, j, be, nu))),
               pl.BlockSpec((1, D_MODEL, FFN_COLS), lambda i, j, be, nu: (be[blk(i, be, nu)], 0, col(i, j, be, nu))),
               pl.BlockSpec((1, FFN_COLS, D_MODEL), lambda i, j, be, nu: (be[blk(i, be, nu)], col(i, j, be, nu), 0))]
    scratch = [pltpu.VMEM((FFN_ROWS, D_MODEL), BF16), pltpu.VMEM((FFN_ROWS, D_MODEL), F32)]
    if gather:
        assert ln is None
        src3 = jnp.pad(src.reshape(n_blocks, 1, FFN_ROWS), ((0, 0), (0, 0), (0, FFN_GATHER_ROWS - FFN_ROWS)))
        in_specs = [pl.BlockSpec((1, 1, FFN_GATHER_ROWS), lambda i, j, be, nu: (0, 0, 0), memory_space=pltpu.SMEM),
                    pl.BlockSpec((1, 1, FFN_GATHER_ROWS),
                                 lambda i, j, be, nu: (jnp.minimum(i + 1, n_blocks - 1), 0, 0),
                                 memory_space=pltpu.SMEM),
                    pl.BlockSpec(memory_space=pl.ANY)] + w_specs
        args = [src3, src3, x, w_gate, w_up, w_down]
        scratch += [pltpu.VMEM((2, FFN_GATHER_ROWS * SUBLANES, LANES), F32), pltpu.SemaphoreType.DMA((2,))]
        out_spec = pl.BlockSpec((FFN_ROWS * SUBLANES, LANES), row)
        out_shape = jax.ShapeDtypeStruct((n_blocks * FFN_ROWS * SUBLANES, LANES), F32)
    else:
        out_spec = pl.BlockSpec((FFN_ROWS, D_MODEL), row)
        out_shape = jax.ShapeDtypeStruct((n_blocks * FFN_ROWS, D_MODEL), F32)
        in_specs = [pl.BlockSpec((FFN_ROWS, D_MODEL), row)] + w_specs
        args = [x, w_gate, w_up, w_down]
        if ln is not None:
            in_specs += [pl.BlockSpec((1, D_MODEL), lambda i, j, be, nu: (0, 0))] * 2
            args += [ln[0].reshape(1, -1), ln[1].reshape(1, -1)]
    return pl.pallas_call(
        functools.partial(_ffn_body, alpha=alpha, fuse_ln=ln is not None, gather=gather),
        grid_spec=pltpu.PrefetchScalarGridSpec(
            num_scalar_prefetch=2,
            grid=(n_blocks, nj),
            in_specs=in_specs,
            out_specs=out_spec,
            scratch_shapes=scratch),
        out_shape=out_shape,
        compiler_params=_params("arbitrary", "arbitrary"),
        name="swiglu_ln" if ln is not None else "swiglu_experts",
    )(block_e, n_used, *args)


def _route(x, w_ref, meta_ref, gate_ref, cnt_ref, run_ref):
    tm = x.shape[0]

    @pl.when(pl.program_id(0) == 0)
    def _init():
        run_ref[...] = jnp.zeros_like(run_ref)

    lane_i = lax.broadcasted_iota(jnp.int32, (tm, LANES), 1)
    lane = lane_i.astype(F32)
    logits = jnp.full((tm, LANES), -jnp.inf, F32)
    for e in range(N_EXPERTS):
        logit_e = jnp.sum(x * w_ref[e:e + 1, :], axis=-1, keepdims=True)
        logits = jnp.where(lane_i == e, logit_e, logits)
    m1 = jnp.max(logits, axis=-1, keepdims=True)
    e1 = jnp.min(jnp.where(logits == m1, lane, float(LANES)), axis=-1, keepdims=True)
    rest = jnp.where(lane == e1, -jnp.inf, logits)
    m2 = jnp.max(rest, axis=-1, keepdims=True)
    e2 = jnp.min(jnp.where(rest == m2, lane, float(LANES)), axis=-1, keepdims=True)
    ex = jnp.exp(m2 - m1)
    g1 = 1.0 / (1.0 + ex)
    g2 = ex / (1.0 + ex)

    chosen = jnp.logical_or(lane == e1, lane == e2)
    r_i = lax.broadcasted_iota(jnp.int32, (tm, tm), 0)
    c_i = lax.broadcasted_iota(jnp.int32, (tm, tm), 1)
    earlier = (c_i < r_i).astype(BF16)
    before = jnp.dot(earlier, chosen.astype(BF16), preferred_element_type=F32) + run_ref[...]
    rank1 = jnp.sum(jnp.where(lane == e1, before, 0.0), axis=-1, keepdims=True)
    rank2 = jnp.sum(jnp.where(lane == e2, before, 0.0), axis=-1, keepdims=True)
    total = run_ref[...] + jnp.sum(chosen.astype(F32), axis=0, keepdims=True)
    run_ref[...] = total
    cnt_ref[...] = jnp.broadcast_to(total, cnt_ref.shape)
    meta = jnp.where(lane_i == 0, e1, jnp.where(lane_i == 1, e2, jnp.where(lane_i == 2, rank1, rank2)))
    meta_ref[...] = meta[:, :ROUTE_META].astype(jnp.int32)
    gate_ref[...] = jnp.where(lane_i == 0, g1, g2)


def _combine_body(d0_ref, dn_ref, ys_hbm, gate_ref, x_ref, g_ref, b_ref, o_ref, buf_ref, sem, *, alpha):
    i = pl.program_id(0)
    slot = lax.rem(i, 2)
    n_rows = buf_ref.shape[1] // SUBLANES

    def row_copy(d_ref, to_slot):
        def copy(r):
            return pltpu.make_async_copy(ys_hbm.at[_tile_window(d_ref[0, 0, r])],
                                         buf_ref.at[to_slot, _tile_window(r)], sem.at[to_slot])
        return copy

    def wait_slot(which):
        pltpu.make_async_copy(buf_ref.at[which], buf_ref.at[which], sem.at[which]).wait()

    @pl.when(i == 0)
    def _first_rows():
        _start_row_copies(n_rows, row_copy(d0_ref, 0), priorities=(0, 1))

    wait_slot(slot)
    next_copy = row_copy(dn_ref, 1 - slot)
    for r in range(n_rows):
        next_copy(r).start(priority=r % 2)

    gates = gate_ref[...]
    g1, g2 = gates[:, 0:1], gates[:, 1:2]
    first = _load_row_tiles(buf_ref.at[slot], 0, MOVE_ROWS)
    second = _load_row_tiles(buf_ref.at[slot], MOVE_ROWS, MOVE_ROWS)
    f = jnp.concatenate([g1 * a + g2 * b for a, b in zip(first, second)], axis=-1)
    o_ref[...] = _layer_norm(alpha * x_ref[...] + f, g_ref[...], b_ref[...])

    @pl.when(i == pl.num_programs(0) - 1)
    def _drain():
        wait_slot(1 - slot)


def _combine(ys, dest1, dest2, gates, x, g, b, alpha):
    t = x.shape[0]
    steps = t // MOVE_ROWS
    dest = jnp.concatenate([dest1.reshape(steps, 1, MOVE_ROWS), dest2.reshape(steps, 1, MOVE_ROWS)], axis=-1)
    idx_shape = (1, 1, TOP_K * MOVE_ROWS)
    row = lambda i: (i, 0)
    const = lambda i: (0, 0)
    return pl.pallas_call(
        functools.partial(_combine_body, alpha=alpha),
        grid=(steps,),
        in_specs=[pl.BlockSpec(idx_shape, lambda i: (0, 0, 0), memory_space=pltpu.SMEM),
                  pl.BlockSpec(idx_shape, lambda i: (jnp.minimum(i + 1, steps - 1), 0, 0), memory_space=pltpu.SMEM),
                  pl.BlockSpec(memory_space=pl.ANY),
                  pl.BlockSpec((MOVE_ROWS, LANES), row), pl.BlockSpec((MOVE_ROWS, D_MODEL), row),
                  pl.BlockSpec((1, D_MODEL), const), pl.BlockSpec((1, D_MODEL), const)],
        out_specs=pl.BlockSpec((MOVE_ROWS, D_MODEL), row),
        out_shape=jax.ShapeDtypeStruct((t, D_MODEL), F32),
        scratch_shapes=[pltpu.VMEM((2, TOP_K * MOVE_ROWS * SUBLANES, LANES), F32), pltpu.SemaphoreType.DMA((2,))],
        compiler_params=_params("arbitrary"),
        name="combine_ln",
    )(dest, dest, ys, gates, x, g.reshape(1, -1), b.reshape(1, -1))


def _invert_body(d1_ref, d2_ref, src_ref):
    def clear(r, _):
        src_ref[r] = 0
        return 0
    lax.fori_loop(0, src_ref.shape[0], clear, 0, unroll=4 * SCALAR_LOOP_UNROLL)

    def place(t, _):
        src_ref[d1_ref[t]] = t
        src_ref[d2_ref[t]] = t
        return 0
    lax.fori_loop(0, d1_ref.shape[0], place, 0, unroll=SCALAR_LOOP_UNROLL)


def _invert_placement(dest1, dest2, n_rows):
    smem = pl.BlockSpec(memory_space=pltpu.SMEM)
    return pl.pallas_call(
        _invert_body,
        in_specs=[smem, smem],
        out_specs=smem,
        out_shape=jax.ShapeDtypeStruct((n_rows,), jnp.int32),
        name="invert_placement",
    )(dest1, dest2)


def _moe(x1, x1_tiles, meta, gates, counts, w_gate, w_up, w_down, first_expert, g, b, alpha):
    t = x1.shape[0]
    e1, e2, rank1, rank2 = meta[:, 0], meta[:, 1], meta[:, 2], meta[:, 3]
    sizes = counts[0, :N_EXPERTS].astype(jnp.int32)
    padded = (sizes + FFN_ROWS - 1) // FFN_ROWS * FFN_ROWS
    group_end = jnp.cumsum(padded)
    group_start = group_end - padded
    dest1 = group_start[e1] + rank1
    dest2 = group_start[e2] + rank2
    n_blocks = (t * TOP_K) // FFN_ROWS + N_EXPERTS
    block_start = jnp.arange(n_blocks, dtype=jnp.int32) * FFN_ROWS
    block_e = jnp.minimum(jnp.sum(group_end[None, :] <= block_start[:, None], axis=1), N_EXPERTS - 1)
    n_used = (group_end[-1] // FFN_ROWS).reshape(1)
    src = _invert_placement(dest1, dest2, n_blocks * FFN_ROWS)
    ys = _ffn((block_e + first_expert).astype(jnp.int32), n_used.astype(jnp.int32), x1_tiles, w_gate, w_up,
              w_down, src=src)
    return _combine(ys, dest1, dest2, gates, x1, g, b, alpha)


def kernel(x, w_in, w_out, attn_lambda, attn_subln_g, pool_w, pool_scale, conv_w, conv_b, lru_wa, lru_ba,
           lru_wi, lru_bi, lru_lambda, ln1_g, ln1_b, ln2_g, ln2_b, ffn_w_gate, ffn_w_up, ffn_w_down,
           router_w, moe_w_gate, moe_w_up, moe_w_down):
    batch, seq, d = x.shape
    depth = w_in.shape[0]
    assert d == D_MODEL and seq % ATTN_TILE == 0 and seq % MIX_ROWS == 0
    t = batch * seq
    assert t % FFN_ROWS == 0 and t % ROW_TILE == 0
    alpha = (2.0 * depth) ** 0.25
    dense_used = jnp.full((1,), t // FFN_ROWS, jnp.int32)
    moe_gate = moe_w_gate.reshape(-1, D_MODEL, D_FF)
    moe_up = moe_w_up.reshape(-1, D_MODEL, D_FF)
    moe_down = moe_w_down.reshape(-1, D_FF, D_MODEL)
    xt = x.reshape(t, d)
    for l in range(depth):
        lambda_init = 0.8 - 0.6 * math.exp(-0.3 * l)
        q, k, v, rest = _inproj(xt, w_in, l)
        y_attn = _attention(q, k, v, attn_lambda[l], attn_subln_g[l], lambda_init, batch, seq)
        y_mix = _mixer(rest, pool_w[l], pool_scale[l], conv_w[l], conv_b[l], lru_wa[l], lru_ba[l],
                       lru_wi[l], lru_bi[l], lru_lambda[l], batch, seq)
        if l % 2 == 0:
            x1, = _outproj(y_attn, y_mix, xt, w_out, l, ln1_g[l], ln1_b[l], alpha)
            dense_blocks = jnp.full((t // FFN_ROWS,), l // 2, jnp.int32)
            xt = _ffn(dense_blocks, dense_used, x1, ffn_w_gate, ffn_w_up, ffn_w_down,
                      ln=(ln2_g[l], ln2_b[l]), alpha=alpha)
        else:
            x1, x1_tiles, meta, gates, counts = _outproj(y_attn, y_mix, xt, w_out, l, ln1_g[l], ln1_b[l], alpha,
                                                         w_router=router_w[l // 2])
            xt = _moe(x1, x1_tiles, meta, gates, counts, moe_gate, moe_up, moe_down, (l // 2) * N_EXPERTS,
                      ln2_g[l], ln2_b[l], alpha)
    return xt.reshape(batch, seq, d)
```

```python
import functools
import math

import jax
import jax.numpy as jnp
from jax import lax
from jax.experimental import pallas as pl
from jax.experimental.pallas import tpu as pltpu

F32 = jnp.float32
BF16 = jnp.bfloat16

D_MODEL = 1024
CHUNK = 64
ATTN_HEADS = 4
ATTN_WIDTH = 512
ATTN_HEAD_DIM = 64
HEAD_WIDTH = 2 * ATTN_HEAD_DIM
POOL_WINDOWS = (2, 4, 8, 16)
POOL_WIDTH = 256
POOL_GROUP_DIM = 64
LRU_WIDTH = 256
LRU_C = 8.0
CONV_WIDTH = 4
REST_WIDTH = POOL_WIDTH + 2 * LRU_WIDTH
IN_WIDTH = 3 * ATTN_WIDTH + REST_WIDTH
MIX_WIDTH = POOL_WIDTH + LRU_WIDTH
D_FF = 2816
N_EXPERTS = 8
TOP_K = 2
LN_EPS = 1e-5
HEAD_NORM_EPS = 1e-5

LANES = 128
SUBLANES = 8
VMEM_LIMIT_BYTES = 56 * 1024 * 1024

ROW_TILE = 512
ATTN_TILE = 256
ATTN_HEADS_PER_STEP = 4
ATTN_SUM_ROWS = 16
ATTN_SCORE_SCALE = ATTN_HEAD_DIM ** -0.5 * math.log2(math.e)
MIX_ROWS = 256
FFN_ROWS = 1024
FFN_COLS = 256
FFN_STEPS = D_FF // FFN_COLS
FFN_GATHER_ROWS_PER_STEP = -(-FFN_ROWS // ((FFN_STEPS - 1) * SUBLANES)) * SUBLANES
FFN_GATHER_ROWS = FFN_GATHER_ROWS_PER_STEP * FFN_STEPS
MOVE_ROWS = 256
ROUTE_META = 8
SCALAR_LOOP_UNROLL = 8
DMA_ISSUE_UNROLL = 8
NEG_BIG = -1e30
F32_TINY = float(jnp.finfo(jnp.float32).tiny)
assert MIX_ROWS >= max(POOL_WINDOWS)


def _params(*semantics):
    return pltpu.CompilerParams(dimension_semantics=semantics, vmem_limit_bytes=VMEM_LIMIT_BYTES)


def _layer_norm(z, g, b):
    mu = jnp.mean(z, axis=-1, keepdims=True)
    zc = z - mu
    var = jnp.mean(zc * zc, axis=-1, keepdims=True)
    return zc * lax.rsqrt(var + LN_EPS) * g + b


def _inproj_body(x_ref, w_ref, q_ref, k_ref, v_ref, r_ref, wbf_ref):
    @pl.when(pl.program_id(0) == 0)
    def _cast_weights():
        for c in range(0, IN_WIDTH, 256):
            wbf_ref[:, c:c + 256] = w_ref[0, :, c:c + 256].astype(BF16)

    xb = x_ref[...].astype(BF16)

    def proj(c0, c1):
        return jnp.dot(xb, wbf_ref[:, c0:c1], preferred_element_type=F32)

    q_ref[...] = (proj(0, ATTN_WIDTH) * ATTN_SCORE_SCALE).astype(BF16)
    k_ref[...] = proj(ATTN_WIDTH, 2 * ATTN_WIDTH).astype(BF16)
    v_ref[...] = proj(2 * ATTN_WIDTH, 3 * ATTN_WIDTH).astype(BF16)
    r_ref[...] = proj(3 * ATTN_WIDTH, IN_WIDTH)


def _inproj(x, w, layer):
    t = x.shape[0]
    row = lambda i: (i, 0)
    return pl.pallas_call(
        _inproj_body,
        grid=(t // ROW_TILE,),
        in_specs=[pl.BlockSpec((ROW_TILE, D_MODEL), row),
                  pl.BlockSpec((1, D_MODEL, IN_WIDTH), lambda i: (layer, 0, 0))],
        out_specs=[pl.BlockSpec((ROW_TILE, ATTN_WIDTH), row)] * 3
        + [pl.BlockSpec((ROW_TILE, REST_WIDTH), row)],
        out_shape=[jax.ShapeDtypeStruct((t, ATTN_WIDTH), BF16)] * 3
        + [jax.ShapeDtypeStruct((t, REST_WIDTH), F32)],
        scratch_shapes=[pltpu.VMEM((D_MODEL, IN_WIDTH), BF16)],
        compiler_params=_params("arbitrary"),
        name="inproj",
    )(x, w)


def _attn_body(lam_ref, g_ref, q_ref, k_ref, v_ref, o_ref, vt_ref, *, lambda_init, seq):
    tq = ATTN_TILE
    lv = lam_ref[...]
    lam = (jnp.exp(jnp.sum(lv[0:1] * lv[1:2], axis=-1, keepdims=True))
           - jnp.exp(jnp.sum(lv[2:3] * lv[3:4], axis=-1, keepdims=True)) + lambda_init)
    gain = g_ref[...] * (1.0 - lambda_init)

    heads = range(ATTN_HEADS_PER_STEP)
    cols = [slice(g * HEAD_WIDTH, (g + 1) * HEAD_WIDTH) for g in heads]
    for g in heads:
        for c in range(seq // tq):
            vt_ref[g, c, 0:HEAD_WIDTH, :] = v_ref[c * tq:(c + 1) * tq, cols[g]].astype(F32).T.astype(BF16)
            vt_ref[g, c, HEAD_WIDTH:, :] = jnp.ones((ATTN_SUM_ROWS, tq), BF16)

    lane = lax.broadcasted_iota(jnp.int32, (tq, HEAD_WIDTH), 1)
    first_map = lane < ATTN_HEAD_DIM
    key = lax.broadcasted_iota(jnp.int32, (tq, 2 * tq), 0)
    qry = lax.broadcasted_iota(jnp.int32, (tq, 2 * tq), 1)
    q_chunk = jnp.where(qry >= tq, qry - tq, qry) // CHUNK
    visible = (key // CHUNK) <= q_chunk

    def q_block(i, _):
        q0 = pl.multiple_of(i * tq, tq)
        qqs = []
        for g in heads:
            q = q_ref[pl.ds(q0, tq), cols[g]]
            zero = jnp.zeros_like(q)
            qqs.append(jnp.concatenate([jnp.where(first_map, q, zero), jnp.where(first_map, zero, q)], axis=0))

        def step(j, carry, masked):
            k0 = pl.multiple_of(j * tq, tq)
            out = []
            scores = [lax.dot_general(k_ref[pl.ds(k0, tq), cols[g]], qqs[g], (((1,), (1,)), ((), ())),
                                      preferred_element_type=F32) for g in heads]
            for g in heads:
                m, acc = carry[g]
                s = scores[g]
                if masked:
                    s = jnp.where(visible, s, NEG_BIG)
                m_new = jnp.maximum(m, jnp.max(s, axis=0, keepdims=True))
                p = jnp.exp2(s - m_new)
                scale = jnp.exp2(m - m_new)
                pv = jnp.dot(vt_ref[g, j], p.astype(BF16), preferred_element_type=F32)
                out.append((m_new, scale * acc + pv))
            return tuple(out)

        init = tuple((jnp.full((1, 2 * tq), NEG_BIG, F32),
                      jnp.zeros((HEAD_WIDTH + ATTN_SUM_ROWS, 2 * tq), F32)) for _ in heads)
        carry = lax.fori_loop(0, i, lambda j, c: step(j, c, False), init)
        carry = step(i, carry, True)
        for g in heads:
            _, acc = carry[g]
            l = acc[HEAD_WIDTH:HEAD_WIDTH + 1, :]
            acc = acc[:HEAD_WIDTH, :]
            o = acc[:, :tq] / l[:, :tq] - lam * (acc[:, tq:] / l[:, tq:])
            o = o * lax.rsqrt(jnp.mean(o * o, axis=0, keepdims=True) + HEAD_NORM_EPS)
            o_ref[pl.ds(q0, tq), cols[g]] = (o.T * gain).astype(BF16)
        return 0

    lax.fori_loop(0, seq // tq, q_block, 0)


def _attention(q, k, v, lam_params, subln_g, lambda_init, batch, seq):
    t = q.shape[0]
    blk = pl.BlockSpec((seq, ATTN_HEADS_PER_STEP * HEAD_WIDTH), lambda b, h: (b, h))
    return pl.pallas_call(
        functools.partial(_attn_body, lambda_init=lambda_init, seq=seq),
        grid=(batch, ATTN_HEADS // ATTN_HEADS_PER_STEP),
        in_specs=[pl.BlockSpec((4, ATTN_HEAD_DIM), lambda b, h: (0, 0)),
                  pl.BlockSpec((1, HEAD_WIDTH), lambda b, h: (0, 0)),
                  blk, blk, blk],
        out_specs=blk,
        out_shape=jax.ShapeDtypeStruct((t, ATTN_WIDTH), BF16),
        scratch_shapes=[pltpu.VMEM((ATTN_HEADS_PER_STEP, seq // ATTN_TILE, HEAD_WIDTH + ATTN_SUM_ROWS, ATTN_TILE),
                                   BF16)],
        compiler_params=_params("arbitrary", "arbitrary"),
        name="diff_attention",
    )(lam_params, subln_g.reshape(1, HEAD_WIDTH), q, k, v)


def _mixer_body(r_ref, pw_ref, ps_ref, cw_ref, cb_ref, wa_ref, ba_ref, wi_ref, bi_ref, lam_ref, y_ref,
                *, seq):
    rows = MIX_ROWS
    pool_hist = max(POOL_WINDOWS)
    conv_hist = SUBLANES
    pw = pw_ref[...]
    wa = wa_ref[...]
    wi = wi_ref[...]
    ps, cb, ba, bi = ps_ref[...], cb_ref[...], ba_ref[...], bi_ref[...]
    cw = cw_ref[...]
    neg_c_softplus = -LRU_C * jax.nn.softplus(-lam_ref[...])

    lane_e = lax.broadcasted_iota(jnp.int32, (rows + pool_hist, POOL_WIDTH), 1)
    lane = lax.broadcasted_iota(jnp.int32, (rows, POOL_WIDTH), 1)
    row = lax.broadcasted_iota(jnp.int32, (rows, POOL_WIDTH), 0)
    win = jnp.where(lane < 64, 2, jnp.where(lane < 128, 4, jnp.where(lane < 192, 8, 16)))
    inv_win = 1.0 / win.astype(F32)
    inv_first = 1.0 / (row + 1).astype(F32)

    def chunk(c, carry):
        tail_u, tail_x, h_prev = carry
        r0 = pl.multiple_of(c * rows, rows)
        u = r_ref[pl.ds(r0, rows), 0:POOL_WIDTH]
        xr = r_ref[pl.ds(r0, rows), POOL_WIDTH:POOL_WIDTH + LRU_WIDTH]
        xg = r_ref[pl.ds(r0, rows), POOL_WIDTH + LRU_WIDTH:REST_WIDTH]

        ue = jnp.concatenate([tail_u, u], axis=0)
        w2 = ue + pltpu.roll(ue, 1, 0)
        w4 = w2 + pltpu.roll(w2, 2, 0)
        w8 = w4 + pltpu.roll(w4, 4, 0)
        w16 = w8 + pltpu.roll(w8, 8, 0)
        ws = jnp.where(lane_e < 64, w2, jnp.where(lane_e < 128, w4, jnp.where(lane_e < 192, w8, w16)))
        ws = ws[pool_hist:]
        inv_count = jnp.where(r0 + row + 1 >= win, inv_win, inv_first)
        pooled = ws * inv_count - u
        y_pool = jnp.dot(pooled.astype(BF16), pw, preferred_element_type=F32) * ps

        xe = jnp.concatenate([tail_x, xr], axis=0)
        xc = cb + pltpu.roll(xe, 3, 0)[conv_hist:] * cw[0:1]
        xc = xc + pltpu.roll(xe, 2, 0)[conv_hist:] * cw[1:2]
        xc = xc + pltpu.roll(xe, 1, 0)[conv_hist:] * cw[2:3]
        xc = xc + xr * cw[3:4]
        xcb = xc.astype(BF16)
        r_gate = jax.nn.sigmoid(jnp.dot(xcb, wa, preferred_element_type=F32) + ba)
        i_gate = jax.nn.sigmoid(jnp.dot(xcb, wi, preferred_element_type=F32) + bi)
        log_a = r_gate * neg_c_softplus
        a = jnp.exp(log_a)
        gap = -jnp.tanh(log_a) * (a * a + 1.0)
        b = (gap * lax.rsqrt(jnp.maximum(gap, F32_TINY))) * (i_gate * xc)

        s = 1
        while s < rows:
            keep = row >= s
            a_prev = jnp.where(keep, pltpu.roll(a, s, 0), 1.0)
            b_prev = jnp.where(keep, pltpu.roll(b, s, 0), 0.0)
            b = a * b_prev + b
            a = a * a_prev
            s *= 2
        h = a * h_prev + b
        y_lru = h * jax.nn.gelu(xg)

        y_ref[pl.ds(r0, rows), 0:POOL_WIDTH] = y_pool.astype(BF16)
        y_ref[pl.ds(r0, rows), POOL_WIDTH:MIX_WIDTH] = y_lru.astype(BF16)
        return u[rows - pool_hist:], xr[rows - conv_hist:], h[rows - 1:rows]

    init = (jnp.zeros((pool_hist, POOL_WIDTH), F32), jnp.zeros((conv_hist, LRU_WIDTH), F32),
            jnp.zeros((1, LRU_WIDTH), F32))
    lax.fori_loop(0, seq // rows, chunk, init)


def _block_diag(w):
    g, c, d = w.shape
    eye = jnp.eye(g, dtype=w.dtype)
    return (eye[:, None, :, None] * w[:, :, None, :]).reshape(g * c, g * d)


def _mixer(rest, pool_w, pool_scale, conv_w, conv_b, wa, ba, wi, bi, lru_lambda, batch, seq):
    t = rest.shape[0]
    full = lambda shape: pl.BlockSpec(shape, lambda b: (0, 0))
    vec = lambda a: a.reshape(1, -1)
    return pl.pallas_call(
        functools.partial(_mixer_body, seq=seq),
        grid=(batch,),
        in_specs=[pl.BlockSpec((seq, REST_WIDTH), lambda b: (b, 0)),
                  full((POOL_WIDTH, POOL_WIDTH)), full((1, POOL_WIDTH)),
                  full((CONV_WIDTH, LRU_WIDTH)), full((1, LRU_WIDTH)),
                  full((LRU_WIDTH, LRU_WIDTH)), full((1, LRU_WIDTH)),
                  full((LRU_WIDTH, LRU_WIDTH)), full((1, LRU_WIDTH)),
                  full((1, LRU_WIDTH))],
        out_specs=pl.BlockSpec((seq, MIX_WIDTH), lambda b: (b, 0)),
        out_shape=jax.ShapeDtypeStruct((t, MIX_WIDTH), BF16),
        compiler_params=_params("arbitrary"),
        name="pool_lru_mixer",
    )(rest, _block_diag(pool_w).astype(BF16), vec(pool_scale), conv_w, vec(conv_b),
      _block_diag(wa).astype(BF16), vec(ba), _block_diag(wi).astype(BF16), vec(bi), vec(lru_lambda))


def _outproj_body(ya_ref, ym_ref, x_ref, w_ref, g_ref, b_ref, *rest, alpha, route):
    if route:
        wr_ref, o_ref, ot_ref, meta_ref, gate_ref, cnt_ref, wbf_ref, run_ref = rest
    else:
        o_ref, wbf_ref = rest

    @pl.when(pl.program_id(0) == 0)
    def _cast_weights():
        for c in range(0, D_MODEL, 256):
            wbf_ref[:, c:c + 256] = w_ref[0, :, c:c + 256].astype(BF16)

    mix = jnp.dot(ya_ref[...], wbf_ref[0:ATTN_WIDTH, :], preferred_element_type=F32)
    mix = mix + jnp.dot(ym_ref[...], wbf_ref[ATTN_WIDTH:, :], preferred_element_type=F32)
    x1 = _layer_norm(alpha * x_ref[...] + mix, g_ref[...], b_ref[...])
    o_ref[...] = x1
    if route:
        _store_row_tiles(ot_ref, x1)
        _route(x1, wr_ref, meta_ref, gate_ref, cnt_ref, run_ref)


def _outproj(y_attn, y_mix, x, w, layer, g, b, alpha, w_router=None):
    t = x.shape[0]
    route = w_router is not None
    row = lambda i: (i, 0)
    const = lambda i: (0, 0)
    in_specs = [pl.BlockSpec((ROW_TILE, ATTN_WIDTH), row), pl.BlockSpec((ROW_TILE, MIX_WIDTH), row),
                pl.BlockSpec((ROW_TILE, D_MODEL), row),
                pl.BlockSpec((1, D_MODEL, D_MODEL), lambda i: (layer, 0, 0)),
                pl.BlockSpec((1, D_MODEL), const), pl.BlockSpec((1, D_MODEL), const)]
    args = [y_attn, y_mix, x, w, g.reshape(1, -1), b.reshape(1, -1)]
    out_specs = [pl.BlockSpec((ROW_TILE, D_MODEL), row)]
    out_shape = [jax.ShapeDtypeStruct((t, D_MODEL), F32)]
    scratch = [pltpu.VMEM((D_MODEL, D_MODEL), BF16)]
    if route:
        in_specs.append(pl.BlockSpec((N_EXPERTS, D_MODEL), const))
        args.append(w_router.T)
        out_specs += [pl.BlockSpec((ROW_TILE * SUBLANES, LANES), row),
                      pl.BlockSpec((ROW_TILE, ROUTE_META), row), pl.BlockSpec((ROW_TILE, LANES), row),
                      pl.BlockSpec((SUBLANES, LANES), const)]
        out_shape += [jax.ShapeDtypeStruct((t * SUBLANES, LANES), F32),
                      jax.ShapeDtypeStruct((t, ROUTE_META), jnp.int32), jax.ShapeDtypeStruct((t, LANES), F32),
                      jax.ShapeDtypeStruct((SUBLANES, LANES), F32)]
        scratch.append(pltpu.VMEM((1, LANES), F32))
    return pl.pallas_call(
        functools.partial(_outproj_body, alpha=alpha, route=route),
        grid=(t // ROW_TILE,),
        in_specs=in_specs,
        out_specs=out_specs,
        out_shape=out_shape,
        scratch_shapes=scratch,
        compiler_params=_params("arbitrary"),
        name="outproj_ln_route" if route else "outproj_ln",
    )(*args)


def _store_row_tiles(o_ref, val):
    rows = val.shape[0]
    for s in range(SUBLANES):
        o_ref[pl.ds(s, rows, stride=SUBLANES), :] = val[:, s * LANES:(s + 1) * LANES]


def _load_row_tiles(ref, first_row, rows):
    return [ref[pl.ds(first_row * SUBLANES + s, rows, stride=SUBLANES), :] for s in range(SUBLANES)]


def _tile_window(row):
    return pl.ds(pl.multiple_of(row * SUBLANES, SUBLANES), SUBLANES)


def _start_row_copies(n_rows, make_copy, priorities):
    k = len(priorities)

    def body(q, _):
        for p, priority in enumerate(priorities):
            make_copy(q * k + p).start(priority=priority)
        return 0
    lax.fori_loop(0, n_rows // k, body, 0, unroll=DMA_ISSUE_UNROLL // k)


def _gather_copies(src_ref, x_hbm, buf_ref, sem, slot):
    def copy(r):
        return pltpu.make_async_copy(x_hbm.at[_tile_window(src_ref[0, 0, r])], buf_ref.at[slot, _tile_window(r)],
                                     sem.at[slot])
    return copy


def _ffn_body(be_ref, nu_ref, *rest, alpha, fuse_ln, gather):
    if gather:
        src0_ref, src_next_ref, x_hbm, wg_ref, wu_ref, wd_ref, o_ref, xb_ref, acc_ref, buf_ref, sem = rest
    elif fuse_ln:
        x_ref, wg_ref, wu_ref, wd_ref, g_ref, b_ref, o_ref, xb_ref, acc_ref = rest
    else:
        x_ref, wg_ref, wu_ref, wd_ref, o_ref, xb_ref, acc_ref = rest
    i = pl.program_id(0)
    j = pl.program_id(1)
    n_used = nu_ref[0]
    used = i < n_used

    if gather:
        slot = lax.rem(i, 2)
        priority = 1

        def wait_slot(which):
            pltpu.make_async_copy(buf_ref.at[which], buf_ref.at[which], sem.at[which]).wait()

        @pl.when(jnp.logical_and(used, jnp.logical_and(i == 0, j == 0)))
        def _first_rows():
            _start_row_copies(FFN_GATHER_ROWS, _gather_copies(src0_ref, x_hbm, buf_ref, sem, 0),
                              priorities=(priority,))

        @pl.when(jnp.logical_and(used, j == 0))
        def _start():
            wait_slot(slot)
            for s, part in enumerate(_load_row_tiles(buf_ref.at[slot], 0, FFN_ROWS)):
                xb_ref[:, s * LANES:(s + 1) * LANES] = part.astype(BF16)
            acc_ref[...] = jnp.zeros_like(acc_ref)
    else:
        @pl.when(jnp.logical_and(used, j == 0))
        def _start():
            xb_ref[...] = x_ref[...].astype(BF16)
            acc_ref[...] = jnp.zeros_like(acc_ref)

    @pl.when(used)
    def _accumulate():
        if gather:
            copy = _gather_copies(src_next_ref, x_hbm, buf_ref, sem, 1 - slot)
            for r in range(FFN_GATHER_ROWS_PER_STEP):
                copy(j * FFN_GATHER_ROWS_PER_STEP + r).start(priority=priority)
        xb = xb_ref[...]
        gate = jnp.dot(xb, wg_ref[0].astype(BF16), preferred_element_type=F32)
        up = jnp.dot(xb, wu_ref[0].astype(BF16), preferred_element_type=F32)
        hidden = (jax.nn.silu(gate) * up).astype(BF16)
        acc_ref[...] += jnp.dot(hidden, wd_ref[0].astype(BF16), preferred_element_type=F32)

    @pl.when(jnp.logical_and(used, j == pl.num_programs(1) - 1))
    def _finish():
        if gather:
            @pl.when(i == n_used - 1)
            def _drain():
                wait_slot(1 - slot)
        if fuse_ln:
            o_ref[...] = _layer_norm(alpha * x_ref[...] + acc_ref[...], g_ref[...], b_ref[...])
        elif gather:
            _store_row_tiles(o_ref, acc_ref[...])
        else:
            o_ref[...] = acc_ref[...]

    @pl.when(jnp.logical_and(jnp.logical_not(used), j == 0))
    def _empty():
        o_ref[...] = jnp.zeros_like(o_ref)


def _ffn(block_e, n_used, x, w_gate, w_up, w_down, ln=None, alpha=1.0, src=None):
    gather = src is not None
    n_blocks = block_e.shape[0]
    nj = D_FF // FFN_COLS

    def col(i, j, be, nu):
        return jnp.where(i < nu[0], j, nj - 1)

    def blk(i, be, nu):
        return jnp.minimum(i, nu[0] - 1)

    row = lambda i, j, be, nu: (i, 0)
    w_specs = [pl.BlockSpec((1, D_MODEL, FFN_COLS), lambda i, j, be, nu: (be[blk(i, be, nu)], 0, col(i, j, be, nu))),
               pl.BlockSpec((1, D_MODEL, FFN_COLS), lambda i, j, be, nu: (be[blk(i, be, nu)], 0, col(i, j, be, nu))),
               pl.BlockSpec((1, FFN_COLS, D_MODEL), lambda i, j, be, nu: (be[blk(i, be, nu)], col(i, j, be, nu), 0))]
    scratch = [pltpu.VMEM((FFN_ROWS, D_MODEL), BF16), pltpu.VMEM((FFN_ROWS, D_MODEL), F32)]
    if gather:
        assert ln is None
        dummy = jnp.broadcast_to(jnp.arange(FFN_GATHER_ROWS - FFN_ROWS, dtype=jnp.int32),
                                 (n_blocks, 1, FFN_GATHER_ROWS - FFN_ROWS))
        src3 = jnp.concatenate([src.reshape(n_blocks, 1, FFN_ROWS), dummy], axis=-1)
        in_specs = [pl.BlockSpec((1, 1, FFN_GATHER_ROWS), lambda i, j, be, nu: (0, 0, 0), memory_space=pltpu.SMEM),
                    pl.BlockSpec((1, 1, FFN_GATHER_ROWS),
                                 lambda i, j, be, nu: (jnp.minimum(i + 1, n_blocks - 1), 0, 0),
                                 memory_space=pltpu.SMEM),
                    pl.BlockSpec(memory_space=pl.ANY)] + w_specs
        args = [src3, src3, x, w_gate, w_up, w_down]
        scratch += [pltpu.VMEM((2, FFN_GATHER_ROWS * SUBLANES, LANES), F32), pltpu.SemaphoreType.DMA((2,))]
        out_spec = pl.BlockSpec((FFN_ROWS * SUBLANES, LANES), row)
        out_shape = jax.ShapeDtypeStruct((n_blocks * FFN_ROWS * SUBLANES, LANES), F32)
    else:
        out_spec = pl.BlockSpec((FFN_ROWS, D_MODEL), row)
        out_shape = jax.ShapeDtypeStruct((n_blocks * FFN_ROWS, D_MODEL), F32)
        in_specs = [pl.BlockSpec((FFN_ROWS, D_MODEL), row)] + w_specs
        args = [x, w_gate, w_up, w_down]
        if ln is not None:
            in_specs += [pl.BlockSpec((1, D_MODEL), lambda i, j, be, nu: (0, 0))] * 2
            args += [ln[0].reshape(1, -1), ln[1].reshape(1, -1)]
    return pl.pallas_call(
        functools.partial(_ffn_body, alpha=alpha, fuse_ln=ln is not None, gather=gather),
        grid_spec=pltpu.PrefetchScalarGridSpec(
            num_scalar_prefetch=2,
            grid=(n_blocks, nj),
            in_specs=in_specs,
            out_specs=out_spec,
            scratch_shapes=scratch),
        out_shape=out_shape,
        compiler_params=_params("arbitrary", "arbitrary"),
        name="swiglu_ln" if ln is not None else "swiglu_experts",
    )(block_e, n_used, *args)


def _route(x, w_ref, meta_ref, gate_ref, cnt_ref, run_ref):
    tm = x.shape[0]

    @pl.when(pl.program_id(0) == 0)
    def _init():
        run_ref[...] = jnp.zeros_like(run_ref)

    lane_i = lax.broadcasted_iota(jnp.int32, (tm, LANES), 1)
    lane = lane_i.astype(F32)
    logits = jnp.full((tm, LANES), -jnp.inf, F32)
    for e in range(N_EXPERTS):
        logit_e = jnp.sum(x * w_ref[e:e + 1, :], axis=-1, keepdims=True)
        logits = jnp.where(lane_i == e, logit_e, logits)
    m1 = jnp.max(logits, axis=-1, keepdims=True)
    e1 = jnp.min(jnp.where(logits == m1, lane, float(LANES)), axis=-1, keepdims=True)
    rest = jnp.where(lane == e1, -jnp.inf, logits)
    m2 = jnp.max(rest, axis=-1, keepdims=True)
    e2 = jnp.min(jnp.where(rest == m2, lane, float(LANES)), axis=-1, keepdims=True)
    ex = jnp.exp(m2 - m1)
    g1 = 1.0 / (1.0 + ex)
    g2 = ex / (1.0 + ex)

    chosen = jnp.logical_or(lane == e1, lane == e2)
    r_i = lax.broadcasted_iota(jnp.int32, (tm, tm), 0)
    c_i = lax.broadcasted_iota(jnp.int32, (tm, tm), 1)
    earlier = (c_i < r_i).astype(BF16)
    before = jnp.dot(earlier, chosen.astype(BF16), preferred_element_type=F32) + run_ref[...]
    rank1 = jnp.sum(jnp.where(lane == e1, before, 0.0), axis=-1, keepdims=True)
    rank2 = jnp.sum(jnp.where(lane == e2, before, 0.0), axis=-1, keepdims=True)
    total = run_ref[...] + jnp.sum(chosen.astype(F32), axis=0, keepdims=True)
    run_ref[...] = total
    cnt_ref[...] = jnp.broadcast_to(total, cnt_ref.shape)
    meta = jnp.where(lane_i == 0, e1, jnp.where(lane_i == 1, e2, jnp.where(lane_i == 2, rank1, rank2)))
    meta_ref[...] = meta[:, :ROUTE_META].astype(jnp.int32)
    gate_ref[...] = jnp.where(lane_i == 0, g1, g2)


def _combine_body(d0_ref, dn_ref, ys_hbm, gate_ref, x_ref, g_ref, b_ref, o_ref, buf_ref, sem, *, alpha):
    i = pl.program_id(0)
    slot = lax.rem(i, 2)
    n_rows = buf_ref.shape[1] // SUBLANES

    def row_copy(d_ref, to_slot):
        def copy(r):
            return pltpu.make_async_copy(ys_hbm.at[_tile_window(d_ref[0, 0, r])],
                                         buf_ref.at[to_slot, _tile_window(r)], sem.at[to_slot])
        return copy

    def wait_slot(which):
        pltpu.make_async_copy(buf_ref.at[which], buf_ref.at[which], sem.at[which]).wait()

    @pl.when(i == 0)
    def _first_rows():
        _start_row_copies(n_rows, row_copy(d0_ref, 0), priorities=(0, 1))

    wait_slot(slot)
    next_copy = row_copy(dn_ref, 1 - slot)
    for r in range(n_rows):
        next_copy(r).start(priority=r % 2)

    gates = gate_ref[...]
    g1, g2 = gates[:, 0:1], gates[:, 1:2]
    first = _load_row_tiles(buf_ref.at[slot], 0, MOVE_ROWS)
    second = _load_row_tiles(buf_ref.at[slot], MOVE_ROWS, MOVE_ROWS)
    f = jnp.concatenate([g1 * a + g2 * b for a, b in zip(first, second)], axis=-1)
    o_ref[...] = _layer_norm(alpha * x_ref[...] + f, g_ref[...], b_ref[...])

    @pl.when(i == pl.num_programs(0) - 1)
    def _drain():
        wait_slot(1 - slot)


def _combine(ys, dest1, dest2, gates, x, g, b, alpha):
    t = x.shape[0]
    steps = t // MOVE_ROWS
    dest = jnp.concatenate([dest1.reshape(steps, 1, MOVE_ROWS), dest2.reshape(steps, 1, MOVE_ROWS)], axis=-1)
    idx_shape = (1, 1, TOP_K * MOVE_ROWS)
    row = lambda i: (i, 0)
    const = lambda i: (0, 0)
    return pl.pallas_call(
        functools.partial(_combine_body, alpha=alpha),
        grid=(steps,),
        in_specs=[pl.BlockSpec(idx_shape, lambda i: (0, 0, 0), memory_space=pltpu.SMEM),
                  pl.BlockSpec(idx_shape, lambda i: (jnp.minimum(i + 1, steps - 1), 0, 0), memory_space=pltpu.SMEM),
                  pl.BlockSpec(memory_space=pl.ANY),
                  pl.BlockSpec((MOVE_ROWS, LANES), row), pl.BlockSpec((MOVE_ROWS, D_MODEL), row),
                  pl.BlockSpec((1, D_MODEL), const), pl.BlockSpec((1, D_MODEL), const)],
        out_specs=pl.BlockSpec((MOVE_ROWS, D_MODEL), row),
        out_shape=jax.ShapeDtypeStruct((t, D_MODEL), F32),
        scratch_shapes=[pltpu.VMEM((2, TOP_K * MOVE_ROWS * SUBLANES, LANES), F32), pltpu.SemaphoreType.DMA((2,))],
        compiler_params=_params("arbitrary"),
        name="combine_ln",
    )(dest, dest, ys, gates, x, g.reshape(1, -1), b.reshape(1, -1))


def _invert_body(d1_ref, d2_ref, src_ref):
    n_tokens = d1_ref.shape[0]

    def fill(r, _):
        src_ref[r] = lax.rem(r, n_tokens)
        return 0
    lax.fori_loop(0, src_ref.shape[0], fill, 0, unroll=4 * SCALAR_LOOP_UNROLL)

    def place(t, _):
        src_ref[d1_ref[t]] = t
        src_ref[d2_ref[t]] = t
        return 0
    lax.fori_loop(0, d1_ref.shape[0], place, 0, unroll=SCALAR_LOOP_UNROLL)


def _invert_placement(dest1, dest2, n_rows):
    smem = pl.BlockSpec(memory_space=pltpu.SMEM)
    return pl.pallas_call(
        _invert_body,
        in_specs=[smem, smem],
        out_specs=smem,
        out_shape=jax.ShapeDtypeStruct((n_rows,), jnp.int32),
        name="invert_placement",
    )(dest1, dest2)


def _moe(x1, x1_tiles, meta, gates, counts, w_gate, w_up, w_down, first_expert, g, b, alpha):
    t = x1.shape[0]
    e1, e2, rank1, rank2 = meta[:, 0], meta[:, 1], meta[:, 2], meta[:, 3]
    sizes = counts[0, :N_EXPERTS].astype(jnp.int32)
    padded = (sizes + FFN_ROWS - 1) // FFN_ROWS * FFN_ROWS
    group_end = jnp.cumsum(padded)
    group_start = group_end - padded
    dest1 = group_start[e1] + rank1
    dest2 = group_start[e2] + rank2
    n_blocks = (t * TOP_K) // FFN_ROWS + N_EXPERTS
    block_start = jnp.arange(n_blocks, dtype=jnp.int32) * FFN_ROWS
    block_e = jnp.minimum(jnp.sum(group_end[None, :] <= block_start[:, None], axis=1), N_EXPERTS - 1)
    n_used = (group_end[-1] // FFN_ROWS).reshape(1)
    src = _invert_placement(dest1, dest2, n_blocks * FFN_ROWS)
    ys = _ffn((block_e + first_expert).astype(jnp.int32), n_used.astype(jnp.int32), x1_tiles, w_gate, w_up,
              w_down, src=src)
    return _combine(ys, dest1, dest2, gates, x1, g, b, alpha)


def kernel(x, w_in, w_out, attn_lambda, attn_subln_g, pool_w, pool_scale, conv_w, conv_b, lru_wa, lru_ba,
           lru_wi, lru_bi, lru_lambda, ln1_g, ln1_b, ln2_g, ln2_b, ffn_w_gate, ffn_w_up, ffn_w_down,
           router_w, moe_w_gate, moe_w_up, moe_w_down):
    batch, seq, d = x.shape
    depth = w_in.shape[0]
    assert d == D_MODEL and seq % ATTN_TILE == 0 and seq % MIX_ROWS == 0
    t = batch * seq
    assert t % FFN_ROWS == 0 and t % ROW_TILE == 0
    alpha = (2.0 * depth) ** 0.25
    dense_used = jnp.full((1,), t // FFN_ROWS, jnp.int32)
    moe_gate = moe_w_gate.reshape(-1, D_MODEL, D_FF)
    moe_up = moe_w_up.reshape(-1, D_MODEL, D_FF)
    moe_down = moe_w_down.reshape(-1, D_FF, D_MODEL)
    xt = x.reshape(t, d)
    for l in range(depth):
        lambda_init = 0.8 - 0.6 * math.exp(-0.3 * l)
        q, k, v, rest = _inproj(xt, w_in, l)
        y_attn = _attention(q, k, v, attn_lambda[l], attn_subln_g[l], lambda_init, batch, seq)
        y_mix = _mixer(rest, pool_w[l], pool_scale[l], conv_w[l], conv_b[l], lru_wa[l], lru_ba[l],
                       lru_wi[l], lru_bi[l], lru_lambda[l], batch, seq)
        if l % 2 == 0:
            x1, = _outproj(y_attn, y_mix, xt, w_out, l, ln1_g[l], ln1_b[l], alpha)
            dense_blocks = jnp.full((t // FFN_ROWS,), l // 2, jnp.int32)
            xt = _ffn(dense_blocks, dense_used, x1, ffn_w_gate, ffn_w_up, ffn_w_down,
                      ln=(ln2_g[l], ln2_b[l]), alpha=alpha)
        else:
            x1, x1_tiles, meta, gates, counts = _outproj(y_attn, y_mix, xt, w_out, l, ln1_g[l], ln1_b[l], alpha,
                                                         w_router=router_w[l // 2])
            xt = _moe(x1, x1_tiles, meta, gates, counts, moe_gate, moe_up, moe_down, (l // 2) * N_EXPERTS,
                      ln2_g[l], ln2_b[l], alpha)
    return xt.reshape(batch, seq, d)
```

```python
import functools
import math

import jax
import jax.numpy as jnp
from jax import lax
from jax.experimental import pallas as pl
from jax.experimental.pallas import tpu as pltpu

F32 = jnp.float32
BF16 = jnp.bfloat16

D_MODEL = 1024
CHUNK = 64
ATTN_HEADS = 4
ATTN_WIDTH = 512
ATTN_HEAD_DIM = 64
HEAD_WIDTH = 2 * ATTN_HEAD_DIM
POOL_WINDOWS = (2, 4, 8, 16)
POOL_WIDTH = 256
POOL_GROUP_DIM = 64
LRU_WIDTH = 256
LRU_C = 8.0
CONV_WIDTH = 4
REST_WIDTH = POOL_WIDTH + 2 * LRU_WIDTH
IN_WIDTH = 3 * ATTN_WIDTH + REST_WIDTH
MIX_WIDTH = POOL_WIDTH + LRU_WIDTH
D_FF = 2816
N_EXPERTS = 8
TOP_K = 2
LN_EPS = 1e-5
HEAD_NORM_EPS = 1e-5

LANES = 128
SUBLANES = 8
VMEM_LIMIT_BYTES = 56 * 1024 * 1024

ROW_TILE = 512
ATTN_TILE = 256
ATTN_HEADS_PER_STEP = 4
ATTN_SUM_ROWS = 16
ATTN_SCORE_SCALE = ATTN_HEAD_DIM ** -0.5 * math.log2(math.e)
MIX_ROWS = 256
FFN_ROWS = 1024
FFN_COLS = 256
FFN_STEPS = D_FF // FFN_COLS
FFN_GATHER_ROWS_PER_STEP = -(-FFN_ROWS // ((FFN_STEPS - 1) * SUBLANES)) * SUBLANES
FFN_GATHER_ROWS = FFN_GATHER_ROWS_PER_STEP * FFN_STEPS
MOVE_ROWS = 256
ROUTE_META = 8
SCALAR_LOOP_UNROLL = 8
DMA_ISSUE_UNROLL = 8
NEG_BIG = -1e30
F32_TINY = float(jnp.finfo(jnp.float32).tiny)
assert MIX_ROWS >= max(POOL_WINDOWS)


def _params(*semantics):
    return pltpu.CompilerParams(dimension_semantics=semantics, vmem_limit_bytes=VMEM_LIMIT_BYTES)


def _layer_norm(z, g, b):
    mu = jnp.mean(z, axis=-1, keepdims=True)
    zc = z - mu
    var = jnp.mean(zc * zc, axis=-1, keepdims=True)
    return zc * lax.rsqrt(var + LN_EPS) * g + b


def _inproj_body(x_ref, w_ref, q_ref, k_ref, v_ref, r_ref, wbf_ref):
    @pl.when(pl.program_id(0) == 0)
    def _cast_weights():
        for c in range(0, IN_WIDTH, 256):
            wbf_ref[:, c:c + 256] = w_ref[0, :, c:c + 256].astype(BF16)

    xb = x_ref[...].astype(BF16)

    def proj(c0, c1):
        return jnp.dot(xb, wbf_ref[:, c0:c1], preferred_element_type=F32)

    q_ref[...] = (proj(0, ATTN_WIDTH) * ATTN_SCORE_SCALE).astype(BF16)
    k_ref[...] = proj(ATTN_WIDTH, 2 * ATTN_WIDTH).astype(BF16)
    v_ref[...] = proj(2 * ATTN_WIDTH, 3 * ATTN_WIDTH).astype(BF16)
    r_ref[...] = proj(3 * ATTN_WIDTH, IN_WIDTH)


def _inproj(x, w, layer):
    t = x.shape[0]
    row = lambda i: (i, 0)
    return pl.pallas_call(
        _inproj_body,
        grid=(t // ROW_TILE,),
        in_specs=[pl.BlockSpec((ROW_TILE, D_MODEL), row),
                  pl.BlockSpec((1, D_MODEL, IN_WIDTH), lambda i: (layer, 0, 0))],
        out_specs=[pl.BlockSpec((ROW_TILE, ATTN_WIDTH), row)] * 3
        + [pl.BlockSpec((ROW_TILE, REST_WIDTH), row)],
        out_shape=[jax.ShapeDtypeStruct((t, ATTN_WIDTH), BF16)] * 3
        + [jax.ShapeDtypeStruct((t, REST_WIDTH), F32)],
        scratch_shapes=[pltpu.VMEM((D_MODEL, IN_WIDTH), BF16)],
        compiler_params=_params("arbitrary"),
        name="inproj",
    )(x, w)


def _attn_body(lam_ref, g_ref, q_ref, k_ref, v_ref, o_ref, vt_ref, *, lambda_init, seq):
    tq = ATTN_TILE
    lv = lam_ref[...]
    lam = (jnp.exp(jnp.sum(lv[0:1] * lv[1:2], axis=-1, keepdims=True))
           - jnp.exp(jnp.sum(lv[2:3] * lv[3:4], axis=-1, keepdims=True)) + lambda_init)
    gain = g_ref[...] * (1.0 - lambda_init)

    heads = range(ATTN_HEADS_PER_STEP)
    cols = [slice(g * HEAD_WIDTH, (g + 1) * HEAD_WIDTH) for g in heads]
    for g in heads:
        for c in range(seq // tq):
            vt_ref[g, c, 0:HEAD_WIDTH, :] = v_ref[c * tq:(c + 1) * tq, cols[g]].astype(F32).T.astype(BF16)
            vt_ref[g, c, HEAD_WIDTH:, :] = jnp.ones((ATTN_SUM_ROWS, tq), BF16)

    lane = lax.broadcasted_iota(jnp.int32, (tq, HEAD_WIDTH), 1)
    first_map = lane < ATTN_HEAD_DIM
    key = lax.broadcasted_iota(jnp.int32, (tq, 2 * tq), 0)
    qry = lax.broadcasted_iota(jnp.int32, (tq, 2 * tq), 1)
    q_chunk = jnp.where(qry >= tq, qry - tq, qry) // CHUNK
    visible = (key // CHUNK) <= q_chunk

    def q_block(i, _):
        q0 = pl.multiple_of(i * tq, tq)
        qqs = []
        for g in heads:
            q = q_ref[pl.ds(q0, tq), cols[g]]
            zero = jnp.zeros_like(q)
            qqs.append(jnp.concatenate([jnp.where(first_map, q, zero), jnp.where(first_map, zero, q)], axis=0))

        def step(j, carry, masked):
            k0 = pl.multiple_of(j * tq, tq)
            out = []
            scores = [lax.dot_general(k_ref[pl.ds(k0, tq), cols[g]], qqs[g], (((1,), (1,)), ((), ())),
                                      preferred_element_type=F32) for g in heads]
            for g in heads:
                m, acc = carry[g]
                s = scores[g]
                if masked:
                    s = jnp.where(visible, s, NEG_BIG)
                m_new = jnp.maximum(m, jnp.max(s, axis=0, keepdims=True))
                p = jnp.exp2(s - m_new)
                scale = jnp.exp2(m - m_new)
                pv = jnp.dot(vt_ref[g, j], p.astype(BF16), preferred_element_type=F32)
                out.append((m_new, scale * acc + pv))
            return tuple(out)

        init = tuple((jnp.full((1, 2 * tq), NEG_BIG, F32),
                      jnp.zeros((HEAD_WIDTH + ATTN_SUM_ROWS, 2 * tq), F32)) for _ in heads)
        carry = lax.fori_loop(0, i, lambda j, c: step(j, c, False), init)
        carry = step(i, carry, True)
        for g in heads:
            _, acc = carry[g]
            l = acc[HEAD_WIDTH:HEAD_WIDTH + 1, :]
            acc = acc[:HEAD_WIDTH, :]
            o = acc[:, :tq] / l[:, :tq] - lam * (acc[:, tq:] / l[:, tq:])
            o = o * lax.rsqrt(jnp.mean(o * o, axis=0, keepdims=True) + HEAD_NORM_EPS)
            o_ref[pl.ds(q0, tq), cols[g]] = (o.T * gain).astype(BF16)
        return 0

    lax.fori_loop(0, seq // tq, q_block, 0)


def _attention(q, k, v, lam_params, subln_g, lambda_init, batch, seq):
    t = q.shape[0]
    blk = pl.BlockSpec((seq, ATTN_HEADS_PER_STEP * HEAD_WIDTH), lambda b, h: (b, h))
    return pl.pallas_call(
        functools.partial(_attn_body, lambda_init=lambda_init, seq=seq),
        grid=(batch, ATTN_HEADS // ATTN_HEADS_PER_STEP),
        in_specs=[pl.BlockSpec((4, ATTN_HEAD_DIM), lambda b, h: (0, 0)),
                  pl.BlockSpec((1, HEAD_WIDTH), lambda b, h: (0, 0)),
                  blk, blk, blk],
        out_specs=blk,
        out_shape=jax.ShapeDtypeStruct((t, ATTN_WIDTH), BF16),
        scratch_shapes=[pltpu.VMEM((ATTN_HEADS_PER_STEP, seq // ATTN_TILE, HEAD_WIDTH + ATTN_SUM_ROWS, ATTN_TILE),
                                   BF16)],
        compiler_params=_params("arbitrary", "arbitrary"),
        name="diff_attention",
    )(lam_params, subln_g.reshape(1, HEAD_WIDTH), q, k, v)


def _mixer_body(r_ref, pw_ref, ps_ref, cw_ref, cb_ref, wa_ref, ba_ref, wi_ref, bi_ref, lam_ref, y_ref,
                a_ref, b_ref, h_ref, inv_ref, *, seq):
    rows = MIX_ROWS
    pool_hist = max(POOL_WINDOWS)
    conv_hist = SUBLANES
    pw = pw_ref[...]
    wa = wa_ref[...]
    wi = wi_ref[...]
    ps, cb, ba, bi = ps_ref[...], cb_ref[...], ba_ref[...], bi_ref[...]
    cw = cw_ref[...]
    neg_c_softplus = -LRU_C * jax.nn.softplus(-lam_ref[...])

    lane_e = lax.broadcasted_iota(jnp.int32, (rows + pool_hist, POOL_WIDTH), 1)
    lane = lax.broadcasted_iota(jnp.int32, (rows, POOL_WIDTH), 1)
    row = lax.broadcasted_iota(jnp.int32, (rows, POOL_WIDTH), 0)
    win = jnp.where(lane < 64, 2, jnp.where(lane < 128, 4, jnp.where(lane < 192, 8, 16)))
    inv_win = jnp.where(lane < 64, 1 / 2, jnp.where(lane < 128, 1 / 4, jnp.where(lane < 192, 1 / 8, 1 / 16)))
    inv_ref[...] = 1.0 / (row + 1).astype(F32)
    groups = rows // SUBLANES
    row_in_group = lax.broadcasted_iota(jnp.int32, (groups, SUBLANES, LRU_WIDTH), 1)
    group = lax.broadcasted_iota(jnp.int32, (groups, LRU_WIDTH), 0)

    def chunk(c, carry):
        tail_u, tail_x, h_prev = carry
        r0 = pl.multiple_of(c * rows, rows)
        u = r_ref[pl.ds(r0, rows), 0:POOL_WIDTH]
        xr = r_ref[pl.ds(r0, rows), POOL_WIDTH:POOL_WIDTH + LRU_WIDTH]
        xg = r_ref[pl.ds(r0, rows), POOL_WIDTH + LRU_WIDTH:REST_WIDTH]

        ue = jnp.concatenate([tail_u, u], axis=0)
        w2 = ue + pltpu.roll(ue, 1, 0)
        w4 = w2 + pltpu.roll(w2, 2, 0)
        w8 = w4 + pltpu.roll(w4, 4, 0)
        w16 = w8 + pltpu.roll(w8, 8, 0)
        ws = jnp.where(lane_e < 64, w2, jnp.where(lane_e < 128, w4, jnp.where(lane_e < 192, w8, w16)))
        ws = ws[pool_hist:]
        inv_count = jnp.where(r0 + row + 1 >= win, inv_win, inv_ref[...])
        pooled = ws * inv_count - u
        y_pool = jnp.dot(pooled.astype(BF16), pw, preferred_element_type=F32) * ps

        xe = jnp.concatenate([tail_x, xr], axis=0)
        xc = cb + pltpu.roll(xe, 3, 0)[conv_hist:] * cw[0:1]
        xc = xc + pltpu.roll(xe, 2, 0)[conv_hist:] * cw[1:2]
        xc = xc + pltpu.roll(xe, 1, 0)[conv_hist:] * cw[2:3]
        xc = xc + xr * cw[3:4]
        xcb = xc.astype(BF16)
        r_gate = jax.nn.sigmoid(jnp.dot(xcb, wa, preferred_element_type=F32) + ba)
        i_gate = jax.nn.sigmoid(jnp.dot(xcb, wi, preferred_element_type=F32) + bi)
        log_a = r_gate * neg_c_softplus
        a = jnp.exp(log_a)
        gap = -jnp.tanh(log_a) * (a * a + 1.0)
        b = (gap * lax.rsqrt(jnp.maximum(gap, F32_TINY))) * (i_gate * xc)

        def doubling(a, b, pos, length, axis):
            s = 1
            while s < length:
                keep = pos >= s
                a_prev = jnp.where(keep, pltpu.roll(a, s, axis), 1.0)
                b_prev = jnp.where(keep, pltpu.roll(b, s, axis), 0.0)
                b = a * b_prev + b
                a = a * a_prev
                s *= 2
            return a, b

        a, b = doubling(a.reshape(groups, SUBLANES, LRU_WIDTH), b.reshape(groups, SUBLANES, LRU_WIDTH),
                        row_in_group, SUBLANES, 1)
        a = a.reshape(rows, LRU_WIDTH)
        b = b.reshape(rows, LRU_WIDTH)
        halves = range(LRU_WIDTH // LANES)
        ends = pl.ds(SUBLANES - 1, groups, stride=SUBLANES)
        for k in halves:
            a_ref[k] = a[:, k * LANES:(k + 1) * LANES]
            b_ref[k] = b[:, k * LANES:(k + 1) * LANES]
        a_end = jnp.concatenate([a_ref[k, ends, :] for k in halves], axis=-1)
        b_end = jnp.concatenate([b_ref[k, ends, :] for k in halves], axis=-1)
        a_end, b_end = doubling(a_end, b_end, group, groups, 0)
        h_end = a_end * h_prev + b_end
        h_ref[...] = jnp.where(group == 0, h_prev, pltpu.roll(h_end, 1, 0))
        h_start = jnp.concatenate([jnp.broadcast_to(h_ref[g:g + 1, :], (SUBLANES, LRU_WIDTH))
                                   for g in range(groups)], axis=0)
        h = a * h_start + b
        y_lru = h * jax.nn.gelu(xg)

        y_ref[pl.ds(r0, rows), 0:POOL_WIDTH] = y_pool.astype(BF16)
        y_ref[pl.ds(r0, rows), POOL_WIDTH:MIX_WIDTH] = y_lru.astype(BF16)
        return u[rows - pool_hist:], xr[rows - conv_hist:], h_end[groups - 1:groups]

    init = (jnp.zeros((pool_hist, POOL_WIDTH), F32), jnp.zeros((conv_hist, LRU_WIDTH), F32),
            jnp.zeros((1, LRU_WIDTH), F32))
    lax.fori_loop(0, seq // rows, chunk, init)


def _block_diag(w):
    g, c, d = w.shape
    eye = jnp.eye(g, dtype=w.dtype)
    return (eye[:, None, :, None] * w[:, :, None, :]).reshape(g * c, g * d)


def _mixer(rest, pool_w, pool_scale, conv_w, conv_b, wa, ba, wi, bi, lru_lambda, batch, seq):
    t = rest.shape[0]
    full = lambda shape: pl.BlockSpec(shape, lambda b: (0, 0))
    vec = lambda a: a.reshape(1, -1)
    return pl.pallas_call(
        functools.partial(_mixer_body, seq=seq),
        grid=(batch,),
        in_specs=[pl.BlockSpec((seq, REST_WIDTH), lambda b: (b, 0)),
                  full((POOL_WIDTH, POOL_WIDTH)), full((1, POOL_WIDTH)),
                  full((CONV_WIDTH, LRU_WIDTH)), full((1, LRU_WIDTH)),
                  full((LRU_WIDTH, LRU_WIDTH)), full((1, LRU_WIDTH)),
                  full((LRU_WIDTH, LRU_WIDTH)), full((1, LRU_WIDTH)),
                  full((1, LRU_WIDTH))],
        out_specs=pl.BlockSpec((seq, MIX_WIDTH), lambda b: (b, 0)),
        out_shape=jax.ShapeDtypeStruct((t, MIX_WIDTH), BF16),
        scratch_shapes=[pltpu.VMEM((LRU_WIDTH // LANES, MIX_ROWS, LANES), F32),
                        pltpu.VMEM((LRU_WIDTH // LANES, MIX_ROWS, LANES), F32),
                        pltpu.VMEM((MIX_ROWS // SUBLANES, LRU_WIDTH), F32),
                        pltpu.VMEM((MIX_ROWS, POOL_WIDTH), F32)],
        compiler_params=_params("arbitrary"),
        name="pool_lru_mixer",
    )(rest, _block_diag(pool_w).astype(BF16), vec(pool_scale), conv_w, vec(conv_b),
      _block_diag(wa).astype(BF16), vec(ba), _block_diag(wi).astype(BF16), vec(bi), vec(lru_lambda))


def _outproj_body(ya_ref, ym_ref, x_ref, w_ref, g_ref, b_ref, *rest, alpha, route):
    if route:
        wr_ref, o_ref, ot_ref, meta_ref, gate_ref, cnt_ref, wbf_ref, run_ref = rest
    else:
        o_ref, wbf_ref = rest

    @pl.when(pl.program_id(0) == 0)
    def _cast_weights():
        for c in range(0, D_MODEL, 256):
            wbf_ref[:, c:c + 256] = w_ref[0, :, c:c + 256].astype(BF16)

    mix = jnp.dot(ya_ref[...], wbf_ref[0:ATTN_WIDTH, :], preferred_element_type=F32)
    mix = mix + jnp.dot(ym_ref[...], wbf_ref[ATTN_WIDTH:, :], preferred_element_type=F32)
    x1 = _layer_norm(alpha * x_ref[...] + mix, g_ref[...], b_ref[...])
    o_ref[...] = x1
    if route:
        _store_row_tiles(ot_ref, x1)
        _route(x1, wr_ref, meta_ref, gate_ref, cnt_ref, run_ref)


def _outproj(y_attn, y_mix, x, w, layer, g, b, alpha, w_router=None):
    t = x.shape[0]
    route = w_router is not None
    row = lambda i: (i, 0)
    const = lambda i: (0, 0)
    in_specs = [pl.BlockSpec((ROW_TILE, ATTN_WIDTH), row), pl.BlockSpec((ROW_TILE, MIX_WIDTH), row),
                pl.BlockSpec((ROW_TILE, D_MODEL), row),
                pl.BlockSpec((1, D_MODEL, D_MODEL), lambda i: (layer, 0, 0)),
                pl.BlockSpec((1, D_MODEL), const), pl.BlockSpec((1, D_MODEL), const)]
    args = [y_attn, y_mix, x, w, g.reshape(1, -1), b.reshape(1, -1)]
    out_specs = [pl.BlockSpec((ROW_TILE, D_MODEL), row)]
    out_shape = [jax.ShapeDtypeStruct((t, D_MODEL), F32)]
    scratch = [pltpu.VMEM((D_MODEL, D_MODEL), BF16)]
    if route:
        in_specs.append(pl.BlockSpec((N_EXPERTS, D_MODEL), const))
        args.append(w_router.T)
        out_specs += [pl.BlockSpec((ROW_TILE * SUBLANES, LANES), row),
                      pl.BlockSpec((ROW_TILE, ROUTE_META), row), pl.BlockSpec((ROW_TILE, LANES), row),
                      pl.BlockSpec((SUBLANES, LANES), const)]
        out_shape += [jax.ShapeDtypeStruct((t * SUBLANES, LANES), F32),
                      jax.ShapeDtypeStruct((t, ROUTE_META), jnp.int32), jax.ShapeDtypeStruct((t, LANES), F32),
                      jax.ShapeDtypeStruct((SUBLANES, LANES), F32)]
        scratch.append(pltpu.VMEM((1, LANES), F32))
    return pl.pallas_call(
        functools.partial(_outproj_body, alpha=alpha, route=route),
        grid=(t // ROW_TILE,),
        in_specs=in_specs,
        out_specs=out_specs,
        out_shape=out_shape,
        scratch_shapes=scratch,
        compiler_params=_params("arbitrary"),
        name="outproj_ln_route" if route else "outproj_ln",
    )(*args)


def _store_row_tiles(o_ref, val):
    rows = val.shape[0]
    for s in range(SUBLANES):
        o_ref[pl.ds(s, rows, stride=SUBLANES), :] = val[:, s * LANES:(s + 1) * LANES]


def _load_row_tiles(ref, first_row, rows):
    return [ref[pl.ds(first_row * SUBLANES + s, rows, stride=SUBLANES), :] for s in range(SUBLANES)]


def _tile_window(row):
    return pl.ds(pl.multiple_of(row * SUBLANES, SUBLANES), SUBLANES)


def _start_row_copies(n_rows, make_copy, priorities):
    k = len(priorities)

    def body(q, _):
        for p, priority in enumerate(priorities):
            make_copy(q * k + p).start(priority=priority)
        return 0
    lax.fori_loop(0, n_rows // k, body, 0, unroll=DMA_ISSUE_UNROLL // k)


def _gather_copies(src_ref, x_hbm, buf_ref, sem, slot):
    def copy(r):
        return pltpu.make_async_copy(x_hbm.at[_tile_window(src_ref[0, 0, r])], buf_ref.at[slot, _tile_window(r)],
                                     sem.at[slot])
    return copy


def _ffn_body(be_ref, nu_ref, *rest, alpha, fuse_ln, gather):
    if gather:
        src0_ref, src_next_ref, x_hbm, wg_ref, wu_ref, wd_ref, o_ref, xb_ref, acc_ref, buf_ref, sem = rest
    elif fuse_ln:
        x_ref, wg_ref, wu_ref, wd_ref, g_ref, b_ref, o_ref, xb_ref, acc_ref = rest
    else:
        x_ref, wg_ref, wu_ref, wd_ref, o_ref, xb_ref, acc_ref = rest
    i = pl.program_id(0)
    j = pl.program_id(1)
    n_used = nu_ref[0]
    used = i < n_used

    if gather:
        slot = lax.rem(i, 2)
        priority = 1

        def wait_slot(which):
            pltpu.make_async_copy(buf_ref.at[which], buf_ref.at[which], sem.at[which]).wait()

        @pl.when(jnp.logical_and(used, jnp.logical_and(i == 0, j == 0)))
        def _first_rows():
            _start_row_copies(FFN_GATHER_ROWS, _gather_copies(src0_ref, x_hbm, buf_ref, sem, 0),
                              priorities=(priority,))

        @pl.when(jnp.logical_and(used, j == 0))
        def _start():
            wait_slot(slot)
            for s, part in enumerate(_load_row_tiles(buf_ref.at[slot], 0, FFN_ROWS)):
                xb_ref[:, s * LANES:(s + 1) * LANES] = part.astype(BF16)
            acc_ref[...] = jnp.zeros_like(acc_ref)
    else:
        @pl.when(jnp.logical_and(used, j == 0))
        def _start():
            xb_ref[...] = x_ref[...].astype(BF16)
            acc_ref[...] = jnp.zeros_like(acc_ref)

    @pl.when(used)
    def _accumulate():
        if gather:
            copy = _gather_copies(src_next_ref, x_hbm, buf_ref, sem, 1 - slot)
            for r in range(FFN_GATHER_ROWS_PER_STEP):
                copy(j * FFN_GATHER_ROWS_PER_STEP + r).start(priority=priority)
        xb = xb_ref[...]
        gate = jnp.dot(xb, wg_ref[0].astype(BF16), preferred_element_type=F32)
        up = jnp.dot(xb, wu_ref[0].astype(BF16), preferred_element_type=F32)
        hidden = (jax.nn.silu(gate) * up).astype(BF16)
        acc_ref[...] += jnp.dot(hidden, wd_ref[0].astype(BF16), preferred_element_type=F32)

    @pl.when(jnp.logical_and(used, j == pl.num_programs(1) - 1))
    def _finish():
        if gather:
            @pl.when(i == n_used - 1)
            def _drain():
                wait_slot(1 - slot)
        if fuse_ln:
            o_ref[...] = _layer_norm(alpha * x_ref[...] + acc_ref[...], g_ref[...], b_ref[...])
        elif gather:
            _store_row_tiles(o_ref, acc_ref[...])
        else:
            o_ref[...] = acc_ref[...]

    @pl.when(jnp.logical_and(jnp.logical_not(used), j == 0))
    def _empty():
        o_ref[...] = jnp.zeros_like(o_ref)


def _ffn(block_e, n_used, x, w_gate, w_up, w_down, ln=None, alpha=1.0, src=None):
    gather = src is not None
    n_blocks = block_e.shape[0]
    nj = D_FF // FFN_COLS

    def col(i, j, be, nu):
        return jnp.where(i < nu[0], j, nj - 1)

    def blk(i, be, nu):
        return jnp.minimum(i, nu[0] - 1)

    row = lambda i, j, be, nu: (i, 0)
    w_specs = [pl.BlockSpec((1, D_MODEL, FFN_COLS), lambda i, j, be, nu: (be[blk(i, be, nu)], 0, col(i, j, be, nu))),
               pl.BlockSpec((1, D_MODEL, FFN_COLS), lambda i, j, be, nu: (be[blk(i, be, nu)], 0, col(i, j, be, nu))),
               pl.BlockSpec((1, FFN_COLS, D_MODEL), lambda i, j, be, nu: (be[blk(i, be, nu)], col(i, j, be, nu), 0))]
    scratch = [pltpu.VMEM((FFN_ROWS, D_MODEL), BF16), pltpu.VMEM((FFN_ROWS, D_MODEL), F32)]
    if gather:
        assert ln is None
        dummy = jnp.broadcast_to(jnp.arange(FFN_GATHER_ROWS - FFN_ROWS, dtype=jnp.int32),
                                 (n_blocks, 1, FFN_GATHER_ROWS - FFN_ROWS))
        src3 = jnp.concatenate([src.reshape(n_blocks, 1, FFN_ROWS), dummy], axis=-1)
        in_specs = [pl.BlockSpec((1, 1, FFN_GATHER_ROWS), lambda i, j, be, nu: (0, 0, 0), memory_space=pltpu.SMEM),
                    pl.BlockSpec((1, 1, FFN_GATHER_ROWS),
                                 lambda i, j, be, nu: (jnp.minimum(i + 1, n_blocks - 1), 0, 0),
                                 memory_space=pltpu.SMEM),
                    pl.BlockSpec(memory_space=pl.ANY)] + w_specs
        args = [src3, src3, x, w_gate, w_up, w_down]
        scratch += [pltpu.VMEM((2, FFN_GATHER_ROWS * SUBLANES, LANES), F32), pltpu.SemaphoreType.DMA((2,))]
        out_spec = pl.BlockSpec((FFN_ROWS * SUBLANES, LANES), row)
        out_shape = jax.ShapeDtypeStruct((n_blocks * FFN_ROWS * SUBLANES, LANES), F32)
    else:
        out_spec = pl.BlockSpec((FFN_ROWS, D_MODEL), row)
        out_shape = jax.ShapeDtypeStruct((n_blocks * FFN_ROWS, D_MODEL), F32)
        in_specs = [pl.BlockSpec((FFN_ROWS, D_MODEL), row)] + w_specs
        args = [x, w_gate, w_up, w_down]
        if ln is not None:
            in_specs += [pl.BlockSpec((1, D_MODEL), lambda i, j, be, nu: (0, 0))] * 2
            args += [ln[0].reshape(1, -1), ln[1].reshape(1, -1)]
    return pl.pallas_call(
        functools.partial(_ffn_body, alpha=alpha, fuse_ln=ln is not None, gather=gather),
        grid_spec=pltpu.PrefetchScalarGridSpec(
            num_scalar_prefetch=2,
            grid=(n_blocks, nj),
            in_specs=in_specs,
            out_specs=out_spec,
            scratch_shapes=scratch),
        out_shape=out_shape,
        compiler_params=_params("arbitrary", "arbitrary"),
        name="swiglu_ln" if ln is not None else "swiglu_experts",
    )(block_e, n_used, *args)


def _route(x, w_ref, meta_ref, gate_ref, cnt_ref, run_ref):
    tm = x.shape[0]

    @pl.when(pl.program_id(0) == 0)
    def _init():
        run_ref[...] = jnp.zeros_like(run_ref)

    lane_i = lax.broadcasted_iota(jnp.int32, (tm, LANES), 1)
    lane = lane_i.astype(F32)
    logits = jnp.full((tm, LANES), -jnp.inf, F32)
    for e in range(N_EXPERTS):
        logit_e = jnp.sum(x * w_ref[e:e + 1, :], axis=-1, keepdims=True)
        logits = jnp.where(lane_i == e, logit_e, logits)
    m1 = jnp.max(logits, axis=-1, keepdims=True)
    e1 = jnp.min(jnp.where(logits == m1, lane, float(LANES)), axis=-1, keepdims=True)
    rest = jnp.where(lane == e1, -jnp.inf, logits)
    m2 = jnp.max(rest, axis=-1, keepdims=True)
    e2 = jnp.min(jnp.where(rest == m2, lane, float(LANES)), axis=-1, keepdims=True)
    ex = jnp.exp(m2 - m1)
    g1 = 1.0 / (1.0 + ex)
    g2 = ex / (1.0 + ex)

    chosen = jnp.logical_or(lane == e1, lane == e2)
    r_i = lax.broadcasted_iota(jnp.int32, (tm, tm), 0)
    c_i = lax.broadcasted_iota(jnp.int32, (tm, tm), 1)
    earlier = (c_i < r_i).astype(BF16)
    before = jnp.dot(earlier, chosen.astype(BF16), preferred_element_type=F32) + run_ref[...]
    rank1 = jnp.sum(jnp.where(lane == e1, before, 0.0), axis=-1, keepdims=True)
    rank2 = jnp.sum(jnp.where(lane == e2, before, 0.0), axis=-1, keepdims=True)
    total = run_ref[...] + jnp.sum(chosen.astype(F32), axis=0, keepdims=True)
    run_ref[...] = total
    cnt_ref[...] = jnp.broadcast_to(total, cnt_ref.shape)
    meta = jnp.where(lane_i == 0, e1, jnp.where(lane_i == 1, e2, jnp.where(lane_i == 2, rank1, rank2)))
    meta_ref[...] = meta[:, :ROUTE_META].astype(jnp.int32)
    gate_ref[...] = jnp.where(lane_i == 0, g1, g2)


def _combine_body(d0_ref, dn_ref, ys_hbm, gate_ref, x_ref, g_ref, b_ref, o_ref, buf_ref, sem, *, alpha):
    i = pl.program_id(0)
    slot = lax.rem(i, 2)
    n_rows = buf_ref.shape[1] // SUBLANES

    def row_copy(d_ref, to_slot):
        def copy(r):
            return pltpu.make_async_copy(ys_hbm.at[_tile_window(d_ref[0, 0, r])],
                                         buf_ref.at[to_slot, _tile_window(r)], sem.at[to_slot])
        return copy

    def wait_slot(which):
        pltpu.make_async_copy(buf_ref.at[which], buf_ref.at[which], sem.at[which]).wait()

    @pl.when(i == 0)
    def _first_rows():
        _start_row_copies(n_rows, row_copy(d0_ref, 0), priorities=(0, 1))

    wait_slot(slot)
    next_copy = row_copy(dn_ref, 1 - slot)
    for r in range(n_rows):
        next_copy(r).start(priority=r % 2)

    gates = gate_ref[...]
    g1, g2 = gates[:, 0:1], gates[:, 1:2]
    first = _load_row_tiles(buf_ref.at[slot], 0, MOVE_ROWS)
    second = _load_row_tiles(buf_ref.at[slot], MOVE_ROWS, MOVE_ROWS)
    f = jnp.concatenate([g1 * a + g2 * b for a, b in zip(first, second)], axis=-1)
    o_ref[...] = _layer_norm(alpha * x_ref[...] + f, g_ref[...], b_ref[...])

    @pl.when(i == pl.num_programs(0) - 1)
    def _drain():
        wait_slot(1 - slot)


def _combine(ys, dest1, dest2, gates, x, g, b, alpha):
    t = x.shape[0]
    steps = t // MOVE_ROWS
    dest = jnp.concatenate([dest1.reshape(steps, 1, MOVE_ROWS), dest2.reshape(steps, 1, MOVE_ROWS)], axis=-1)
    idx_shape = (1, 1, TOP_K * MOVE_ROWS)
    row = lambda i: (i, 0)
    const = lambda i: (0, 0)
    return pl.pallas_call(
        functools.partial(_combine_body, alpha=alpha),
        grid=(steps,),
        in_specs=[pl.BlockSpec(idx_shape, lambda i: (0, 0, 0), memory_space=pltpu.SMEM),
                  pl.BlockSpec(idx_shape, lambda i: (jnp.minimum(i + 1, steps - 1), 0, 0), memory_space=pltpu.SMEM),
                  pl.BlockSpec(memory_space=pl.ANY),
                  pl.BlockSpec((MOVE_ROWS, LANES), row), pl.BlockSpec((MOVE_ROWS, D_MODEL), row),
                  pl.BlockSpec((1, D_MODEL), const), pl.BlockSpec((1, D_MODEL), const)],
        out_specs=pl.BlockSpec((MOVE_ROWS, D_MODEL), row),
        out_shape=jax.ShapeDtypeStruct((t, D_MODEL), F32),
        scratch_shapes=[pltpu.VMEM((2, TOP_K * MOVE_ROWS * SUBLANES, LANES), F32), pltpu.SemaphoreType.DMA((2,))],
        compiler_params=_params("arbitrary"),
        name="combine_ln",
    )(dest, dest, ys, gates, x, g.reshape(1, -1), b.reshape(1, -1))


def _invert_body(d1_ref, d2_ref, src_ref):
    n_tokens = d1_ref.shape[0]

    n_rows = src_ref.shape[0]
    for first in range(0, n_rows, n_tokens):
        def fill(r, _, first=first):
            src_ref[r] = r - first
            return 0
        lax.fori_loop(first, min(first + n_tokens, n_rows), fill, 0, unroll=4 * SCALAR_LOOP_UNROLL)

    def place(t, _):
        src_ref[d1_ref[t]] = t
        src_ref[d2_ref[t]] = t
        return 0
    lax.fori_loop(0, d1_ref.shape[0], place, 0, unroll=SCALAR_LOOP_UNROLL)


def _invert_placement(dest1, dest2, n_rows):
    smem = pl.BlockSpec(memory_space=pltpu.SMEM)
    return pl.pallas_call(
        _invert_body,
        in_specs=[smem, smem],
        out_specs=smem,
        out_shape=jax.ShapeDtypeStruct((n_rows,), jnp.int32),
        name="invert_placement",
    )(dest1, dest2)


def _moe(x1, x1_tiles, meta, gates, counts, w_gate, w_up, w_down, first_expert, g, b, alpha):
    t = x1.shape[0]
    e1, e2, rank1, rank2 = meta[:, 0], meta[:, 1], meta[:, 2], meta[:, 3]
    sizes = counts[0, :N_EXPERTS].astype(jnp.int32)
    padded = (sizes + FFN_ROWS - 1) // FFN_ROWS * FFN_ROWS
    group_end = jnp.cumsum(padded)
    group_start = group_end - padded
    dest1 = group_start[e1] + rank1
    dest2 = group_start[e2] + rank2
    n_blocks = (t * TOP_K) // FFN_ROWS + N_EXPERTS
    block_start = jnp.arange(n_blocks, dtype=jnp.int32) * FFN_ROWS
    block_e = jnp.minimum(jnp.sum(group_end[None, :] <= block_start[:, None], axis=1), N_EXPERTS - 1)
    n_used = (group_end[-1] // FFN_ROWS).reshape(1)
    src = _invert_placement(dest1, dest2, n_blocks * FFN_ROWS)
    ys = _ffn((block_e + first_expert).astype(jnp.int32), n_used.astype(jnp.int32), x1_tiles, w_gate, w_up,
              w_down, src=src)
    return _combine(ys, dest1, dest2, gates, x1, g, b, alpha)


def kernel(x, w_in, w_out, attn_lambda, attn_subln_g, pool_w, pool_scale, conv_w, conv_b, lru_wa, lru_ba,
           lru_wi, lru_bi, lru_lambda, ln1_g, ln1_b, ln2_g, ln2_b, ffn_w_gate, ffn_w_up, ffn_w_down,
           router_w, moe_w_gate, moe_w_up, moe_w_down):
    batch, seq, d = x.shape
    depth = w_in.shape[0]
    assert d == D_MODEL and seq % ATTN_TILE == 0 and seq % MIX_ROWS == 0
    t = batch * seq
    assert t % FFN_ROWS == 0 and t % ROW_TILE == 0
    alpha = (2.0 * depth) ** 0.25
    dense_used = jnp.full((1,), t // FFN_ROWS, jnp.int32)
    moe_gate = moe_w_gate.reshape(-1, D_MODEL, D_FF)
    moe_up = moe_w_up.reshape(-1, D_MODEL, D_FF)
    moe_down = moe_w_down.reshape(-1, D_FF, D_MODEL)
    xt = x.reshape(t, d)
    for l in range(depth):
        lambda_init = 0.8 - 0.6 * math.exp(-0.3 * l)
        q, k, v, rest = _inproj(xt, w_in, l)
        y_attn = _attention(q, k, v, attn_lambda[l], attn_subln_g[l], lambda_init, batch, seq)
        y_mix = _mixer(rest, pool_w[l], pool_scale[l], conv_w[l], conv_b[l], lru_wa[l], lru_ba[l],
                       lru_wi[l], lru_bi[l], lru_lambda[l], batch, seq)
        if l % 2 == 0:
            x1, = _outproj(y_attn, y_mix, xt, w_out, l, ln1_g[l], ln1_b[l], alpha)
            dense_blocks = jnp.full((t // FFN_ROWS,), l // 2, jnp.int32)
            xt = _ffn(dense_blocks, dense_used, x1, ffn_w_gate, ffn_w_up, ffn_w_down,
                      ln=(ln2_g[l], ln2_b[l]), alpha=alpha)
        else:
            x1, x1_tiles, meta, gates, counts = _outproj(y_attn, y_mix, xt, w_out, l, ln1_g[l], ln1_b[l], alpha,
                                                         w_router=router_w[l // 2])
            xt = _moe(x1, x1_tiles, meta, gates, counts, moe_gate, moe_up, moe_down, (l // 2) * N_EXPERTS,
                      ln2_g[l], ln2_b[l], alpha)
    return xt.reshape(batch, seq, d)
```

```python
import functools
import math

import jax
import jax.numpy as jnp
from jax import lax
from jax.experimental import pallas as pl
from jax.experimental.pallas import tpu as pltpu

F32 = jnp.float32
BF16 = jnp.bfloat16

D_MODEL = 1024
CHUNK = 64
ATTN_HEADS = 4
ATTN_WIDTH = 512
ATTN_HEAD_DIM = 64
HEAD_WIDTH = 2 * ATTN_HEAD_DIM
POOL_WINDOWS = (2, 4, 8, 16)
POOL_WIDTH = 256
POOL_GROUP_DIM = 64
LRU_WIDTH = 256
LRU_C = 8.0
CONV_WIDTH = 4
REST_WIDTH = POOL_WIDTH + 2 * LRU_WIDTH
IN_WIDTH = 3 * ATTN_WIDTH + REST_WIDTH
MIX_WIDTH = POOL_WIDTH + LRU_WIDTH
D_FF = 2816
N_EXPERTS = 8
TOP_K = 2
LN_EPS = 1e-5
HEAD_NORM_EPS = 1e-5

LANES = 128
SUBLANES = 8
VMEM_LIMIT_BYTES = 56 * 1024 * 1024

ROW_TILE = 512
ATTN_TILE = 256
ATTN_HEADS_PER_STEP = 4
ATTN_SUM_ROWS = 16
ATTN_SCORE_SCALE = ATTN_HEAD_DIM ** -0.5 * math.log2(math.e)
MIX_ROWS = 256
FFN_ROWS = 1024
FFN_COLS = 256
FFN_STEPS = D_FF // FFN_COLS
FFN_GATHER_ROWS_PER_STEP = -(-FFN_ROWS // ((FFN_STEPS - 1) * SUBLANES)) * SUBLANES
FFN_GATHER_ROWS = FFN_GATHER_ROWS_PER_STEP * FFN_STEPS
MOVE_ROWS = 256
ROUTE_META = 8
SCALAR_LOOP_UNROLL = 8
DMA_ISSUE_UNROLL = 8
NEG_BIG = -1e30
F32_TINY = float(jnp.finfo(jnp.float32).tiny)
assert MIX_ROWS >= max(POOL_WINDOWS)


def _params(*semantics):
    return pltpu.CompilerParams(dimension_semantics=semantics, vmem_limit_bytes=VMEM_LIMIT_BYTES)


def _layer_norm(z, g, b):
    mu = jnp.mean(z, axis=-1, keepdims=True)
    zc = z - mu
    var = jnp.mean(zc * zc, axis=-1, keepdims=True)
    return zc * lax.rsqrt(var + LN_EPS) * g + b


def _inproj_body(x_ref, w_ref, q_ref, k_ref, v_ref, r_ref, wbf_ref):
    @pl.when(pl.program_id(0) == 0)
    def _cast_weights():
        for c in range(0, IN_WIDTH, 256):
            wbf_ref[:, c:c + 256] = w_ref[0, :, c:c + 256].astype(BF16)

    xb = x_ref[...].astype(BF16)

    def proj(c0, c1):
        return jnp.dot(xb, wbf_ref[:, c0:c1], preferred_element_type=F32)

    q_ref[...] = (proj(0, ATTN_WIDTH) * ATTN_SCORE_SCALE).astype(BF16)
    k_ref[...] = proj(ATTN_WIDTH, 2 * ATTN_WIDTH).astype(BF16)
    v_ref[...] = proj(2 * ATTN_WIDTH, 3 * ATTN_WIDTH).astype(BF16)
    r_ref[...] = proj(3 * ATTN_WIDTH, IN_WIDTH)


def _inproj(x, w, layer):
    t = x.shape[0]
    row = lambda i: (i, 0)
    return pl.pallas_call(
        _inproj_body,
        grid=(t // ROW_TILE,),
        in_specs=[pl.BlockSpec((ROW_TILE, D_MODEL), row),
                  pl.BlockSpec((1, D_MODEL, IN_WIDTH), lambda i: (layer, 0, 0))],
        out_specs=[pl.BlockSpec((ROW_TILE, ATTN_WIDTH), row)] * 3
        + [pl.BlockSpec((ROW_TILE, REST_WIDTH), row)],
        out_shape=[jax.ShapeDtypeStruct((t, ATTN_WIDTH), BF16)] * 3
        + [jax.ShapeDtypeStruct((t, REST_WIDTH), F32)],
        scratch_shapes=[pltpu.VMEM((D_MODEL, IN_WIDTH), BF16)],
        compiler_params=_params("arbitrary"),
        name="inproj",
    )(x, w)


def _attn_body(lam_ref, g_ref, q_ref, k_ref, v_ref, o_ref, vt_ref, *, lambda_init, seq):
    tq = ATTN_TILE
    lv = lam_ref[...]
    lam = (jnp.exp(jnp.sum(lv[0:1] * lv[1:2], axis=-1, keepdims=True))
           - jnp.exp(jnp.sum(lv[2:3] * lv[3:4], axis=-1, keepdims=True)) + lambda_init)
    gain = g_ref[...] * (1.0 - lambda_init)

    heads = range(ATTN_HEADS_PER_STEP)
    cols = [slice(g * HEAD_WIDTH, (g + 1) * HEAD_WIDTH) for g in heads]
    for g in heads:
        for c in range(seq // tq):
            vt_ref[g, c, 0:HEAD_WIDTH, :] = v_ref[c * tq:(c + 1) * tq, cols[g]].astype(F32).T.astype(BF16)
            vt_ref[g, c, HEAD_WIDTH:, :] = jnp.ones((ATTN_SUM_ROWS, tq), BF16)

    lane = lax.broadcasted_iota(jnp.int32, (tq, HEAD_WIDTH), 1)
    first_map = lane < ATTN_HEAD_DIM
    key = lax.broadcasted_iota(jnp.int32, (tq, 2 * tq), 0)
    qry = lax.broadcasted_iota(jnp.int32, (tq, 2 * tq), 1)
    q_chunk = jnp.where(qry >= tq, qry - tq, qry) // CHUNK
    visible = (key // CHUNK) <= q_chunk

    def q_block(i, _):
        q0 = pl.multiple_of(i * tq, tq)
        qqs = []
        for g in heads:
            q = q_ref[pl.ds(q0, tq), cols[g]]
            zero = jnp.zeros_like(q)
            qqs.append(jnp.concatenate([jnp.where(first_map, q, zero), jnp.where(first_map, zero, q)], axis=0))

        def step(j, carry, masked):
            k0 = pl.multiple_of(j * tq, tq)
            out = []
            scores = [lax.dot_general(k_ref[pl.ds(k0, tq), cols[g]], qqs[g], (((1,), (1,)), ((), ())),
                                      preferred_element_type=F32) for g in heads]
            for g in heads:
                m, acc = carry[g]
                s = scores[g]
                if masked:
                    s = jnp.where(visible, s, NEG_BIG)
                m_new = jnp.maximum(m, jnp.max(s, axis=0, keepdims=True))
                p = jnp.exp2(s - m_new)
                scale = jnp.exp2(m - m_new)
                pv = jnp.dot(vt_ref[g, j], p.astype(BF16), preferred_element_type=F32)
                out.append((m_new, scale * acc + pv))
            return tuple(out)

        init = tuple((jnp.full((1, 2 * tq), NEG_BIG, F32),
                      jnp.zeros((HEAD_WIDTH + ATTN_SUM_ROWS, 2 * tq), F32)) for _ in heads)
        carry = lax.fori_loop(0, i, lambda j, c: step(j, c, False), init)
        carry = step(i, carry, True)
        for g in heads:
            _, acc = carry[g]
            l = acc[HEAD_WIDTH:HEAD_WIDTH + 1, :]
            acc = acc[:HEAD_WIDTH, :]
            o = acc[:, :tq] / l[:, :tq] - lam * (acc[:, tq:] / l[:, tq:])
            o = o * lax.rsqrt(jnp.mean(o * o, axis=0, keepdims=True) + HEAD_NORM_EPS)
            o_ref[pl.ds(q0, tq), cols[g]] = (o.T * gain).astype(BF16)
        return 0

    lax.fori_loop(0, seq // tq, q_block, 0)


def _attention(q, k, v, lam_params, subln_g, lambda_init, batch, seq):
    t = q.shape[0]
    blk = pl.BlockSpec((seq, ATTN_HEADS_PER_STEP * HEAD_WIDTH), lambda b, h: (b, h))
    return pl.pallas_call(
        functools.partial(_attn_body, lambda_init=lambda_init, seq=seq),
        grid=(batch, ATTN_HEADS // ATTN_HEADS_PER_STEP),
        in_specs=[pl.BlockSpec((4, ATTN_HEAD_DIM), lambda b, h: (0, 0)),
                  pl.BlockSpec((1, HEAD_WIDTH), lambda b, h: (0, 0)),
                  blk, blk, blk],
        out_specs=blk,
        out_shape=jax.ShapeDtypeStruct((t, ATTN_WIDTH), BF16),
        scratch_shapes=[pltpu.VMEM((ATTN_HEADS_PER_STEP, seq // ATTN_TILE, HEAD_WIDTH + ATTN_SUM_ROWS, ATTN_TILE),
                                   BF16)],
        compiler_params=_params("arbitrary", "arbitrary"),
        name="diff_attention",
    )(lam_params, subln_g.reshape(1, HEAD_WIDTH), q, k, v)


def _mixer_body(r_ref, pw_ref, ps_ref, cw_ref, cb_ref, wa_ref, ba_ref, wi_ref, bi_ref, lam_ref, y_ref,
                a_ref, b_ref, h_ref, inv_ref, *, seq):
    rows = MIX_ROWS
    pool_hist = max(POOL_WINDOWS)
    conv_hist = SUBLANES
    pw = pw_ref[...]
    wa = wa_ref[...]
    wi = wi_ref[...]
    ps, cb, ba, bi = ps_ref[...], cb_ref[...], ba_ref[...], bi_ref[...]
    cw = cw_ref[...]
    neg_c_softplus = -LRU_C * jax.nn.softplus(-lam_ref[...])

    lane_e = lax.broadcasted_iota(jnp.int32, (rows + pool_hist, POOL_WIDTH), 1)
    lane = lax.broadcasted_iota(jnp.int32, (rows, POOL_WIDTH), 1)
    row = lax.broadcasted_iota(jnp.int32, (rows, POOL_WIDTH), 0)
    win = jnp.where(lane < 64, 2, jnp.where(lane < 128, 4, jnp.where(lane < 192, 8, 16)))
    inv_win = jnp.where(lane < 64, 1 / 2, jnp.where(lane < 128, 1 / 4, jnp.where(lane < 192, 1 / 8, 1 / 16)))
    inv_ref[...] = 1.0 / (row + 1).astype(F32)
    groups = rows // SUBLANES
    row_in_group = lax.broadcasted_iota(jnp.int32, (groups, SUBLANES, LRU_WIDTH), 1)
    group = lax.broadcasted_iota(jnp.int32, (groups, LRU_WIDTH), 0)

    def chunk(c, carry):
        tail_u, tail_x, h_prev = carry
        r0 = pl.multiple_of(c * rows, rows)
        u = r_ref[pl.ds(r0, rows), 0:POOL_WIDTH]
        xr = r_ref[pl.ds(r0, rows), POOL_WIDTH:POOL_WIDTH + LRU_WIDTH]
        xg = r_ref[pl.ds(r0, rows), POOL_WIDTH + LRU_WIDTH:REST_WIDTH]

        ue = jnp.concatenate([tail_u, u], axis=0)
        w2 = ue + pltpu.roll(ue, 1, 0)
        w4 = w2 + pltpu.roll(w2, 2, 0)
        w8 = w4 + pltpu.roll(w4, 4, 0)
        w16 = w8 + pltpu.roll(w8, 8, 0)
        ws = jnp.where(lane_e < 64, w2, jnp.where(lane_e < 128, w4, jnp.where(lane_e < 192, w8, w16)))
        ws = ws[pool_hist:]
        inv_count = jnp.where(r0 + row + 1 >= win, inv_win, inv_ref[...])
        pooled = ws * inv_count - u
        y_pool = jnp.dot(pooled.astype(BF16), pw, preferred_element_type=F32) * ps

        xe = jnp.concatenate([tail_x, xr], axis=0)
        xc = cb + pltpu.roll(xe, 3, 0)[conv_hist:] * cw[0:1]
        xc = xc + pltpu.roll(xe, 2, 0)[conv_hist:] * cw[1:2]
        xc = xc + pltpu.roll(xe, 1, 0)[conv_hist:] * cw[2:3]
        xc = xc + xr * cw[3:4]
        xcb = xc.astype(BF16)
        r_gate = jax.nn.sigmoid(jnp.dot(xcb, wa, preferred_element_type=F32) + ba)
        i_gate = jax.nn.sigmoid(jnp.dot(xcb, wi, preferred_element_type=F32) + bi)
        log_a = r_gate * neg_c_softplus
        a = jnp.exp(log_a)
        gap = -jnp.tanh(log_a) * (a * a + 1.0)
        b = (gap * lax.rsqrt(jnp.maximum(gap, F32_TINY))) * (i_gate * xc)

        def doubling(a, b, pos, length, axis):
            s = 1
            while s < length:
                keep = pos >= s
                a_prev = jnp.where(keep, pltpu.roll(a, s, axis), 1.0)
                b_prev = jnp.where(keep, pltpu.roll(b, s, axis), 0.0)
                b = a * b_prev + b
                a = a * a_prev
                s *= 2
            return a, b

        a, b = doubling(a.reshape(groups, SUBLANES, LRU_WIDTH), b.reshape(groups, SUBLANES, LRU_WIDTH),
                        row_in_group, SUBLANES, 1)
        a = a.reshape(rows, LRU_WIDTH)
        b = b.reshape(rows, LRU_WIDTH)
        halves = range(LRU_WIDTH // LANES)
        ends = pl.ds(SUBLANES - 1, groups, stride=SUBLANES)
        for k in halves:
            a_ref[k] = a[:, k * LANES:(k + 1) * LANES]
            b_ref[k] = b[:, k * LANES:(k + 1) * LANES]
        a_end = jnp.concatenate([a_ref[k, ends, :] for k in halves], axis=-1)
        b_end = jnp.concatenate([b_ref[k, ends, :] for k in halves], axis=-1)
        a_end, b_end = doubling(a_end, b_end, group, groups, 0)
        h_end = a_end * h_prev + b_end
        h_ref[...] = jnp.where(group == 0, h_prev, pltpu.roll(h_end, 1, 0))
        h_start = jnp.concatenate([jnp.broadcast_to(h_ref[g:g + 1, :], (SUBLANES, LRU_WIDTH))
                                   for g in range(groups)], axis=0)
        h = a * h_start + b
        y_lru = h * jax.nn.gelu(xg)

        y_ref[pl.ds(r0, rows), 0:POOL_WIDTH] = y_pool.astype(BF16)
        y_ref[pl.ds(r0, rows), POOL_WIDTH:MIX_WIDTH] = y_lru.astype(BF16)
        return u[rows - pool_hist:], xr[rows - conv_hist:], h_end[groups - 1:groups]

    init = (jnp.zeros((pool_hist, POOL_WIDTH), F32), jnp.zeros((conv_hist, LRU_WIDTH), F32),
            jnp.zeros((1, LRU_WIDTH), F32))
    lax.fori_loop(0, seq // rows, chunk, init)


def _block_diag(w):
    g, c, d = w.shape
    eye = jnp.eye(g, dtype=w.dtype)
    return (eye[:, None, :, None] * w[:, :, None, :]).reshape(g * c, g * d)


def _mixer(rest, pool_w, pool_scale, conv_w, conv_b, wa, ba, wi, bi, lru_lambda, batch, seq):
    t = rest.shape[0]
    full = lambda shape: pl.BlockSpec(shape, lambda b: (0, 0))
    vec = lambda a: a.reshape(1, -1)
    return pl.pallas_call(
        functools.partial(_mixer_body, seq=seq),
        grid=(batch,),
        in_specs=[pl.BlockSpec((seq, REST_WIDTH), lambda b: (b, 0)),
                  full((POOL_WIDTH, POOL_WIDTH)), full((1, POOL_WIDTH)),
                  full((CONV_WIDTH, LRU_WIDTH)), full((1, LRU_WIDTH)),
                  full((LRU_WIDTH, LRU_WIDTH)), full((1, LRU_WIDTH)),
                  full((LRU_WIDTH, LRU_WIDTH)), full((1, LRU_WIDTH)),
                  full((1, LRU_WIDTH))],
        out_specs=pl.BlockSpec((seq, MIX_WIDTH), lambda b: (b, 0)),
        out_shape=jax.ShapeDtypeStruct((t, MIX_WIDTH), BF16),
        scratch_shapes=[pltpu.VMEM((LRU_WIDTH // LANES, MIX_ROWS, LANES), F32),
                        pltpu.VMEM((LRU_WIDTH // LANES, MIX_ROWS, LANES), F32),
                        pltpu.VMEM((MIX_ROWS // SUBLANES, LRU_WIDTH), F32),
                        pltpu.VMEM((MIX_ROWS, POOL_WIDTH), F32)],
        compiler_params=_params("arbitrary"),
        name="pool_lru_mixer",
    )(rest, _block_diag(pool_w).astype(BF16), vec(pool_scale), conv_w, vec(conv_b),
      _block_diag(wa).astype(BF16), vec(ba), _block_diag(wi).astype(BF16), vec(bi), vec(lru_lambda))


def _outproj_body(ya_ref, ym_ref, x_ref, w_ref, g_ref, b_ref, *rest, alpha, route):
    if route:
        wr_ref, o_ref, ot_ref, meta_ref, gate_ref, cnt_ref, wbf_ref, run_ref = rest
    else:
        o_ref, wbf_ref = rest

    @pl.when(pl.program_id(0) == 0)
    def _cast_weights():
        for c in range(0, D_MODEL, 256):
            wbf_ref[:, c:c + 256] = w_ref[0, :, c:c + 256].astype(BF16)

    mix = jnp.dot(ya_ref[...], wbf_ref[0:ATTN_WIDTH, :], preferred_element_type=F32)
    mix = mix + jnp.dot(ym_ref[...], wbf_ref[ATTN_WIDTH:, :], preferred_element_type=F32)
    x1 = _layer_norm(alpha * x_ref[...] + mix, g_ref[...], b_ref[...])
    o_ref[...] = x1
    if route:
        _store_row_tiles(ot_ref, x1)
        _route(x1, wr_ref, meta_ref, gate_ref, cnt_ref, run_ref)


def _outproj(y_attn, y_mix, x, w, layer, g, b, alpha, w_router=None):
    t = x.shape[0]
    route = w_router is not None
    row = lambda i: (i, 0)
    const = lambda i: (0, 0)
    in_specs = [pl.BlockSpec((ROW_TILE, ATTN_WIDTH), row), pl.BlockSpec((ROW_TILE, MIX_WIDTH), row),
                pl.BlockSpec((ROW_TILE, D_MODEL), row),
                pl.BlockSpec((1, D_MODEL, D_MODEL), lambda i: (layer, 0, 0)),
                pl.BlockSpec((1, D_MODEL), const), pl.BlockSpec((1, D_MODEL), const)]
    args = [y_attn, y_mix, x, w, g.reshape(1, -1), b.reshape(1, -1)]
    out_specs = [pl.BlockSpec((ROW_TILE, D_MODEL), row)]
    out_shape = [jax.ShapeDtypeStruct((t, D_MODEL), F32)]
    scratch = [pltpu.VMEM((D_MODEL, D_MODEL), BF16)]
    if route:
        in_specs.append(pl.BlockSpec((N_EXPERTS, D_MODEL), const))
        args.append(w_router.T)
        out_specs += [pl.BlockSpec((ROW_TILE * SUBLANES, LANES), row),
                      pl.BlockSpec((ROW_TILE, ROUTE_META), row), pl.BlockSpec((ROW_TILE, LANES), row),
                      pl.BlockSpec((SUBLANES, LANES), const)]
        out_shape += [jax.ShapeDtypeStruct((t * SUBLANES, LANES), F32),
                      jax.ShapeDtypeStruct((t, ROUTE_META), jnp.int32), jax.ShapeDtypeStruct((t, LANES), F32),
                      jax.ShapeDtypeStruct((SUBLANES, LANES), F32)]
        scratch.append(pltpu.VMEM((1, LANES), F32))
    return pl.pallas_call(
        functools.partial(_outproj_body, alpha=alpha, route=route),
        grid=(t // ROW_TILE,),
        in_specs=in_specs,
        out_specs=out_specs,
        out_shape=out_shape,
        scratch_shapes=scratch,
        compiler_params=_params("arbitrary"),
        name="outproj_ln_route" if route else "outproj_ln",
    )(*args)


def _dense_ffn_body(x_ref, wg_hbm, wu_hbm, wd_hbm, g_ref, b_ref, o_ref,
                    wg_ref, wu_ref, wd_ref, stage_in_ref, stage_out_ref, xb_ref, acc_ref, sem, *, layer, alpha):
    n_chunks = 3 * FFN_STEPS

    def chunk_copy(k):
        which, c = divmod(k, FFN_STEPS)
        cols = pl.ds(c * FFN_COLS, FFN_COLS)
        if which < 2:
            src, dst = (wg_hbm, wu_hbm)[which].at[layer, :, cols], stage_in_ref.at[k % 2]
        else:
            src, dst = wd_hbm.at[layer, cols, :], stage_out_ref.at[k % 2]
        return pltpu.make_async_copy(src, dst, sem.at[k % 2])

    @pl.when(pl.program_id(0) == 0)
    def _load_weights():
        chunk_copy(0).start()
        for k in range(n_chunks):
            if k + 1 < n_chunks:
                chunk_copy(k + 1).start()
            chunk_copy(k).wait()
            which, c = divmod(k, FFN_STEPS)
            if which == 0:
                wg_ref[c] = stage_in_ref[k % 2].astype(BF16)
            elif which == 1:
                wu_ref[c] = stage_in_ref[k % 2].astype(BF16)
            else:
                wd_ref[c] = stage_out_ref[k % 2].astype(BF16)

    xb_ref[...] = x_ref[...].astype(BF16)
    acc_ref[...] = jnp.zeros_like(acc_ref)

    def tile(c, _):
        xb = xb_ref[...]
        gate = jnp.dot(xb, wg_ref[c], preferred_element_type=F32)
        up = jnp.dot(xb, wu_ref[c], preferred_element_type=F32)
        hidden = (jax.nn.silu(gate) * up).astype(BF16)
        acc_ref[...] += jnp.dot(hidden, wd_ref[c], preferred_element_type=F32)
        return 0
    lax.fori_loop(0, FFN_STEPS, tile, 0)
    o_ref[...] = _layer_norm(alpha * x_ref[...] + acc_ref[...], g_ref[...], b_ref[...])


def _dense_ffn(x, w_gate, w_up, w_down, layer, g, b, alpha):
    t = x.shape[0]
    row = lambda i: (i, 0)
    const = lambda i: (0, 0)
    any_spec = pl.BlockSpec(memory_space=pl.ANY)
    return pl.pallas_call(
        functools.partial(_dense_ffn_body, layer=layer, alpha=alpha),
        grid=(t // FFN_ROWS,),
        in_specs=[pl.BlockSpec((FFN_ROWS, D_MODEL), row), any_spec, any_spec, any_spec,
                  pl.BlockSpec((1, D_MODEL), const), pl.BlockSpec((1, D_MODEL), const)],
        out_specs=pl.BlockSpec((FFN_ROWS, D_MODEL), row),
        out_shape=jax.ShapeDtypeStruct((t, D_MODEL), F32),
        scratch_shapes=[pltpu.VMEM((FFN_STEPS, D_MODEL, FFN_COLS), BF16),
                        pltpu.VMEM((FFN_STEPS, D_MODEL, FFN_COLS), BF16),
                        pltpu.VMEM((FFN_STEPS, FFN_COLS, D_MODEL), BF16),
                        pltpu.VMEM((2, D_MODEL, FFN_COLS), F32), pltpu.VMEM((2, FFN_COLS, D_MODEL), F32),
                        pltpu.VMEM((FFN_ROWS, D_MODEL), BF16), pltpu.VMEM((FFN_ROWS, D_MODEL), F32),
                        pltpu.SemaphoreType.DMA((2,))],
        compiler_params=_params("arbitrary"),
        name="swiglu_ln",
    )(x, w_gate, w_up, w_down, g.reshape(1, -1), b.reshape(1, -1))


def _store_row_tiles(o_ref, val):
    rows = val.shape[0]
    for s in range(SUBLANES):
        o_ref[pl.ds(s, rows, stride=SUBLANES), :] = val[:, s * LANES:(s + 1) * LANES]


def _load_row_tiles(ref, first_row, rows):
    return [ref[pl.ds(first_row * SUBLANES + s, rows, stride=SUBLANES), :] for s in range(SUBLANES)]


def _tile_window(row):
    return pl.ds(pl.multiple_of(row * SUBLANES, SUBLANES), SUBLANES)


def _start_row_copies(n_rows, make_copy, priorities):
    k = len(priorities)

    def body(q, _):
        for p, priority in enumerate(priorities):
            make_copy(q * k + p).start(priority=priority)
        return 0
    lax.fori_loop(0, n_rows // k, body, 0, unroll=DMA_ISSUE_UNROLL // k)


def _gather_copies(src_ref, x_hbm, buf_ref, sem, slot):
    def copy(r):
        return pltpu.make_async_copy(x_hbm.at[_tile_window(src_ref[0, 0, r])], buf_ref.at[slot, _tile_window(r)],
                                     sem.at[slot])
    return copy


def _expert_ffn_body(be_ref, nu_ref, src0_ref, src_next_ref, x_hbm, wg_ref, wu_ref, wd_ref, o_ref,
                     xb_ref, acc_ref, buf_ref, sem):
    i = pl.program_id(0)
    j = pl.program_id(1)
    n_used = nu_ref[0]
    used = i < n_used

    slot = lax.rem(i, 2)
    priority = 1

    def wait_slot(which):
        pltpu.make_async_copy(buf_ref.at[which], buf_ref.at[which], sem.at[which]).wait()

    @pl.when(jnp.logical_and(used, jnp.logical_and(i == 0, j == 0)))
    def _first_rows():
        _start_row_copies(FFN_GATHER_ROWS, _gather_copies(src0_ref, x_hbm, buf_ref, sem, 0),
                          priorities=(priority,))

    @pl.when(jnp.logical_and(used, j == 0))
    def _start():
        wait_slot(slot)
        for s, part in enumerate(_load_row_tiles(buf_ref.at[slot], 0, FFN_ROWS)):
            xb_ref[:, s * LANES:(s + 1) * LANES] = part.astype(BF16)
        acc_ref[...] = jnp.zeros_like(acc_ref)

    @pl.when(used)
    def _accumulate():
        copy = _gather_copies(src_next_ref, x_hbm, buf_ref, sem, 1 - slot)
        for r in range(FFN_GATHER_ROWS_PER_STEP):
            copy(j * FFN_GATHER_ROWS_PER_STEP + r).start(priority=priority)
        xb = xb_ref[...]
        gate = jnp.dot(xb, wg_ref[0].astype(BF16), preferred_element_type=F32)
        up = jnp.dot(xb, wu_ref[0].astype(BF16), preferred_element_type=F32)
        hidden = (jax.nn.silu(gate) * up).astype(BF16)
        acc_ref[...] += jnp.dot(hidden, wd_ref[0].astype(BF16), preferred_element_type=F32)

    @pl.when(jnp.logical_and(used, j == pl.num_programs(1) - 1))
    def _finish():
        @pl.when(i == n_used - 1)
        def _drain():
            wait_slot(1 - slot)
        _store_row_tiles(o_ref, acc_ref[...])

    @pl.when(jnp.logical_and(jnp.logical_not(used), j == 0))
    def _empty():
        o_ref[...] = jnp.zeros_like(o_ref)


def _expert_ffn(block_e, n_used, x_tiles, w_gate, w_up, w_down, src):
    n_blocks = block_e.shape[0]

    def col(i, j, be, nu):
        return jnp.where(i < nu[0], j, FFN_STEPS - 1)

    def expert(i, be, nu):
        return be[jnp.minimum(i, nu[0] - 1)]

    row = lambda i, j, be, nu: (i, 0)
    dummy = jnp.broadcast_to(jnp.arange(FFN_GATHER_ROWS - FFN_ROWS, dtype=jnp.int32),
                             (n_blocks, 1, FFN_GATHER_ROWS - FFN_ROWS))
    src3 = jnp.concatenate([src.reshape(n_blocks, 1, FFN_ROWS), dummy], axis=-1)
    in_specs = [pl.BlockSpec((1, 1, FFN_GATHER_ROWS), lambda i, j, be, nu: (0, 0, 0), memory_space=pltpu.SMEM),
                pl.BlockSpec((1, 1, FFN_GATHER_ROWS), lambda i, j, be, nu: (jnp.minimum(i + 1, n_blocks - 1), 0, 0),
                             memory_space=pltpu.SMEM),
                pl.BlockSpec(memory_space=pl.ANY),
                pl.BlockSpec((1, D_MODEL, FFN_COLS), lambda i, j, be, nu: (expert(i, be, nu), 0, col(i, j, be, nu))),
                pl.BlockSpec((1, D_MODEL, FFN_COLS), lambda i, j, be, nu: (expert(i, be, nu), 0, col(i, j, be, nu))),
                pl.BlockSpec((1, FFN_COLS, D_MODEL), lambda i, j, be, nu: (expert(i, be, nu), col(i, j, be, nu), 0))]
    return pl.pallas_call(
        _expert_ffn_body,
        grid_spec=pltpu.PrefetchScalarGridSpec(
            num_scalar_prefetch=2,
            grid=(n_blocks, FFN_STEPS),
            in_specs=in_specs,
            out_specs=pl.BlockSpec((FFN_ROWS * SUBLANES, LANES), row),
            scratch_shapes=[pltpu.VMEM((FFN_ROWS, D_MODEL), BF16), pltpu.VMEM((FFN_ROWS, D_MODEL), F32),
                            pltpu.VMEM((2, FFN_GATHER_ROWS * SUBLANES, LANES), F32),
                            pltpu.SemaphoreType.DMA((2,))]),
        out_shape=jax.ShapeDtypeStruct((n_blocks * FFN_ROWS * SUBLANES, LANES), F32),
        compiler_params=_params("arbitrary", "arbitrary"),
        name="swiglu_experts",
    )(block_e, n_used, src3, src3, x_tiles, w_gate, w_up, w_down)


def _route(x, w_ref, meta_ref, gate_ref, cnt_ref, run_ref):
    tm = x.shape[0]

    @pl.when(pl.program_id(0) == 0)
    def _init():
        run_ref[...] = jnp.zeros_like(run_ref)

    lane_i = lax.broadcasted_iota(jnp.int32, (tm, LANES), 1)
    lane = lane_i.astype(F32)
    logits = jnp.full((tm, LANES), -jnp.inf, F32)
    for e in range(N_EXPERTS):
        logit_e = jnp.sum(x * w_ref[e:e + 1, :], axis=-1, keepdims=True)
        logits = jnp.where(lane_i == e, logit_e, logits)
    m1 = jnp.max(logits, axis=-1, keepdims=True)
    e1 = jnp.min(jnp.where(logits == m1, lane, float(LANES)), axis=-1, keepdims=True)
    rest = jnp.where(lane == e1, -jnp.inf, logits)
    m2 = jnp.max(rest, axis=-1, keepdims=True)
    e2 = jnp.min(jnp.where(rest == m2, lane, float(LANES)), axis=-1, keepdims=True)
    ex = jnp.exp(m2 - m1)
    g1 = 1.0 / (1.0 + ex)
    g2 = ex / (1.0 + ex)

    chosen = jnp.logical_or(lane == e1, lane == e2)
    r_i = lax.broadcasted_iota(jnp.int32, (tm, tm), 0)
    c_i = lax.broadcasted_iota(jnp.int32, (tm, tm), 1)
    earlier = (c_i < r_i).astype(BF16)
    before = jnp.dot(earlier, chosen.astype(BF16), preferred_element_type=F32) + run_ref[...]
    rank1 = jnp.sum(jnp.where(lane == e1, before, 0.0), axis=-1, keepdims=True)
    rank2 = jnp.sum(jnp.where(lane == e2, before, 0.0), axis=-1, keepdims=True)
    total = run_ref[...] + jnp.sum(chosen.astype(F32), axis=0, keepdims=True)
    run_ref[...] = total
    cnt_ref[...] = jnp.broadcast_to(total, cnt_ref.shape)
    meta = jnp.where(lane_i == 0, e1, jnp.where(lane_i == 1, e2, jnp.where(lane_i == 2, rank1, rank2)))
    meta_ref[...] = meta[:, :ROUTE_META].astype(jnp.int32)
    gate_ref[...] = jnp.where(lane_i == 0, g1, g2)


def _combine_body(d0_ref, dn_ref, ys_hbm, gate_ref, x_ref, g_ref, b_ref, o_ref, buf_ref, sem, *, alpha):
    i = pl.program_id(0)
    slot = lax.rem(i, 2)
    n_rows = buf_ref.shape[1] // SUBLANES

    def row_copy(d_ref, to_slot):
        def copy(r):
            return pltpu.make_async_copy(ys_hbm.at[_tile_window(d_ref[0, 0, r])],
                                         buf_ref.at[to_slot, _tile_window(r)], sem.at[to_slot])
        return copy

    def wait_slot(which):
        pltpu.make_async_copy(buf_ref.at[which], buf_ref.at[which], sem.at[which]).wait()

    @pl.when(i == 0)
    def _first_rows():
        _start_row_copies(n_rows, row_copy(d0_ref, 0), priorities=(0, 1))

    wait_slot(slot)
    next_copy = row_copy(dn_ref, 1 - slot)
    for r in range(n_rows):
        next_copy(r).start(priority=r % 2)

    gates = gate_ref[...]
    g1, g2 = gates[:, 0:1], gates[:, 1:2]
    first = _load_row_tiles(buf_ref.at[slot], 0, MOVE_ROWS)
    second = _load_row_tiles(buf_ref.at[slot], MOVE_ROWS, MOVE_ROWS)
    f = jnp.concatenate([g1 * a + g2 * b for a, b in zip(first, second)], axis=-1)
    o_ref[...] = _layer_norm(alpha * x_ref[...] + f, g_ref[...], b_ref[...])

    @pl.when(i == pl.num_programs(0) - 1)
    def _drain():
        wait_slot(1 - slot)


def _combine(ys, dest1, dest2, gates, x, g, b, alpha):
    t = x.shape[0]
    steps = t // MOVE_ROWS
    dest = jnp.concatenate([dest1.reshape(steps, 1, MOVE_ROWS), dest2.reshape(steps, 1, MOVE_ROWS)], axis=-1)
    idx_shape = (1, 1, TOP_K * MOVE_ROWS)
    row = lambda i: (i, 0)
    const = lambda i: (0, 0)
    return pl.pallas_call(
        functools.partial(_combine_body, alpha=alpha),
        grid=(steps,),
        in_specs=[pl.BlockSpec(idx_shape, lambda i: (0, 0, 0), memory_space=pltpu.SMEM),
                  pl.BlockSpec(idx_shape, lambda i: (jnp.minimum(i + 1, steps - 1), 0, 0), memory_space=pltpu.SMEM),
                  pl.BlockSpec(memory_space=pl.ANY),
                  pl.BlockSpec((MOVE_ROWS, LANES), row), pl.BlockSpec((MOVE_ROWS, D_MODEL), row),
                  pl.BlockSpec((1, D_MODEL), const), pl.BlockSpec((1, D_MODEL), const)],
        out_specs=pl.BlockSpec((MOVE_ROWS, D_MODEL), row),
        out_shape=jax.ShapeDtypeStruct((t, D_MODEL), F32),
        scratch_shapes=[pltpu.VMEM((2, TOP_K * MOVE_ROWS * SUBLANES, LANES), F32), pltpu.SemaphoreType.DMA((2,))],
        compiler_params=_params("arbitrary"),
        name="combine_ln",
    )(dest, dest, ys, gates, x, g.reshape(1, -1), b.reshape(1, -1))


def _invert_body(d1_ref, d2_ref, fill_hbm, src_ref, sem):
    fill = pltpu.make_async_copy(fill_hbm, src_ref, sem)
    fill.start()
    fill.wait()

    def place(t, _):
        src_ref[d1_ref[t]] = t
        src_ref[d2_ref[t]] = t
        return 0
    lax.fori_loop(0, d1_ref.shape[0], place, 0, unroll=SCALAR_LOOP_UNROLL)


def _invert_placement(dest1, dest2, n_rows):
    smem = pl.BlockSpec(memory_space=pltpu.SMEM)
    return pl.pallas_call(
        _invert_body,
        in_specs=[smem, smem, pl.BlockSpec(memory_space=pl.ANY)],
        out_specs=smem,
        out_shape=jax.ShapeDtypeStruct((n_rows,), jnp.int32),
        scratch_shapes=[pltpu.SemaphoreType.DMA(())],
        name="invert_placement",
    )(dest1, dest2, jnp.arange(n_rows, dtype=jnp.int32) % dest1.shape[0])


def _moe(x1, x1_tiles, meta, gates, counts, w_gate, w_up, w_down, first_expert, g, b, alpha):
    t = x1.shape[0]
    e1, e2, rank1, rank2 = meta[:, 0], meta[:, 1], meta[:, 2], meta[:, 3]
    sizes = counts[0, :N_EXPERTS].astype(jnp.int32)
    padded = (sizes + FFN_ROWS - 1) // FFN_ROWS * FFN_ROWS
    group_end = jnp.cumsum(padded)
    group_start = group_end - padded
    dest1 = group_start[e1] + rank1
    dest2 = group_start[e2] + rank2
    n_blocks = (t * TOP_K) // FFN_ROWS + N_EXPERTS
    block_start = jnp.arange(n_blocks, dtype=jnp.int32) * FFN_ROWS
    block_e = jnp.minimum(jnp.sum(group_end[None, :] <= block_start[:, None], axis=1), N_EXPERTS - 1)
    n_used = (group_end[-1] // FFN_ROWS).reshape(1)
    src = _invert_placement(dest1, dest2, n_blocks * FFN_ROWS)
    ys = _expert_ffn((block_e + first_expert).astype(jnp.int32), n_used.astype(jnp.int32), x1_tiles, w_gate,
                     w_up, w_down, src)
    return _combine(ys, dest1, dest2, gates, x1, g, b, alpha)


def kernel(x, w_in, w_out, attn_lambda, attn_subln_g, pool_w, pool_scale, conv_w, conv_b, lru_wa, lru_ba,
           lru_wi, lru_bi, lru_lambda, ln1_g, ln1_b, ln2_g, ln2_b, ffn_w_gate, ffn_w_up, ffn_w_down,
           router_w, moe_w_gate, moe_w_up, moe_w_down):
    batch, seq, d = x.shape
    depth = w_in.shape[0]
    assert d == D_MODEL and seq % ATTN_TILE == 0 and seq % MIX_ROWS == 0
    t = batch * seq
    assert t % FFN_ROWS == 0 and t % ROW_TILE == 0
    alpha = (2.0 * depth) ** 0.25
    moe_gate = moe_w_gate.reshape(-1, D_MODEL, D_FF)
    moe_up = moe_w_up.reshape(-1, D_MODEL, D_FF)
    moe_down = moe_w_down.reshape(-1, D_FF, D_MODEL)
    xt = x.reshape(t, d)
    for l in range(depth):
        lambda_init = 0.8 - 0.6 * math.exp(-0.3 * l)
        q, k, v, rest = _inproj(xt, w_in, l)
        y_attn = _attention(q, k, v, attn_lambda[l], attn_subln_g[l], lambda_init, batch, seq)
        y_mix = _mixer(rest, pool_w[l], pool_scale[l], conv_w[l], conv_b[l], lru_wa[l], lru_ba[l],
                       lru_wi[l], lru_bi[l], lru_lambda[l], batch, seq)
        if l % 2 == 0:
            x1, = _outproj(y_attn, y_mix, xt, w_out, l, ln1_g[l], ln1_b[l], alpha)
            xt = _dense_ffn(x1, ffn_w_gate, ffn_w_up, ffn_w_down, l // 2, ln2_g[l], ln2_b[l], alpha)
        else:
            x1, x1_tiles, meta, gates, counts = _outproj(y_attn, y_mix, xt, w_out, l, ln1_g[l], ln1_b[l], alpha,
                                                         w_router=router_w[l // 2])
            xt = _moe(x1, x1_tiles, meta, gates, counts, moe_gate, moe_up, moe_down, (l // 2) * N_EXPERTS,
                      ln2_g[l], ln2_b[l], alpha)
    return xt.reshape(batch, seq, d)
```

```python
import functools
import math

import jax
import jax.numpy as jnp
from jax import lax
from jax.experimental import pallas as pl
from jax.experimental.pallas import tpu as pltpu

F32 = jnp.float32
BF16 = jnp.bfloat16

D_MODEL = 1024
CHUNK = 64
ATTN_HEADS = 4
ATTN_WIDTH = 512
ATTN_HEAD_DIM = 64
HEAD_WIDTH = 2 * ATTN_HEAD_DIM
POOL_WINDOWS = (2, 4, 8, 16)
POOL_WIDTH = 256
POOL_GROUP_DIM = 64
LRU_WIDTH = 256
LRU_C = 8.0
CONV_WIDTH = 4
REST_WIDTH = POOL_WIDTH + 2 * LRU_WIDTH
IN_WIDTH = 3 * ATTN_WIDTH + REST_WIDTH
MIX_WIDTH = POOL_WIDTH + LRU_WIDTH
D_FF = 2816
N_EXPERTS = 8
TOP_K = 2
LN_EPS = 1e-5
HEAD_NORM_EPS = 1e-5

LANES = 128
SUBLANES = 8
VMEM_LIMIT_BYTES = 56 * 1024 * 1024

ROW_TILE = 512
ATTN_TILE = 256
ATTN_HEADS_PER_STEP = 4
ATTN_SUM_ROWS = 16
ATTN_SCORE_SCALE = ATTN_HEAD_DIM ** -0.5 * math.log2(math.e)
MIX_ROWS = 256
FFN_ROWS = 1024
FFN_COLS = 256
FFN_STEPS = D_FF // FFN_COLS
FFN_GATHER_ROWS_PER_STEP = -(-FFN_ROWS // ((FFN_STEPS - 1) * SUBLANES)) * SUBLANES
FFN_GATHER_ROWS = FFN_GATHER_ROWS_PER_STEP * FFN_STEPS
MOVE_ROWS = 256
ROUTE_META = 8
SCALAR_LOOP_UNROLL = 8
DMA_ISSUE_UNROLL = 8
NEG_BIG = -1e30
F32_TINY = float(jnp.finfo(jnp.float32).tiny)
assert MIX_ROWS >= max(POOL_WINDOWS)


def _params(*semantics):
    return pltpu.CompilerParams(dimension_semantics=semantics, vmem_limit_bytes=VMEM_LIMIT_BYTES)


def _layer_norm(z, g, b):
    mu = jnp.mean(z, axis=-1, keepdims=True)
    zc = z - mu
    var = jnp.mean(zc * zc, axis=-1, keepdims=True)
    return zc * lax.rsqrt(var + LN_EPS) * g + b


def _inproj_body(x_ref, w_ref, q_ref, k_ref, v_ref, r_ref, wbf_ref):
    @pl.when(pl.program_id(0) == 0)
    def _cast_weights():
        for c in range(0, IN_WIDTH, 256):
            wbf_ref[:, c:c + 256] = w_ref[0, :, c:c + 256].astype(BF16)

    xb = x_ref[...].astype(BF16)

    def proj(c0, c1):
        return jnp.dot(xb, wbf_ref[:, c0:c1], preferred_element_type=F32)

    q_ref[...] = (proj(0, ATTN_WIDTH) * ATTN_SCORE_SCALE).astype(BF16)
    k_ref[...] = proj(ATTN_WIDTH, 2 * ATTN_WIDTH).astype(BF16)
    v_ref[...] = proj(2 * ATTN_WIDTH, 3 * ATTN_WIDTH).astype(BF16)
    r_ref[...] = proj(3 * ATTN_WIDTH, IN_WIDTH)


def _inproj(x, w, layer):
    t = x.shape[0]
    row = lambda i: (i, 0)
    return pl.pallas_call(
        _inproj_body,
        grid=(t // ROW_TILE,),
        in_specs=[pl.BlockSpec((ROW_TILE, D_MODEL), row),
                  pl.BlockSpec((1, D_MODEL, IN_WIDTH), lambda i: (layer, 0, 0))],
        out_specs=[pl.BlockSpec((ROW_TILE, ATTN_WIDTH), row)] * 3
        + [pl.BlockSpec((ROW_TILE, REST_WIDTH), row)],
        out_shape=[jax.ShapeDtypeStruct((t, ATTN_WIDTH), BF16)] * 3
        + [jax.ShapeDtypeStruct((t, REST_WIDTH), F32)],
        scratch_shapes=[pltpu.VMEM((D_MODEL, IN_WIDTH), BF16)],
        compiler_params=_params("arbitrary"),
        name="inproj",
    )(x, w)


def _attn_body(lam_ref, g_ref, q_ref, k_ref, v_ref, o_ref, vt_ref, *, lambda_init, seq):
    tq = ATTN_TILE
    lv = lam_ref[...]
    lam = (jnp.exp(jnp.sum(lv[0:1] * lv[1:2], axis=-1, keepdims=True))
           - jnp.exp(jnp.sum(lv[2:3] * lv[3:4], axis=-1, keepdims=True)) + lambda_init)
    gain = g_ref[...] * (1.0 - lambda_init)

    heads = range(ATTN_HEADS_PER_STEP)
    cols = [slice(g * HEAD_WIDTH, (g + 1) * HEAD_WIDTH) for g in heads]
    for g in heads:
        for c in range(seq // tq):
            vt_ref[g, c, 0:HEAD_WIDTH, :] = v_ref[c * tq:(c + 1) * tq, cols[g]].astype(F32).T.astype(BF16)
            vt_ref[g, c, HEAD_WIDTH:, :] = jnp.ones((ATTN_SUM_ROWS, tq), BF16)

    lane = lax.broadcasted_iota(jnp.int32, (tq, HEAD_WIDTH), 1)
    first_map = lane < ATTN_HEAD_DIM
    key = lax.broadcasted_iota(jnp.int32, (tq, 2 * tq), 0)
    qry = lax.broadcasted_iota(jnp.int32, (tq, 2 * tq), 1)
    q_chunk = jnp.where(qry >= tq, qry - tq, qry) // CHUNK
    visible = (key // CHUNK) <= q_chunk

    def q_block(i, _):
        q0 = pl.multiple_of(i * tq, tq)
        qqs = []
        for g in heads:
            q = q_ref[pl.ds(q0, tq), cols[g]]
            zero = jnp.zeros_like(q)
            qqs.append(jnp.concatenate([jnp.where(first_map, q, zero), jnp.where(first_map, zero, q)], axis=0))

        def step(j, carry, masked):
            k0 = pl.multiple_of(j * tq, tq)
            out = []
            scores = [lax.dot_general(k_ref[pl.ds(k0, tq), cols[g]], qqs[g], (((1,), (1,)), ((), ())),
                                      preferred_element_type=F32) for g in heads]
            for g in heads:
                m, acc = carry[g]
                s = scores[g]
                if masked:
                    s = jnp.where(visible, s, NEG_BIG)
                m_new = jnp.maximum(m, jnp.max(s, axis=0, keepdims=True))
                p = jnp.exp2(s - m_new)
                scale = jnp.exp2(m - m_new)
                pv = jnp.dot(vt_ref[g, j], p.astype(BF16), preferred_element_type=F32)
                out.append((m_new, scale * acc + pv))
            return tuple(out)

        init = tuple((jnp.full((1, 2 * tq), NEG_BIG, F32),
                      jnp.zeros((HEAD_WIDTH + ATTN_SUM_ROWS, 2 * tq), F32)) for _ in heads)
        carry = lax.fori_loop(0, i, lambda j, c: step(j, c, False), init)
        carry = step(i, carry, True)
        for g in heads:
            _, acc = carry[g]
            l = acc[HEAD_WIDTH:HEAD_WIDTH + 1, :]
            acc = acc[:HEAD_WIDTH, :]
            o = acc[:, :tq] / l[:, :tq] - lam * (acc[:, tq:] / l[:, tq:])
            o = o * lax.rsqrt(jnp.mean(o * o, axis=0, keepdims=True) + HEAD_NORM_EPS)
            o_ref[pl.ds(q0, tq), cols[g]] = (o.T * gain).astype(BF16)
        return 0

    lax.fori_loop(0, seq // tq, q_block, 0)


def _attention(q, k, v, lam_params, subln_g, lambda_init, batch, seq):
    t = q.shape[0]
    blk = pl.BlockSpec((seq, ATTN_HEADS_PER_STEP * HEAD_WIDTH), lambda b, h: (b, h))
    return pl.pallas_call(
        functools.partial(_attn_body, lambda_init=lambda_init, seq=seq),
        grid=(batch, ATTN_HEADS // ATTN_HEADS_PER_STEP),
        in_specs=[pl.BlockSpec((4, ATTN_HEAD_DIM), lambda b, h: (0, 0)),
                  pl.BlockSpec((1, HEAD_WIDTH), lambda b, h: (0, 0)),
                  blk, blk, blk],
        out_specs=blk,
        out_shape=jax.ShapeDtypeStruct((t, ATTN_WIDTH), BF16),
        scratch_shapes=[pltpu.VMEM((ATTN_HEADS_PER_STEP, seq // ATTN_TILE, HEAD_WIDTH + ATTN_SUM_ROWS, ATTN_TILE),
                                   BF16)],
        compiler_params=_params("arbitrary", "arbitrary"),
        name="diff_attention",
    )(lam_params, subln_g.reshape(1, HEAD_WIDTH), q, k, v)


def _mixer_body(r_ref, pw_ref, ps_ref, cw_ref, cb_ref, wa_ref, ba_ref, wi_ref, bi_ref, lam_ref, y_ref,
                a_ref, b_ref, h_ref, inv_ref, *, seq):
    rows = MIX_ROWS
    pool_hist = max(POOL_WINDOWS)
    conv_hist = SUBLANES
    pw = pw_ref[...]
    wa = wa_ref[...]
    wi = wi_ref[...]
    ps, cb, ba, bi = ps_ref[...], cb_ref[...], ba_ref[...], bi_ref[...]
    cw = cw_ref[...]
    neg_c_softplus = -LRU_C * jax.nn.softplus(-lam_ref[...])

    lane_e = lax.broadcasted_iota(jnp.int32, (rows + pool_hist, POOL_WIDTH), 1)
    lane = lax.broadcasted_iota(jnp.int32, (rows, POOL_WIDTH), 1)
    row = lax.broadcasted_iota(jnp.int32, (rows, POOL_WIDTH), 0)
    win = jnp.where(lane < 64, 2, jnp.where(lane < 128, 4, jnp.where(lane < 192, 8, 16)))
    inv_win = jnp.where(lane < 64, 1 / 2, jnp.where(lane < 128, 1 / 4, jnp.where(lane < 192, 1 / 8, 1 / 16)))
    inv_ref[...] = 1.0 / (row + 1).astype(F32)
    groups = rows // SUBLANES
    row_in_group = lax.broadcasted_iota(jnp.int32, (groups, SUBLANES, LRU_WIDTH), 1)
    group = lax.broadcasted_iota(jnp.int32, (groups, LRU_WIDTH), 0)

    def chunk(c, carry):
        tail_u, tail_x, h_prev = carry
        r0 = pl.multiple_of(c * rows, rows)
        u = r_ref[pl.ds(r0, rows), 0:POOL_WIDTH]
        xr = r_ref[pl.ds(r0, rows), POOL_WIDTH:POOL_WIDTH + LRU_WIDTH]
        xg = r_ref[pl.ds(r0, rows), POOL_WIDTH + LRU_WIDTH:REST_WIDTH]

        ue = jnp.concatenate([tail_u, u], axis=0)
        w2 = ue + pltpu.roll(ue, 1, 0)
        w4 = w2 + pltpu.roll(w2, 2, 0)
        w8 = w4 + pltpu.roll(w4, 4, 0)
        w16 = w8 + pltpu.roll(w8, 8, 0)
        ws = jnp.where(lane_e < 64, w2, jnp.where(lane_e < 128, w4, jnp.where(lane_e < 192, w8, w16)))
        ws = ws[pool_hist:]
        inv_count = jnp.where(r0 + row + 1 >= win, inv_win, inv_ref[...])
        pooled = ws * inv_count - u
        y_pool = jnp.dot(pooled.astype(BF16), pw, preferred_element_type=F32) * ps

        xe = jnp.concatenate([tail_x, xr], axis=0)
        xc = cb + pltpu.roll(xe, 3, 0)[conv_hist:] * cw[0:1]
        xc = xc + pltpu.roll(xe, 2, 0)[conv_hist:] * cw[1:2]
        xc = xc + pltpu.roll(xe, 1, 0)[conv_hist:] * cw[2:3]
        xc = xc + xr * cw[3:4]
        xcb = xc.astype(BF16)
        r_gate = jax.nn.sigmoid(jnp.dot(xcb, wa, preferred_element_type=F32) + ba)
        i_gate = jax.nn.sigmoid(jnp.dot(xcb, wi, preferred_element_type=F32) + bi)
        log_a = r_gate * neg_c_softplus
        a = jnp.exp(log_a)
        gap = -jnp.tanh(log_a) * (a * a + 1.0)
        b = (gap * lax.rsqrt(jnp.maximum(gap, F32_TINY))) * (i_gate * xc)

        def doubling(a, b, pos, length, axis):
            s = 1
            while s < length:
                keep = pos >= s
                a_prev = jnp.where(keep, pltpu.roll(a, s, axis), 1.0)
                b_prev = jnp.where(keep, pltpu.roll(b, s, axis), 0.0)
                b = a * b_prev + b
                a = a * a_prev
                s *= 2
            return a, b

        a, b = doubling(a.reshape(groups, SUBLANES, LRU_WIDTH), b.reshape(groups, SUBLANES, LRU_WIDTH),
                        row_in_group, SUBLANES, 1)
        a = a.reshape(rows, LRU_WIDTH)
        b = b.reshape(rows, LRU_WIDTH)
        halves = range(LRU_WIDTH // LANES)
        ends = pl.ds(SUBLANES - 1, groups, stride=SUBLANES)
        for k in halves:
            a_ref[k] = a[:, k * LANES:(k + 1) * LANES]
            b_ref[k] = b[:, k * LANES:(k + 1) * LANES]
        a_end = jnp.concatenate([a_ref[k, ends, :] for k in halves], axis=-1)
        b_end = jnp.concatenate([b_ref[k, ends, :] for k in halves], axis=-1)
        a_end, b_end = doubling(a_end, b_end, group, groups, 0)
        h_end = a_end * h_prev + b_end
        h_ref[...] = jnp.where(group == 0, h_prev, pltpu.roll(h_end, 1, 0))
        h_start = jnp.concatenate([jnp.broadcast_to(h_ref[g:g + 1, :], (SUBLANES, LRU_WIDTH))
                                   for g in range(groups)], axis=0)
        h = a * h_start + b
        y_lru = h * jax.nn.gelu(xg)

        y_ref[pl.ds(r0, rows), 0:POOL_WIDTH] = y_pool.astype(BF16)
        y_ref[pl.ds(r0, rows), POOL_WIDTH:MIX_WIDTH] = y_lru.astype(BF16)
        return u[rows - pool_hist:], xr[rows - conv_hist:], h_end[groups - 1:groups]

    init = (jnp.zeros((pool_hist, POOL_WIDTH), F32), jnp.zeros((conv_hist, LRU_WIDTH), F32),
            jnp.zeros((1, LRU_WIDTH), F32))
    lax.fori_loop(0, seq // rows, chunk, init)


def _block_diag(w):
    g, c, d = w.shape
    eye = jnp.eye(g, dtype=w.dtype)
    return (eye[:, None, :, None] * w[:, :, None, :]).reshape(g * c, g * d)


def _mixer(rest, pool_w, pool_scale, conv_w, conv_b, wa, ba, wi, bi, lru_lambda, batch, seq):
    t = rest.shape[0]
    full = lambda shape: pl.BlockSpec(shape, lambda b: (0, 0))
    vec = lambda a: a.reshape(1, -1)
    return pl.pallas_call(
        functools.partial(_mixer_body, seq=seq),
        grid=(batch,),
        in_specs=[pl.BlockSpec((seq, REST_WIDTH), lambda b: (b, 0)),
                  full((POOL_WIDTH, POOL_WIDTH)), full((1, POOL_WIDTH)),
                  full((CONV_WIDTH, LRU_WIDTH)), full((1, LRU_WIDTH)),
                  full((LRU_WIDTH, LRU_WIDTH)), full((1, LRU_WIDTH)),
                  full((LRU_WIDTH, LRU_WIDTH)), full((1, LRU_WIDTH)),
                  full((1, LRU_WIDTH))],
        out_specs=pl.BlockSpec((seq, MIX_WIDTH), lambda b: (b, 0)),
        out_shape=jax.ShapeDtypeStruct((t, MIX_WIDTH), BF16),
        scratch_shapes=[pltpu.VMEM((LRU_WIDTH // LANES, MIX_ROWS, LANES), F32),
                        pltpu.VMEM((LRU_WIDTH // LANES, MIX_ROWS, LANES), F32),
                        pltpu.VMEM((MIX_ROWS // SUBLANES, LRU_WIDTH), F32),
                        pltpu.VMEM((MIX_ROWS, POOL_WIDTH), F32)],
        compiler_params=_params("arbitrary"),
        name="pool_lru_mixer",
    )(rest, _block_diag(pool_w).astype(BF16), vec(pool_scale), conv_w, vec(conv_b),
      _block_diag(wa).astype(BF16), vec(ba), _block_diag(wi).astype(BF16), vec(bi), vec(lru_lambda))


def _outproj_body(ya_ref, ym_ref, x_ref, w_ref, g_ref, b_ref, *rest, alpha, route):
    if route:
        wr_ref, o_ref, ot_ref, meta_ref, gate_ref, cnt_ref, wbf_ref, run_ref = rest
    else:
        o_ref, wbf_ref = rest

    @pl.when(pl.program_id(0) == 0)
    def _cast_weights():
        for c in range(0, D_MODEL, 256):
            wbf_ref[:, c:c + 256] = w_ref[0, :, c:c + 256].astype(BF16)

    mix = jnp.dot(ya_ref[...], wbf_ref[0:ATTN_WIDTH, :], preferred_element_type=F32)
    mix = mix + jnp.dot(ym_ref[...], wbf_ref[ATTN_WIDTH:, :], preferred_element_type=F32)
    x1 = _layer_norm(alpha * x_ref[...] + mix, g_ref[...], b_ref[...])
    o_ref[...] = x1
    if route:
        _store_row_tiles(ot_ref, x1)
        _route(x1, wr_ref, meta_ref, gate_ref, cnt_ref, run_ref)


def _outproj(y_attn, y_mix, x, w, layer, g, b, alpha, w_router=None):
    t = x.shape[0]
    route = w_router is not None
    row = lambda i: (i, 0)
    const = lambda i: (0, 0)
    in_specs = [pl.BlockSpec((ROW_TILE, ATTN_WIDTH), row), pl.BlockSpec((ROW_TILE, MIX_WIDTH), row),
                pl.BlockSpec((ROW_TILE, D_MODEL), row),
                pl.BlockSpec((1, D_MODEL, D_MODEL), lambda i: (layer, 0, 0)),
                pl.BlockSpec((1, D_MODEL), const), pl.BlockSpec((1, D_MODEL), const)]
    args = [y_attn, y_mix, x, w, g.reshape(1, -1), b.reshape(1, -1)]
    out_specs = [pl.BlockSpec((ROW_TILE, D_MODEL), row)]
    out_shape = [jax.ShapeDtypeStruct((t, D_MODEL), F32)]
    scratch = [pltpu.VMEM((D_MODEL, D_MODEL), BF16)]
    if route:
        in_specs.append(pl.BlockSpec((N_EXPERTS, D_MODEL), const))
        args.append(w_router.T)
        out_specs += [pl.BlockSpec((ROW_TILE * SUBLANES, LANES), row),
                      pl.BlockSpec((ROW_TILE, ROUTE_META), row), pl.BlockSpec((ROW_TILE, LANES), row),
                      pl.BlockSpec((SUBLANES, LANES), const)]
        out_shape += [jax.ShapeDtypeStruct((t * SUBLANES, LANES), F32),
                      jax.ShapeDtypeStruct((t, ROUTE_META), jnp.int32), jax.ShapeDtypeStruct((t, LANES), F32),
                      jax.ShapeDtypeStruct((SUBLANES, LANES), F32)]
        scratch.append(pltpu.VMEM((1, LANES), F32))
    return pl.pallas_call(
        functools.partial(_outproj_body, alpha=alpha, route=route),
        grid=(t // ROW_TILE,),
        in_specs=in_specs,
        out_specs=out_specs,
        out_shape=out_shape,
        scratch_shapes=scratch,
        compiler_params=_params("arbitrary"),
        name="outproj_ln_route" if route else "outproj_ln",
    )(*args)


def _dense_ffn_body(x_ref, wg_hbm, wu_hbm, wd_hbm, g_ref, b_ref, o_ref,
                    wg_ref, wu_ref, wd_ref, stage_in_ref, stage_out_ref, xb_ref, acc_ref, sem, *, layer, alpha):
    n_chunks = 3 * FFN_STEPS

    def chunk_copy(k):
        which, c = divmod(k, FFN_STEPS)
        cols = pl.ds(c * FFN_COLS, FFN_COLS)
        if which < 2:
            src, dst = (wg_hbm, wu_hbm)[which].at[layer, :, cols], stage_in_ref.at[k % 2]
        else:
            src, dst = wd_hbm.at[layer, cols, :], stage_out_ref.at[k % 2]
        return pltpu.make_async_copy(src, dst, sem.at[k % 2])

    @pl.when(pl.program_id(0) == 0)
    def _load_weights():
        chunk_copy(0).start()
        for k in range(n_chunks):
            if k + 1 < n_chunks:
                chunk_copy(k + 1).start()
            chunk_copy(k).wait()
            which, c = divmod(k, FFN_STEPS)
            if which == 0:
                wg_ref[c] = stage_in_ref[k % 2].astype(BF16)
            elif which == 1:
                wu_ref[c] = stage_in_ref[k % 2].astype(BF16)
            else:
                wd_ref[c] = stage_out_ref[k % 2].astype(BF16)

    xb_ref[...] = x_ref[...].astype(BF16)
    acc_ref[...] = jnp.zeros_like(acc_ref)

    def tile(c, _):
        xb = xb_ref[...]
        gate = jnp.dot(xb, wg_ref[c], preferred_element_type=F32)
        up = jnp.dot(xb, wu_ref[c], preferred_element_type=F32)
        hidden = (jax.nn.silu(gate) * up).astype(BF16)
        acc_ref[...] += jnp.dot(hidden, wd_ref[c], preferred_element_type=F32)
        return 0
    lax.fori_loop(0, FFN_STEPS, tile, 0)
    o_ref[...] = _layer_norm(alpha * x_ref[...] + acc_ref[...], g_ref[...], b_ref[...])


def _dense_ffn(x, w_gate, w_up, w_down, layer, g, b, alpha):
    t = x.shape[0]
    row = lambda i: (i, 0)
    const = lambda i: (0, 0)
    any_spec = pl.BlockSpec(memory_space=pl.ANY)
    return pl.pallas_call(
        functools.partial(_dense_ffn_body, layer=layer, alpha=alpha),
        grid=(t // FFN_ROWS,),
        in_specs=[pl.BlockSpec((FFN_ROWS, D_MODEL), row), any_spec, any_spec, any_spec,
                  pl.BlockSpec((1, D_MODEL), const), pl.BlockSpec((1, D_MODEL), const)],
        out_specs=pl.BlockSpec((FFN_ROWS, D_MODEL), row),
        out_shape=jax.ShapeDtypeStruct((t, D_MODEL), F32),
        scratch_shapes=[pltpu.VMEM((FFN_STEPS, D_MODEL, FFN_COLS), BF16),
                        pltpu.VMEM((FFN_STEPS, D_MODEL, FFN_COLS), BF16),
                        pltpu.VMEM((FFN_STEPS, FFN_COLS, D_MODEL), BF16),
                        pltpu.VMEM((2, D_MODEL, FFN_COLS), F32), pltpu.VMEM((2, FFN_COLS, D_MODEL), F32),
                        pltpu.VMEM((FFN_ROWS, D_MODEL), BF16), pltpu.VMEM((FFN_ROWS, D_MODEL), F32),
                        pltpu.SemaphoreType.DMA((2,))],
        compiler_params=_params("arbitrary"),
        name="swiglu_ln",
    )(x, w_gate, w_up, w_down, g.reshape(1, -1), b.reshape(1, -1))


def _store_row_tiles(o_ref, val):
    rows = val.shape[0]
    for s in range(SUBLANES):
        o_ref[pl.ds(s, rows, stride=SUBLANES), :] = val[:, s * LANES:(s + 1) * LANES]


def _load_row_tiles(ref, first_row, rows):
    return [ref[pl.ds(first_row * SUBLANES + s, rows, stride=SUBLANES), :] for s in range(SUBLANES)]


def _tile_window(row):
    return pl.ds(pl.multiple_of(row * SUBLANES, SUBLANES), SUBLANES)


def _start_row_copies(n_rows, make_copy, priorities):
    k = len(priorities)

    def body(q, _):
        for p, priority in enumerate(priorities):
            make_copy(q * k + p).start(priority=priority)
        return 0
    lax.fori_loop(0, n_rows // k, body, 0, unroll=DMA_ISSUE_UNROLL // k)


def _gather_copies(src_ref, x_hbm, buf_ref, sem, slot):
    def copy(r):
        return pltpu.make_async_copy(x_hbm.at[_tile_window(src_ref[0, 0, r])], buf_ref.at[slot, _tile_window(r)],
                                     sem.at[slot])
    return copy


def _expert_ffn_body(be_ref, nu_ref, src0_ref, src_next_ref, x_hbm, wg_hbm, wu_hbm, wd_hbm, o_ref,
                     xb_ref, acc_ref, buf_ref, wg_buf, wu_buf, wd_buf, row_sem, w_sem):
    i = pl.program_id(0)
    n_used = nu_ref[0]
    used = i < n_used
    slot = lax.rem(i, 2)
    row_priority = 1

    def wait_rows(which):
        pltpu.make_async_copy(buf_ref.at[which], buf_ref.at[which], row_sem.at[which]).wait()

    def weight_copies(block, c, wslot):
        e = be_ref[block]
        cols = pl.ds(pl.multiple_of(c * FFN_COLS, FFN_COLS), FFN_COLS)
        return (pltpu.make_async_copy(wg_hbm.at[e, :, cols], wg_buf.at[wslot], w_sem.at[wslot]),
                pltpu.make_async_copy(wu_hbm.at[e, :, cols], wu_buf.at[wslot], w_sem.at[wslot]),
                pltpu.make_async_copy(wd_hbm.at[e, cols, :], wd_buf.at[wslot], w_sem.at[wslot]))

    @pl.when(jnp.logical_and(used, i == 0))
    def _first_copies():
        for copy in weight_copies(0, 0, 0):
            copy.start()
        _start_row_copies(FFN_GATHER_ROWS, _gather_copies(src0_ref, x_hbm, buf_ref, row_sem, 0),
                          priorities=(row_priority,))

    @pl.when(used)
    def _block():
        wait_rows(slot)
        for s, part in enumerate(_load_row_tiles(buf_ref.at[slot], 0, FFN_ROWS)):
            xb_ref[:, s * LANES:(s + 1) * LANES] = part.astype(BF16)
        acc_ref[...] = jnp.zeros_like(acc_ref)
        next_block = jnp.minimum(i + 1, n_used - 1)
        next_row = _gather_copies(src_next_ref, x_hbm, buf_ref, row_sem, 1 - slot)

        def tile(c, _):
            wslot = lax.rem(i * FFN_STEPS + c, 2)
            last = c == FFN_STEPS - 1
            for copy in weight_copies(jnp.where(last, next_block, i), jnp.where(last, 0, c + 1), 1 - wslot):
                copy.start()
            for copy in weight_copies(i, c, wslot):
                copy.wait()
            for r in range(FFN_GATHER_ROWS_PER_STEP):
                next_row(c * FFN_GATHER_ROWS_PER_STEP + r).start(priority=row_priority)
            xb = xb_ref[...]
            gate = jnp.dot(xb, wg_buf[wslot].astype(BF16), preferred_element_type=F32)
            up = jnp.dot(xb, wu_buf[wslot].astype(BF16), preferred_element_type=F32)
            hidden = (jax.nn.silu(gate) * up).astype(BF16)
            acc_ref[...] += jnp.dot(hidden, wd_buf[wslot].astype(BF16), preferred_element_type=F32)
            return 0
        lax.fori_loop(0, FFN_STEPS, tile, 0)

        @pl.when(i == n_used - 1)
        def _drain():
            wait_rows(1 - slot)
            for copy in weight_copies(next_block, 0, lax.rem((i + 1) * FFN_STEPS, 2)):
                copy.wait()
        _store_row_tiles(o_ref, acc_ref[...])

    @pl.when(jnp.logical_not(used))
    def _empty():
        o_ref[...] = jnp.zeros_like(o_ref)


def _expert_ffn(block_e, n_used, x_tiles, w_gate, w_up, w_down, src):
    n_blocks = block_e.shape[0]
    dummy = jnp.broadcast_to(jnp.arange(FFN_GATHER_ROWS - FFN_ROWS, dtype=jnp.int32),
                             (n_blocks, 1, FFN_GATHER_ROWS - FFN_ROWS))
    src3 = jnp.concatenate([src.reshape(n_blocks, 1, FFN_ROWS), dummy], axis=-1)
    any_spec = pl.BlockSpec(memory_space=pl.ANY)
    in_specs = [pl.BlockSpec((1, 1, FFN_GATHER_ROWS), lambda i, be, nu: (0, 0, 0), memory_space=pltpu.SMEM),
                pl.BlockSpec((1, 1, FFN_GATHER_ROWS), lambda i, be, nu: (jnp.minimum(i + 1, n_blocks - 1), 0, 0),
                             memory_space=pltpu.SMEM),
                any_spec, any_spec, any_spec, any_spec]
    return pl.pallas_call(
        _expert_ffn_body,
        grid_spec=pltpu.PrefetchScalarGridSpec(
            num_scalar_prefetch=2,
            grid=(n_blocks,),
            in_specs=in_specs,
            out_specs=pl.BlockSpec((FFN_ROWS * SUBLANES, LANES), lambda i, be, nu: (i, 0)),
            scratch_shapes=[pltpu.VMEM((FFN_ROWS, D_MODEL), BF16), pltpu.VMEM((FFN_ROWS, D_MODEL), F32),
                            pltpu.VMEM((2, FFN_GATHER_ROWS * SUBLANES, LANES), F32),
                            pltpu.VMEM((2, D_MODEL, FFN_COLS), F32), pltpu.VMEM((2, D_MODEL, FFN_COLS), F32),
                            pltpu.VMEM((2, FFN_COLS, D_MODEL), F32),
                            pltpu.SemaphoreType.DMA((2,)), pltpu.SemaphoreType.DMA((2,))]),
        out_shape=jax.ShapeDtypeStruct((n_blocks * FFN_ROWS * SUBLANES, LANES), F32),
        compiler_params=_params("arbitrary"),
        name="swiglu_experts",
    )(block_e, n_used, src3, src3, x_tiles, w_gate, w_up, w_down)


def _route(x, w_ref, meta_ref, gate_ref, cnt_ref, run_ref):
    tm = x.shape[0]

    @pl.when(pl.program_id(0) == 0)
    def _init():
        run_ref[...] = jnp.zeros_like(run_ref)

    lane_i = lax.broadcasted_iota(jnp.int32, (tm, LANES), 1)
    lane = lane_i.astype(F32)
    logits = jnp.full((tm, LANES), -jnp.inf, F32)
    for e in range(N_EXPERTS):
        logit_e = jnp.sum(x * w_ref[e:e + 1, :], axis=-1, keepdims=True)
        logits = jnp.where(lane_i == e, logit_e, logits)
    m1 = jnp.max(logits, axis=-1, keepdims=True)
    e1 = jnp.min(jnp.where(logits == m1, lane, float(LANES)), axis=-1, keepdims=True)
    rest = jnp.where(lane == e1, -jnp.inf, logits)
    m2 = jnp.max(rest, axis=-1, keepdims=True)
    e2 = jnp.min(jnp.where(rest == m2, lane, float(LANES)), axis=-1, keepdims=True)
    ex = jnp.exp(m2 - m1)
    g1 = 1.0 / (1.0 + ex)
    g2 = ex / (1.0 + ex)

    chosen = jnp.logical_or(lane == e1, lane == e2)
    r_i = lax.broadcasted_iota(jnp.int32, (tm, tm), 0)
    c_i = lax.broadcasted_iota(jnp.int32, (tm, tm), 1)
    earlier = (c_i < r_i).astype(BF16)
    before = jnp.dot(earlier, chosen.astype(BF16), preferred_element_type=F32) + run_ref[...]
    rank1 = jnp.sum(jnp.where(lane == e1, before, 0.0), axis=-1, keepdims=True)
    rank2 = jnp.sum(jnp.where(lane == e2, before, 0.0), axis=-1, keepdims=True)
    total = run_ref[...] + jnp.sum(chosen.astype(F32), axis=0, keepdims=True)
    run_ref[...] = total
    cnt_ref[...] = jnp.broadcast_to(total, cnt_ref.shape)
    meta = jnp.where(lane_i == 0, e1, jnp.where(lane_i == 1, e2, jnp.where(lane_i == 2, rank1, rank2)))
    meta_ref[...] = meta[:, :ROUTE_META].astype(jnp.int32)
    gate_ref[...] = jnp.where(lane_i == 0, g1, g2)


def _combine_body(d0_ref, dn_ref, ys_hbm, gate_ref, x_ref, g_ref, b_ref, o_ref, buf_ref, sem, *, alpha):
    i = pl.program_id(0)
    slot = lax.rem(i, 2)
    n_rows = buf_ref.shape[1] // SUBLANES

    def row_copy(d_ref, to_slot):
        def copy(r):
            return pltpu.make_async_copy(ys_hbm.at[_tile_window(d_ref[0, 0, r])],
                                         buf_ref.at[to_slot, _tile_window(r)], sem.at[to_slot])
        return copy

    def wait_slot(which):
        pltpu.make_async_copy(buf_ref.at[which], buf_ref.at[which], sem.at[which]).wait()

    @pl.when(i == 0)
    def _first_rows():
        _start_row_copies(n_rows, row_copy(d0_ref, 0), priorities=(0, 1))

    wait_slot(slot)
    next_copy = row_copy(dn_ref, 1 - slot)
    for r in range(n_rows):
        next_copy(r).start(priority=r % 2)

    gates = gate_ref[...]
    g1, g2 = gates[:, 0:1], gates[:, 1:2]
    first = _load_row_tiles(buf_ref.at[slot], 0, MOVE_ROWS)
    second = _load_row_tiles(buf_ref.at[slot], MOVE_ROWS, MOVE_ROWS)
    f = jnp.concatenate([g1 * a + g2 * b for a, b in zip(first, second)], axis=-1)
    o_ref[...] = _layer_norm(alpha * x_ref[...] + f, g_ref[...], b_ref[...])

    @pl.when(i == pl.num_programs(0) - 1)
    def _drain():
        wait_slot(1 - slot)


def _combine(ys, dest1, dest2, gates, x, g, b, alpha):
    t = x.shape[0]
    steps = t // MOVE_ROWS
    dest = jnp.concatenate([dest1.reshape(steps, 1, MOVE_ROWS), dest2.reshape(steps, 1, MOVE_ROWS)], axis=-1)
    idx_shape = (1, 1, TOP_K * MOVE_ROWS)
    row = lambda i: (i, 0)
    const = lambda i: (0, 0)
    return pl.pallas_call(
        functools.partial(_combine_body, alpha=alpha),
        grid=(steps,),
        in_specs=[pl.BlockSpec(idx_shape, lambda i: (0, 0, 0), memory_space=pltpu.SMEM),
                  pl.BlockSpec(idx_shape, lambda i: (jnp.minimum(i + 1, steps - 1), 0, 0), memory_space=pltpu.SMEM),
                  pl.BlockSpec(memory_space=pl.ANY),
                  pl.BlockSpec((MOVE_ROWS, LANES), row), pl.BlockSpec((MOVE_ROWS, D_MODEL), row),
                  pl.BlockSpec((1, D_MODEL), const), pl.BlockSpec((1, D_MODEL), const)],
        out_specs=pl.BlockSpec((MOVE_ROWS, D_MODEL), row),
        out_shape=jax.ShapeDtypeStruct((t, D_MODEL), F32),
        scratch_shapes=[pltpu.VMEM((2, TOP_K * MOVE_ROWS * SUBLANES, LANES), F32), pltpu.SemaphoreType.DMA((2,))],
        compiler_params=_params("arbitrary"),
        name="combine_ln",
    )(dest, dest, ys, gates, x, g.reshape(1, -1), b.reshape(1, -1))


def _invert_body(d1_ref, d2_ref, fill_hbm, src_ref, sem):
    fill = pltpu.make_async_copy(fill_hbm, src_ref, sem)
    fill.start()
    fill.wait()

    def place(t, _):
        src_ref[d1_ref[t]] = t
        src_ref[d2_ref[t]] = t
        return 0
    lax.fori_loop(0, d1_ref.shape[0], place, 0, unroll=SCALAR_LOOP_UNROLL)


def _invert_placement(dest1, dest2, n_rows):
    smem = pl.BlockSpec(memory_space=pltpu.SMEM)
    return pl.pallas_call(
        _invert_body,
        in_specs=[smem, smem, pl.BlockSpec(memory_space=pl.ANY)],
        out_specs=smem,
        out_shape=jax.ShapeDtypeStruct((n_rows,), jnp.int32),
        scratch_shapes=[pltpu.SemaphoreType.DMA(())],
        name="invert_placement",
    )(dest1, dest2, jnp.arange(n_rows, dtype=jnp.int32) % dest1.shape[0])


def _moe(x1, x1_tiles, meta, gates, counts, w_gate, w_up, w_down, first_expert, g, b, alpha):
    t = x1.shape[0]
    e1, e2, rank1, rank2 = meta[:, 0], meta[:, 1], meta[:, 2], meta[:, 3]
    sizes = counts[0, :N_EXPERTS].astype(jnp.int32)
    padded = (sizes + FFN_ROWS - 1) // FFN_ROWS * FFN_ROWS
    group_end = jnp.cumsum(padded)
    group_start = group_end - padded
    dest1 = group_start[e1] + rank1
    dest2 = group_start[e2] + rank2
    n_blocks = (t * TOP_K) // FFN_ROWS + N_EXPERTS
    block_start = jnp.arange(n_blocks, dtype=jnp.int32) * FFN_ROWS
    block_e = jnp.minimum(jnp.sum(group_end[None, :] <= block_start[:, None], axis=1), N_EXPERTS - 1)
    n_used = (group_end[-1] // FFN_ROWS).reshape(1)
    src = _invert_placement(dest1, dest2, n_blocks * FFN_ROWS)
    ys = _expert_ffn((block_e + first_expert).astype(jnp.int32), n_used.astype(jnp.int32), x1_tiles, w_gate,
                     w_up, w_down, src)
    return _combine(ys, dest1, dest2, gates, x1, g, b, alpha)


def kernel(x, w_in, w_out, attn_lambda, attn_subln_g, pool_w, pool_scale, conv_w, conv_b, lru_wa, lru_ba,
           lru_wi, lru_bi, lru_lambda, ln1_g, ln1_b, ln2_g, ln2_b, ffn_w_gate, ffn_w_up, ffn_w_down,
           router_w, moe_w_gate, moe_w_up, moe_w_down):
    batch, seq, d = x.shape
    depth = w_in.shape[0]
    assert d == D_MODEL and seq % ATTN_TILE == 0 and seq % MIX_ROWS == 0
    t = batch * seq
    assert t % FFN_ROWS == 0 and t % ROW_TILE == 0
    alpha = (2.0 * depth) ** 0.25
    moe_gate = moe_w_gate.reshape(-1, D_MODEL, D_FF)
    moe_up = moe_w_up.reshape(-1, D_MODEL, D_FF)
    moe_down = moe_w_down.reshape(-1, D_FF, D_MODEL)
    xt = x.reshape(t, d)
    for l in range(depth):
        lambda_init = 0.8 - 0.6 * math.exp(-0.3 * l)
        q, k, v, rest = _inproj(xt, w_in, l)
        y_attn = _attention(q, k, v, attn_lambda[l], attn_subln_g[l], lambda_init, batch, seq)
        y_mix = _mixer(rest, pool_w[l], pool_scale[l], conv_w[l], conv_b[l], lru_wa[l], lru_ba[l],
                       lru_wi[l], lru_bi[l], lru_lambda[l], batch, seq)
        if l % 2 == 0:
            x1, = _outproj(y_attn, y_mix, xt, w_out, l, ln1_g[l], ln1_b[l], alpha)
            xt = _dense_ffn(x1, ffn_w_gate, ffn_w_up, ffn_w_down, l // 2, ln2_g[l], ln2_b[l], alpha)
        else:
            x1, x1_tiles, meta, gates, counts = _outproj(y_attn, y_mix, xt, w_out, l, ln1_g[l], ln1_b[l], alpha,
                                                         w_router=router_w[l // 2])
            xt = _moe(x1, x1_tiles, meta, gates, counts, moe_gate, moe_up, moe_down, (l // 2) * N_EXPERTS,
                      ln2_g[l], ln2_b[l], alpha)
    return xt.reshape(batch, seq, d)
```

```python
import functools
import math

import jax
import jax.numpy as jnp
from jax import lax
from jax.experimental import pallas as pl
from jax.experimental.pallas import tpu as pltpu

F32 = jnp.float32
BF16 = jnp.bfloat16

D_MODEL = 1024
CHUNK = 64
ATTN_HEADS = 4
ATTN_WIDTH = 512
ATTN_HEAD_DIM = 64
HEAD_WIDTH = 2 * ATTN_HEAD_DIM
POOL_WINDOWS = (2, 4, 8, 16)
POOL_WIDTH = 256
POOL_GROUP_DIM = 64
LRU_WIDTH = 256
LRU_C = 8.0
CONV_WIDTH = 4
REST_WIDTH = POOL_WIDTH + 2 * LRU_WIDTH
IN_WIDTH = 3 * ATTN_WIDTH + REST_WIDTH
MIX_WIDTH = POOL_WIDTH + LRU_WIDTH
D_FF = 2816
N_EXPERTS = 8
TOP_K = 2
LN_EPS = 1e-5
HEAD_NORM_EPS = 1e-5

LANES = 128
SUBLANES = 8
VMEM_LIMIT_BYTES = 56 * 1024 * 1024

ROW_TILE = 512
ATTN_TILE = 256
ATTN_HEADS_PER_STEP = 4
ATTN_SUM_ROWS = 16
ATTN_SCORE_SCALE = ATTN_HEAD_DIM ** -0.5 * math.log2(math.e)
MIX_ROWS = 256
FFN_ROWS = 1024
FFN_COLS = 256
FFN_STEPS = D_FF // FFN_COLS
WEIGHT_TILES_AHEAD = 2
WEIGHT_SLOTS = WEIGHT_TILES_AHEAD + 1
FFN_GATHER_ROWS_PER_STEP = -(-FFN_ROWS // ((FFN_STEPS - 1) * SUBLANES)) * SUBLANES
FFN_GATHER_ROWS = FFN_GATHER_ROWS_PER_STEP * FFN_STEPS
MOVE_ROWS = 256
ROUTE_META = 8
SCALAR_LOOP_UNROLL = 8
DMA_ISSUE_UNROLL = 8
NEG_BIG = -1e30
F32_TINY = float(jnp.finfo(jnp.float32).tiny)
assert MIX_ROWS >= max(POOL_WINDOWS)


def _params(*semantics):
    return pltpu.CompilerParams(dimension_semantics=semantics, vmem_limit_bytes=VMEM_LIMIT_BYTES)


def _layer_norm(z, g, b):
    mu = jnp.mean(z, axis=-1, keepdims=True)
    zc = z - mu
    var = jnp.mean(zc * zc, axis=-1, keepdims=True)
    return zc * lax.rsqrt(var + LN_EPS) * g + b


def _inproj_body(x_ref, w_ref, q_ref, k_ref, v_ref, r_ref, wbf_ref):
    @pl.when(pl.program_id(0) == 0)
    def _cast_weights():
        for c in range(0, IN_WIDTH, 256):
            wbf_ref[:, c:c + 256] = w_ref[0, :, c:c + 256].astype(BF16)

    xb = x_ref[...].astype(BF16)

    def proj(c0, c1):
        return jnp.dot(xb, wbf_ref[:, c0:c1], preferred_element_type=F32)

    q_ref[...] = (proj(0, ATTN_WIDTH) * ATTN_SCORE_SCALE).astype(BF16)
    k_ref[...] = proj(ATTN_WIDTH, 2 * ATTN_WIDTH).astype(BF16)
    v_ref[...] = proj(2 * ATTN_WIDTH, 3 * ATTN_WIDTH).astype(BF16)
    r_ref[...] = proj(3 * ATTN_WIDTH, IN_WIDTH)


def _inproj(x, w, layer):
    t = x.shape[0]
    row = lambda i: (i, 0)
    return pl.pallas_call(
        _inproj_body,
        grid=(t // ROW_TILE,),
        in_specs=[pl.BlockSpec((ROW_TILE, D_MODEL), row),
                  pl.BlockSpec((1, D_MODEL, IN_WIDTH), lambda i: (layer, 0, 0))],
        out_specs=[pl.BlockSpec((ROW_TILE, ATTN_WIDTH), row)] * 3
        + [pl.BlockSpec((ROW_TILE, REST_WIDTH), row)],
        out_shape=[jax.ShapeDtypeStruct((t, ATTN_WIDTH), BF16)] * 3
        + [jax.ShapeDtypeStruct((t, REST_WIDTH), F32)],
        scratch_shapes=[pltpu.VMEM((D_MODEL, IN_WIDTH), BF16)],
        compiler_params=_params("arbitrary"),
        name="inproj",
    )(x, w)


def _attn_body(lam_ref, g_ref, q_ref, k_ref, v_ref, o_ref, vt_ref, *, lambda_init, seq):
    tq = ATTN_TILE
    lv = lam_ref[...]
    lam = (jnp.exp(jnp.sum(lv[0:1] * lv[1:2], axis=-1, keepdims=True))
           - jnp.exp(jnp.sum(lv[2:3] * lv[3:4], axis=-1, keepdims=True)) + lambda_init)
    gain = g_ref[...] * (1.0 - lambda_init)

    heads = range(ATTN_HEADS_PER_STEP)
    cols = [slice(g * HEAD_WIDTH, (g + 1) * HEAD_WIDTH) for g in heads]
    for g in heads:
        for c in range(seq // tq):
            vt_ref[g, c, 0:HEAD_WIDTH, :] = v_ref[c * tq:(c + 1) * tq, cols[g]].astype(F32).T.astype(BF16)
            vt_ref[g, c, HEAD_WIDTH:, :] = jnp.ones((ATTN_SUM_ROWS, tq), BF16)

    lane = lax.broadcasted_iota(jnp.int32, (tq, HEAD_WIDTH), 1)
    first_map = lane < ATTN_HEAD_DIM
    key = lax.broadcasted_iota(jnp.int32, (tq, 2 * tq), 0)
    qry = lax.broadcasted_iota(jnp.int32, (tq, 2 * tq), 1)
    q_chunk = jnp.where(qry >= tq, qry - tq, qry) // CHUNK
    visible = (key // CHUNK) <= q_chunk

    def q_block(i, _):
        q0 = pl.multiple_of(i * tq, tq)
        qqs = []
        for g in heads:
            q = q_ref[pl.ds(q0, tq), cols[g]]
            zero = jnp.zeros_like(q)
            qqs.append(jnp.concatenate([jnp.where(first_map, q, zero), jnp.where(first_map, zero, q)], axis=0))

        def step(j, carry, masked):
            k0 = pl.multiple_of(j * tq, tq)
            out = []
            scores = [lax.dot_general(k_ref[pl.ds(k0, tq), cols[g]], qqs[g], (((1,), (1,)), ((), ())),
                                      preferred_element_type=F32) for g in heads]
            for g in heads:
                m, acc = carry[g]
                s = scores[g]
                if masked:
                    s = jnp.where(visible, s, NEG_BIG)
                m_new = jnp.maximum(m, jnp.max(s, axis=0, keepdims=True))
                p = jnp.exp2(s - m_new)
                scale = jnp.exp2(m - m_new)
                pv = jnp.dot(vt_ref[g, j], p.astype(BF16), preferred_element_type=F32)
                out.append((m_new, scale * acc + pv))
            return tuple(out)

        init = tuple((jnp.full((1, 2 * tq), NEG_BIG, F32),
                      jnp.zeros((HEAD_WIDTH + ATTN_SUM_ROWS, 2 * tq), F32)) for _ in heads)
        carry = lax.fori_loop(0, i, lambda j, c: step(j, c, False), init)
        carry = step(i, carry, True)
        for g in heads:
            _, acc = carry[g]
            l = acc[HEAD_WIDTH:HEAD_WIDTH + 1, :]
            acc = acc[:HEAD_WIDTH, :]
            o = acc[:, :tq] / l[:, :tq] - lam * (acc[:, tq:] / l[:, tq:])
            o = o * lax.rsqrt(jnp.mean(o * o, axis=0, keepdims=True) + HEAD_NORM_EPS)
            o_ref[pl.ds(q0, tq), cols[g]] = (o.T * gain).astype(BF16)
        return 0

    lax.fori_loop(0, seq // tq, q_block, 0)


def _attention(q, k, v, lam_params, subln_g, lambda_init, batch, seq):
    t = q.shape[0]
    blk = pl.BlockSpec((seq, ATTN_HEADS_PER_STEP * HEAD_WIDTH), lambda b, h: (b, h))
    return pl.pallas_call(
        functools.partial(_attn_body, lambda_init=lambda_init, seq=seq),
        grid=(batch, ATTN_HEADS // ATTN_HEADS_PER_STEP),
        in_specs=[pl.BlockSpec((4, ATTN_HEAD_DIM), lambda b, h: (0, 0)),
                  pl.BlockSpec((1, HEAD_WIDTH), lambda b, h: (0, 0)),
                  blk, blk, blk],
        out_specs=blk,
        out_shape=jax.ShapeDtypeStruct((t, ATTN_WIDTH), BF16),
        scratch_shapes=[pltpu.VMEM((ATTN_HEADS_PER_STEP, seq // ATTN_TILE, HEAD_WIDTH + ATTN_SUM_ROWS, ATTN_TILE),
                                   BF16)],
        compiler_params=_params("arbitrary", "arbitrary"),
        name="diff_attention",
    )(lam_params, subln_g.reshape(1, HEAD_WIDTH), q, k, v)


def _mixer_body(r_ref, pw_ref, ps_ref, cw_ref, cb_ref, wa_ref, ba_ref, wi_ref, bi_ref, lam_ref, y_ref,
                a_ref, b_ref, h_ref, inv_ref, *, seq):
    rows = MIX_ROWS
    pool_hist = max(POOL_WINDOWS)
    conv_hist = SUBLANES
    pw = pw_ref[...]
    wa = wa_ref[...]
    wi = wi_ref[...]
    ps, cb, ba, bi = ps_ref[...], cb_ref[...], ba_ref[...], bi_ref[...]
    cw = cw_ref[...]
    neg_c_softplus = -LRU_C * jax.nn.softplus(-lam_ref[...])

    lane_e = lax.broadcasted_iota(jnp.int32, (rows + pool_hist, POOL_WIDTH), 1)
    lane = lax.broadcasted_iota(jnp.int32, (rows, POOL_WIDTH), 1)
    row = lax.broadcasted_iota(jnp.int32, (rows, POOL_WIDTH), 0)
    win = jnp.where(lane < 64, 2, jnp.where(lane < 128, 4, jnp.where(lane < 192, 8, 16)))
    inv_win = jnp.where(lane < 64, 1 / 2, jnp.where(lane < 128, 1 / 4, jnp.where(lane < 192, 1 / 8, 1 / 16)))
    inv_ref[...] = 1.0 / (row + 1).astype(F32)
    groups = rows // SUBLANES
    row_in_group = lax.broadcasted_iota(jnp.int32, (groups, SUBLANES, LRU_WIDTH), 1)
    group = lax.broadcasted_iota(jnp.int32, (groups, LRU_WIDTH), 0)

    def chunk(c, carry):
        tail_u, tail_x, h_prev = carry
        r0 = pl.multiple_of(c * rows, rows)
        u = r_ref[pl.ds(r0, rows), 0:POOL_WIDTH]
        xr = r_ref[pl.ds(r0, rows), POOL_WIDTH:POOL_WIDTH + LRU_WIDTH]
        xg = r_ref[pl.ds(r0, rows), POOL_WIDTH + LRU_WIDTH:REST_WIDTH]

        ue = jnp.concatenate([tail_u, u], axis=0)
        w2 = ue + pltpu.roll(ue, 1, 0)
        w4 = w2 + pltpu.roll(w2, 2, 0)
        w8 = w4 + pltpu.roll(w4, 4, 0)
        w16 = w8 + pltpu.roll(w8, 8, 0)
        ws = jnp.where(lane_e < 64, w2, jnp.where(lane_e < 128, w4, jnp.where(lane_e < 192, w8, w16)))
        ws = ws[pool_hist:]
        inv_count = jnp.where(r0 + row + 1 >= win, inv_win, inv_ref[...])
        pooled = ws * inv_count - u
        y_pool = jnp.dot(pooled.astype(BF16), pw, preferred_element_type=F32) * ps

        xe = jnp.concatenate([tail_x, xr], axis=0)
        xc = cb + pltpu.roll(xe, 3, 0)[conv_hist:] * cw[0:1]
        xc = xc + pltpu.roll(xe, 2, 0)[conv_hist:] * cw[1:2]
        xc = xc + pltpu.roll(xe, 1, 0)[conv_hist:] * cw[2:3]
        xc = xc + xr * cw[3:4]
        xcb = xc.astype(BF16)
        r_gate = jax.nn.sigmoid(jnp.dot(xcb, wa, preferred_element_type=F32) + ba)
        i_gate = jax.nn.sigmoid(jnp.dot(xcb, wi, preferred_element_type=F32) + bi)
        log_a = r_gate * neg_c_softplus
        a = jnp.exp(log_a)
        gap = -jnp.tanh(log_a) * (a * a + 1.0)
        b = (gap * lax.rsqrt(jnp.maximum(gap, F32_TINY))) * (i_gate * xc)

        def doubling(a, b, pos, length, axis):
            s = 1
            while s < length:
                keep = pos >= s
                a_prev = jnp.where(keep, pltpu.roll(a, s, axis), 1.0)
                b_prev = jnp.where(keep, pltpu.roll(b, s, axis), 0.0)
                b = a * b_prev + b
                a = a * a_prev
                s *= 2
            return a, b

        a, b = doubling(a.reshape(groups, SUBLANES, LRU_WIDTH), b.reshape(groups, SUBLANES, LRU_WIDTH),
                        row_in_group, SUBLANES, 1)
        a = a.reshape(rows, LRU_WIDTH)
        b = b.reshape(rows, LRU_WIDTH)
        halves = range(LRU_WIDTH // LANES)
        ends = pl.ds(SUBLANES - 1, groups, stride=SUBLANES)
        for k in halves:
            a_ref[k] = a[:, k * LANES:(k + 1) * LANES]
            b_ref[k] = b[:, k * LANES:(k + 1) * LANES]
        a_end = jnp.concatenate([a_ref[k, ends, :] for k in halves], axis=-1)
        b_end = jnp.concatenate([b_ref[k, ends, :] for k in halves], axis=-1)
        a_end, b_end = doubling(a_end, b_end, group, groups, 0)
        h_end = a_end * h_prev + b_end
        h_ref[...] = jnp.where(group == 0, h_prev, pltpu.roll(h_end, 1, 0))
        h_start = jnp.concatenate([jnp.broadcast_to(h_ref[g:g + 1, :], (SUBLANES, LRU_WIDTH))
                                   for g in range(groups)], axis=0)
        h = a * h_start + b
        y_lru = h * jax.nn.gelu(xg)

        y_ref[pl.ds(r0, rows), 0:POOL_WIDTH] = y_pool.astype(BF16)
        y_ref[pl.ds(r0, rows), POOL_WIDTH:MIX_WIDTH] = y_lru.astype(BF16)
        return u[rows - pool_hist:], xr[rows - conv_hist:], h_end[groups - 1:groups]

    init = (jnp.zeros((pool_hist, POOL_WIDTH), F32), jnp.zeros((conv_hist, LRU_WIDTH), F32),
            jnp.zeros((1, LRU_WIDTH), F32))
    lax.fori_loop(0, seq // rows, chunk, init)


def _block_diag(w):
    g, c, d = w.shape
    eye = jnp.eye(g, dtype=w.dtype)
    return (eye[:, None, :, None] * w[:, :, None, :]).reshape(g * c, g * d)


def _mixer(rest, pool_w, pool_scale, conv_w, conv_b, wa, ba, wi, bi, lru_lambda, batch, seq):
    t = rest.shape[0]
    full = lambda shape: pl.BlockSpec(shape, lambda b: (0, 0))
    vec = lambda a: a.reshape(1, -1)
    return pl.pallas_call(
        functools.partial(_mixer_body, seq=seq),
        grid=(batch,),
        in_specs=[pl.BlockSpec((seq, REST_WIDTH), lambda b: (b, 0)),
                  full((POOL_WIDTH, POOL_WIDTH)), full((1, POOL_WIDTH)),
                  full((CONV_WIDTH, LRU_WIDTH)), full((1, LRU_WIDTH)),
                  full((LRU_WIDTH, LRU_WIDTH)), full((1, LRU_WIDTH)),
                  full((LRU_WIDTH, LRU_WIDTH)), full((1, LRU_WIDTH)),
                  full((1, LRU_WIDTH))],
        out_specs=pl.BlockSpec((seq, MIX_WIDTH), lambda b: (b, 0)),
        out_shape=jax.ShapeDtypeStruct((t, MIX_WIDTH), BF16),
        scratch_shapes=[pltpu.VMEM((LRU_WIDTH // LANES, MIX_ROWS, LANES), F32),
                        pltpu.VMEM((LRU_WIDTH // LANES, MIX_ROWS, LANES), F32),
                        pltpu.VMEM((MIX_ROWS // SUBLANES, LRU_WIDTH), F32),
                        pltpu.VMEM((MIX_ROWS, POOL_WIDTH), F32)],
        compiler_params=_params("arbitrary"),
        name="pool_lru_mixer",
    )(rest, _block_diag(pool_w).astype(BF16), vec(pool_scale), conv_w, vec(conv_b),
      _block_diag(wa).astype(BF16), vec(ba), _block_diag(wi).astype(BF16), vec(bi), vec(lru_lambda))


def _outproj_body(ya_ref, ym_ref, x_ref, w_ref, g_ref, b_ref, *rest, alpha, route):
    if route:
        wr_ref, o_ref, ot_ref, meta_ref, gate_ref, cnt_ref, wbf_ref, run_ref = rest
    else:
        o_ref, wbf_ref = rest

    @pl.when(pl.program_id(0) == 0)
    def _cast_weights():
        for c in range(0, D_MODEL, 256):
            wbf_ref[:, c:c + 256] = w_ref[0, :, c:c + 256].astype(BF16)

    mix = jnp.dot(ya_ref[...], wbf_ref[0:ATTN_WIDTH, :], preferred_element_type=F32)
    mix = mix + jnp.dot(ym_ref[...], wbf_ref[ATTN_WIDTH:, :], preferred_element_type=F32)
    x1 = _layer_norm(alpha * x_ref[...] + mix, g_ref[...], b_ref[...])
    o_ref[...] = x1
    if route:
        _store_row_tiles(ot_ref, x1)
        _route(x1, wr_ref, meta_ref, gate_ref, cnt_ref, run_ref)


def _outproj(y_attn, y_mix, x, w, layer, g, b, alpha, w_router=None):
    t = x.shape[0]
    route = w_router is not None
    row = lambda i: (i, 0)
    const = lambda i: (0, 0)
    in_specs = [pl.BlockSpec((ROW_TILE, ATTN_WIDTH), row), pl.BlockSpec((ROW_TILE, MIX_WIDTH), row),
                pl.BlockSpec((ROW_TILE, D_MODEL), row),
                pl.BlockSpec((1, D_MODEL, D_MODEL), lambda i: (layer, 0, 0)),
                pl.BlockSpec((1, D_MODEL), const), pl.BlockSpec((1, D_MODEL), const)]
    args = [y_attn, y_mix, x, w, g.reshape(1, -1), b.reshape(1, -1)]
    out_specs = [pl.BlockSpec((ROW_TILE, D_MODEL), row)]
    out_shape = [jax.ShapeDtypeStruct((t, D_MODEL), F32)]
    scratch = [pltpu.VMEM((D_MODEL, D_MODEL), BF16)]
    if route:
        in_specs.append(pl.BlockSpec((N_EXPERTS, D_MODEL), const))
        args.append(w_router.T)
        out_specs += [pl.BlockSpec((ROW_TILE * SUBLANES, LANES), row),
                      pl.BlockSpec((ROW_TILE, ROUTE_META), row), pl.BlockSpec((ROW_TILE, LANES), row),
                      pl.BlockSpec((SUBLANES, LANES), const)]
        out_shape += [jax.ShapeDtypeStruct((t * SUBLANES, LANES), F32),
                      jax.ShapeDtypeStruct((t, ROUTE_META), jnp.int32), jax.ShapeDtypeStruct((t, LANES), F32),
                      jax.ShapeDtypeStruct((SUBLANES, LANES), F32)]
        scratch.append(pltpu.VMEM((1, LANES), F32))
    return pl.pallas_call(
        functools.partial(_outproj_body, alpha=alpha, route=route),
        grid=(t // ROW_TILE,),
        in_specs=in_specs,
        out_specs=out_specs,
        out_shape=out_shape,
        scratch_shapes=scratch,
        compiler_params=_params("arbitrary"),
        name="outproj_ln_route" if route else "outproj_ln",
    )(*args)


def _dense_ffn_body(x_ref, wg_hbm, wu_hbm, wd_hbm, g_ref, b_ref, o_ref,
                    wg_ref, wu_ref, wd_ref, stage_in_ref, stage_out_ref, xb_ref, acc_ref, sem, *, layer, alpha):
    n_chunks = 3 * FFN_STEPS

    def chunk_copy(k):
        which, c = divmod(k, FFN_STEPS)
        cols = pl.ds(c * FFN_COLS, FFN_COLS)
        if which < 2:
            src, dst = (wg_hbm, wu_hbm)[which].at[layer, :, cols], stage_in_ref.at[k % 2]
        else:
            src, dst = wd_hbm.at[layer, cols, :], stage_out_ref.at[k % 2]
        return pltpu.make_async_copy(src, dst, sem.at[k % 2])

    @pl.when(pl.program_id(0) == 0)
    def _load_weights():
        chunk_copy(0).start()
        for k in range(n_chunks):
            if k + 1 < n_chunks:
                chunk_copy(k + 1).start()
            chunk_copy(k).wait()
            which, c = divmod(k, FFN_STEPS)
            if which == 0:
                wg_ref[c] = stage_in_ref[k % 2].astype(BF16)
            elif which == 1:
                wu_ref[c] = stage_in_ref[k % 2].astype(BF16)
            else:
                wd_ref[c] = stage_out_ref[k % 2].astype(BF16)

    xb_ref[...] = x_ref[...].astype(BF16)
    acc_ref[...] = jnp.zeros_like(acc_ref)

    def tile(c, _):
        xb = xb_ref[...]
        gate = jnp.dot(xb, wg_ref[c], preferred_element_type=F32)
        up = jnp.dot(xb, wu_ref[c], preferred_element_type=F32)
        hidden = (jax.nn.silu(gate) * up).astype(BF16)
        acc_ref[...] += jnp.dot(hidden, wd_ref[c], preferred_element_type=F32)
        return 0
    lax.fori_loop(0, FFN_STEPS, tile, 0)
    o_ref[...] = _layer_norm(alpha * x_ref[...] + acc_ref[...], g_ref[...], b_ref[...])


def _dense_ffn(x, w_gate, w_up, w_down, layer, g, b, alpha):
    t = x.shape[0]
    row = lambda i: (i, 0)
    const = lambda i: (0, 0)
    any_spec = pl.BlockSpec(memory_space=pl.ANY)
    return pl.pallas_call(
        functools.partial(_dense_ffn_body, layer=layer, alpha=alpha),
        grid=(t // FFN_ROWS,),
        in_specs=[pl.BlockSpec((FFN_ROWS, D_MODEL), row), any_spec, any_spec, any_spec,
                  pl.BlockSpec((1, D_MODEL), const), pl.BlockSpec((1, D_MODEL), const)],
        out_specs=pl.BlockSpec((FFN_ROWS, D_MODEL), row),
        out_shape=jax.ShapeDtypeStruct((t, D_MODEL), F32),
        scratch_shapes=[pltpu.VMEM((FFN_STEPS, D_MODEL, FFN_COLS), BF16),
                        pltpu.VMEM((FFN_STEPS, D_MODEL, FFN_COLS), BF16),
                        pltpu.VMEM((FFN_STEPS, FFN_COLS, D_MODEL), BF16),
                        pltpu.VMEM((2, D_MODEL, FFN_COLS), F32), pltpu.VMEM((2, FFN_COLS, D_MODEL), F32),
                        pltpu.VMEM((FFN_ROWS, D_MODEL), BF16), pltpu.VMEM((FFN_ROWS, D_MODEL), F32),
                        pltpu.SemaphoreType.DMA((2,))],
        compiler_params=_params("arbitrary"),
        name="swiglu_ln",
    )(x, w_gate, w_up, w_down, g.reshape(1, -1), b.reshape(1, -1))


def _store_row_tiles(o_ref, val):
    rows = val.shape[0]
    for s in range(SUBLANES):
        o_ref[pl.ds(s, rows, stride=SUBLANES), :] = val[:, s * LANES:(s + 1) * LANES]


def _load_row_tiles(ref, first_row, rows):
    return [ref[pl.ds(first_row * SUBLANES + s, rows, stride=SUBLANES), :] for s in range(SUBLANES)]


def _tile_window(row):
    return pl.ds(pl.multiple_of(row * SUBLANES, SUBLANES), SUBLANES)


def _start_row_copies(n_rows, make_copy, priorities):
    k = len(priorities)

    def body(q, _):
        for p, priority in enumerate(priorities):
            make_copy(q * k + p).start(priority=priority)
        return 0
    lax.fori_loop(0, n_rows // k, body, 0, unroll=DMA_ISSUE_UNROLL // k)


def _gather_copies(src_ref, x_hbm, buf_ref, sem, slot):
    def copy(r):
        return pltpu.make_async_copy(x_hbm.at[_tile_window(src_ref[0, 0, r])], buf_ref.at[slot, _tile_window(r)],
                                     sem.at[slot])
    return copy


def _expert_ffn_body(be_ref, nu_ref, src0_ref, src_next_ref, x_hbm, wg_hbm, wu_hbm, wd_hbm, o_ref,
                     xb_ref, acc_ref, buf_ref, wg_buf, wu_buf, wd_buf, row_sem, w_sem):
    i = pl.program_id(0)
    n_used = nu_ref[0]
    used = i < n_used
    slot = lax.rem(i, 2)
    row_priority = 1

    def wait_rows(which):
        pltpu.make_async_copy(buf_ref.at[which], buf_ref.at[which], row_sem.at[which]).wait()

    def weight_copies(block, c, wslot):
        e = be_ref[block]
        cols = pl.ds(pl.multiple_of(c * FFN_COLS, FFN_COLS), FFN_COLS)
        return (pltpu.make_async_copy(wg_hbm.at[e, :, cols], wg_buf.at[wslot], w_sem.at[wslot]),
                pltpu.make_async_copy(wu_hbm.at[e, :, cols], wu_buf.at[wslot], w_sem.at[wslot]),
                pltpu.make_async_copy(wd_hbm.at[e, cols, :], wd_buf.at[wslot], w_sem.at[wslot]))

    @pl.when(jnp.logical_and(used, i == 0))
    def _first_copies():
        for k in range(WEIGHT_TILES_AHEAD):
            for copy in weight_copies(0, k, k):
                copy.start()
        _start_row_copies(FFN_GATHER_ROWS, _gather_copies(src0_ref, x_hbm, buf_ref, row_sem, 0),
                          priorities=(row_priority,))

    @pl.when(used)
    def _block():
        wait_rows(slot)
        for s, part in enumerate(_load_row_tiles(buf_ref.at[slot], 0, FFN_ROWS)):
            xb_ref[:, s * LANES:(s + 1) * LANES] = part.astype(BF16)
        acc_ref[...] = jnp.zeros_like(acc_ref)
        next_block = jnp.minimum(i + 1, n_used - 1)
        next_row = _gather_copies(src_next_ref, x_hbm, buf_ref, row_sem, 1 - slot)

        def tile(c, _):
            q = i * FFN_STEPS + c
            wslot = lax.rem(q, WEIGHT_SLOTS)
            ahead = c + WEIGHT_TILES_AHEAD
            wraps = ahead >= FFN_STEPS
            for copy in weight_copies(jnp.where(wraps, next_block, i), jnp.where(wraps, ahead - FFN_STEPS, ahead),
                                      lax.rem(q + WEIGHT_TILES_AHEAD, WEIGHT_SLOTS)):
                copy.start()
            for copy in weight_copies(i, c, wslot):
                copy.wait()
            for r in range(FFN_GATHER_ROWS_PER_STEP):
                next_row(c * FFN_GATHER_ROWS_PER_STEP + r).start(priority=row_priority)
            xb = xb_ref[...]
            gate = jnp.dot(xb, wg_buf[wslot].astype(BF16), preferred_element_type=F32)
            up = jnp.dot(xb, wu_buf[wslot].astype(BF16), preferred_element_type=F32)
            hidden = (jax.nn.silu(gate) * up).astype(BF16)
            acc_ref[...] += jnp.dot(hidden, wd_buf[wslot].astype(BF16), preferred_element_type=F32)
            return 0
        lax.fori_loop(0, FFN_STEPS, tile, 0)

        @pl.when(i == n_used - 1)
        def _drain():
            wait_rows(1 - slot)
            for k in range(WEIGHT_TILES_AHEAD):
                for copy in weight_copies(next_block, k, lax.rem((i + 1) * FFN_STEPS + k, WEIGHT_SLOTS)):
                    copy.wait()
        _store_row_tiles(o_ref, acc_ref[...])

    @pl.when(jnp.logical_not(used))
    def _empty():
        o_ref[...] = jnp.zeros_like(o_ref)


def _expert_ffn(block_e, n_used, x_tiles, w_gate, w_up, w_down, src):
    n_blocks = block_e.shape[0]
    dummy = jnp.broadcast_to(jnp.arange(FFN_GATHER_ROWS - FFN_ROWS, dtype=jnp.int32),
                             (n_blocks, 1, FFN_GATHER_ROWS - FFN_ROWS))
    src3 = jnp.concatenate([src.reshape(n_blocks, 1, FFN_ROWS), dummy], axis=-1)
    any_spec = pl.BlockSpec(memory_space=pl.ANY)
    in_specs = [pl.BlockSpec((1, 1, FFN_GATHER_ROWS), lambda i, be, nu: (0, 0, 0), memory_space=pltpu.SMEM),
                pl.BlockSpec((1, 1, FFN_GATHER_ROWS), lambda i, be, nu: (jnp.minimum(i + 1, n_blocks - 1), 0, 0),
                             memory_space=pltpu.SMEM),
                any_spec, any_spec, any_spec, any_spec]
    return pl.pallas_call(
        _expert_ffn_body,
        grid_spec=pltpu.PrefetchScalarGridSpec(
            num_scalar_prefetch=2,
            grid=(n_blocks,),
            in_specs=in_specs,
            out_specs=pl.BlockSpec((FFN_ROWS * SUBLANES, LANES), lambda i, be, nu: (i, 0)),
            scratch_shapes=[pltpu.VMEM((FFN_ROWS, D_MODEL), BF16), pltpu.VMEM((FFN_ROWS, D_MODEL), F32),
                            pltpu.VMEM((2, FFN_GATHER_ROWS * SUBLANES, LANES), F32),
                            pltpu.VMEM((WEIGHT_SLOTS, D_MODEL, FFN_COLS), F32),
                            pltpu.VMEM((WEIGHT_SLOTS, D_MODEL, FFN_COLS), F32),
                            pltpu.VMEM((WEIGHT_SLOTS, FFN_COLS, D_MODEL), F32),
                            pltpu.SemaphoreType.DMA((2,)), pltpu.SemaphoreType.DMA((WEIGHT_SLOTS,))]),
        out_shape=jax.ShapeDtypeStruct((n_blocks * FFN_ROWS * SUBLANES, LANES), F32),
        compiler_params=_params("arbitrary"),
        name="swiglu_experts",
    )(block_e, n_used, src3, src3, x_tiles, w_gate, w_up, w_down)


def _route(x, w_ref, meta_ref, gate_ref, cnt_ref, run_ref):
    tm = x.shape[0]

    @pl.when(pl.program_id(0) == 0)
    def _init():
        run_ref[...] = jnp.zeros_like(run_ref)

    lane_i = lax.broadcasted_iota(jnp.int32, (tm, LANES), 1)
    lane = lane_i.astype(F32)
    logits = jnp.full((tm, LANES), -jnp.inf, F32)
    for e in range(N_EXPERTS):
        logit_e = jnp.sum(x * w_ref[e:e + 1, :], axis=-1, keepdims=True)
        logits = jnp.where(lane_i == e, logit_e, logits)
    m1 = jnp.max(logits, axis=-1, keepdims=True)
    e1 = jnp.min(jnp.where(logits == m1, lane, float(LANES)), axis=-1, keepdims=True)
    rest = jnp.where(lane == e1, -jnp.inf, logits)
    m2 = jnp.max(rest, axis=-1, keepdims=True)
    e2 = jnp.min(jnp.where(rest == m2, lane, float(LANES)), axis=-1, keepdims=True)
    ex = jnp.exp(m2 - m1)
    g1 = 1.0 / (1.0 + ex)
    g2 = ex / (1.0 + ex)

    chosen = jnp.logical_or(lane == e1, lane == e2)
    r_i = lax.broadcasted_iota(jnp.int32, (tm, tm), 0)
    c_i = lax.broadcasted_iota(jnp.int32, (tm, tm), 1)
    earlier = (c_i < r_i).astype(BF16)
    before = jnp.dot(earlier, chosen.astype(BF16), preferred_element_type=F32) + run_ref[...]
    rank1 = jnp.sum(jnp.where(lane == e1, before, 0.0), axis=-1, keepdims=True)
    rank2 = jnp.sum(jnp.where(lane == e2, before, 0.0), axis=-1, keepdims=True)
    total = run_ref[...] + jnp.sum(chosen.astype(F32), axis=0, keepdims=True)
    run_ref[...] = total
    cnt_ref[...] = jnp.broadcast_to(total, cnt_ref.shape)
    meta = jnp.where(lane_i == 0, e1, jnp.where(lane_i == 1, e2, jnp.where(lane_i == 2, rank1, rank2)))
    meta_ref[...] = meta[:, :ROUTE_META].astype(jnp.int32)
    gate_ref[...] = jnp.where(lane_i == 0, g1, g2)


def _combine_body(d0_ref, dn_ref, ys_hbm, gate_ref, x_ref, g_ref, b_ref, o_ref, buf_ref, sem, *, alpha):
    i = pl.program_id(0)
    slot = lax.rem(i, 2)
    n_rows = buf_ref.shape[1] // SUBLANES

    def row_copy(d_ref, to_slot):
        def copy(r):
            return pltpu.make_async_copy(ys_hbm.at[_tile_window(d_ref[0, 0, r])],
                                         buf_ref.at[to_slot, _tile_window(r)], sem.at[to_slot])
        return copy

    def wait_slot(which):
        pltpu.make_async_copy(buf_ref.at[which], buf_ref.at[which], sem.at[which]).wait()

    @pl.when(i == 0)
    def _first_rows():
        _start_row_copies(n_rows, row_copy(d0_ref, 0), priorities=(0, 1))

    wait_slot(slot)
    next_copy = row_copy(dn_ref, 1 - slot)
    for r in range(n_rows):
        next_copy(r).start(priority=r % 2)

    gates = gate_ref[...]
    g1, g2 = gates[:, 0:1], gates[:, 1:2]
    first = _load_row_tiles(buf_ref.at[slot], 0, MOVE_ROWS)
    second = _load_row_tiles(buf_ref.at[slot], MOVE_ROWS, MOVE_ROWS)
    f = jnp.concatenate([g1 * a + g2 * b for a, b in zip(first, second)], axis=-1)
    o_ref[...] = _layer_norm(alpha * x_ref[...] + f, g_ref[...], b_ref[...])

    @pl.when(i == pl.num_programs(0) - 1)
    def _drain():
        wait_slot(1 - slot)


def _combine(ys, dest1, dest2, gates, x, g, b, alpha):
    t = x.shape[0]
    steps = t // MOVE_ROWS
    dest = jnp.concatenate([dest1.reshape(steps, 1, MOVE_ROWS), dest2.reshape(steps, 1, MOVE_ROWS)], axis=-1)
    idx_shape = (1, 1, TOP_K * MOVE_ROWS)
    row = lambda i: (i, 0)
    const = lambda i: (0, 0)
    return pl.pallas_call(
        functools.partial(_combine_body, alpha=alpha),
        grid=(steps,),
        in_specs=[pl.BlockSpec(idx_shape, lambda i: (0, 0, 0), memory_space=pltpu.SMEM),
                  pl.BlockSpec(idx_shape, lambda i: (jnp.minimum(i + 1, steps - 1), 0, 0), memory_space=pltpu.SMEM),
                  pl.BlockSpec(memory_space=pl.ANY),
                  pl.BlockSpec((MOVE_ROWS, LANES), row), pl.BlockSpec((MOVE_ROWS, D_MODEL), row),
                  pl.BlockSpec((1, D_MODEL), const), pl.BlockSpec((1, D_MODEL), const)],
        out_specs=pl.BlockSpec((MOVE_ROWS, D_MODEL), row),
        out_shape=jax.ShapeDtypeStruct((t, D_MODEL), F32),
        scratch_shapes=[pltpu.VMEM((2, TOP_K * MOVE_ROWS * SUBLANES, LANES), F32), pltpu.SemaphoreType.DMA((2,))],
        compiler_params=_params("arbitrary"),
        name="combine_ln",
    )(dest, dest, ys, gates, x, g.reshape(1, -1), b.reshape(1, -1))


def _invert_body(d1_ref, d2_ref, fill_hbm, src_ref, sem):
    fill = pltpu.make_async_copy(fill_hbm, src_ref, sem)
    fill.start()
    fill.wait()

    def place(t, _):
        src_ref[d1_ref[t]] = t
        src_ref[d2_ref[t]] = t
        return 0
    lax.fori_loop(0, d1_ref.shape[0], place, 0, unroll=SCALAR_LOOP_UNROLL)


def _invert_placement(dest1, dest2, n_rows):
    smem = pl.BlockSpec(memory_space=pltpu.SMEM)
    return pl.pallas_call(
        _invert_body,
        in_specs=[smem, smem, pl.BlockSpec(memory_space=pl.ANY)],
        out_specs=smem,
        out_shape=jax.ShapeDtypeStruct((n_rows,), jnp.int32),
        scratch_shapes=[pltpu.SemaphoreType.DMA(())],
        name="invert_placement",
    )(dest1, dest2, jnp.arange(n_rows, dtype=jnp.int32) % dest1.shape[0])


def _moe(x1, x1_tiles, meta, gates, counts, w_gate, w_up, w_down, first_expert, g, b, alpha):
    t = x1.shape[0]
    e1, e2, rank1, rank2 = meta[:, 0], meta[:, 1], meta[:, 2], meta[:, 3]
    sizes = counts[0, :N_EXPERTS].astype(jnp.int32)
    padded = (sizes + FFN_ROWS - 1) // FFN_ROWS * FFN_ROWS
    group_end = jnp.cumsum(padded)
    group_start = group_end - padded
    dest1 = group_start[e1] + rank1
    dest2 = group_start[e2] + rank2
    n_blocks = (t * TOP_K) // FFN_ROWS + N_EXPERTS
    block_start = jnp.arange(n_blocks, dtype=jnp.int32) * FFN_ROWS
    block_e = jnp.minimum(jnp.sum(group_end[None, :] <= block_start[:, None], axis=1), N_EXPERTS - 1)
    n_used = (group_end[-1] // FFN_ROWS).reshape(1)
    src = _invert_placement(dest1, dest2, n_blocks * FFN_ROWS)
    ys = _expert_ffn((block_e + first_expert).astype(jnp.int32), n_used.astype(jnp.int32), x1_tiles, w_gate,
                     w_up, w_down, src)
    return _combine(ys, dest1, dest2, gates, x1, g, b, alpha)


def kernel(x, w_in, w_out, attn_lambda, attn_subln_g, pool_w, pool_scale, conv_w, conv_b, lru_wa, lru_ba,
           lru_wi, lru_bi, lru_lambda, ln1_g, ln1_b, ln2_g, ln2_b, ffn_w_gate, ffn_w_up, ffn_w_down,
           router_w, moe_w_gate, moe_w_up, moe_w_down):
    batch, seq, d = x.shape
    depth = w_in.shape[0]
    assert d == D_MODEL and seq % ATTN_TILE == 0 and seq % MIX_ROWS == 0
    t = batch * seq
    assert t % FFN_ROWS == 0 and t % ROW_TILE == 0
    alpha = (2.0 * depth) ** 0.25
    moe_gate = moe_w_gate.reshape(-1, D_MODEL, D_FF)
    moe_up = moe_w_up.reshape(-1, D_MODEL, D_FF)
    moe_down = moe_w_down.reshape(-1, D_FF, D_MODEL)
    xt = x.reshape(t, d)
    for l in range(depth):
        lambda_init = 0.8 - 0.6 * math.exp(-0.3 * l)
        q, k, v, rest = _inproj(xt, w_in, l)
        y_attn = _attention(q, k, v, attn_lambda[l], attn_subln_g[l], lambda_init, batch, seq)
        y_mix = _mixer(rest, pool_w[l], pool_scale[l], conv_w[l], conv_b[l], lru_wa[l], lru_ba[l],
                       lru_wi[l], lru_bi[l], lru_lambda[l], batch, seq)
        if l % 2 == 0:
            x1, = _outproj(y_attn, y_mix, xt, w_out, l, ln1_g[l], ln1_b[l], alpha)
            xt = _dense_ffn(x1, ffn_w_gate, ffn_w_up, ffn_w_down, l // 2, ln2_g[l], ln2_b[l], alpha)
        else:
            x1, x1_tiles, meta, gates, counts = _outproj(y_attn, y_mix, xt, w_out, l, ln1_g[l], ln1_b[l], alpha,
                                                         w_router=router_w[l // 2])
            xt = _moe(x1, x1_tiles, meta, gates, counts, moe_gate, moe_up, moe_down, (l // 2) * N_EXPERTS,
                      ln2_g[l], ln2_b[l], alpha)
    return xt.reshape(batch, seq, d)
```

```python
import functools
import math

import jax
import jax.numpy as jnp
from jax import lax
from jax.experimental import pallas as pl
from jax.experimental.pallas import tpu as pltpu

F32 = jnp.float32
BF16 = jnp.bfloat16

D_MODEL = 1024
CHUNK = 64
ATTN_HEADS = 4
ATTN_WIDTH = 512
ATTN_HEAD_DIM = 64
HEAD_WIDTH = 2 * ATTN_HEAD_DIM
POOL_WINDOWS = (2, 4, 8, 16)
POOL_WIDTH = 256
POOL_GROUP_DIM = 64
LRU_WIDTH = 256
LRU_C = 8.0
CONV_WIDTH = 4
REST_WIDTH = POOL_WIDTH + 2 * LRU_WIDTH
IN_WIDTH = 3 * ATTN_WIDTH + REST_WIDTH
MIX_WIDTH = POOL_WIDTH + LRU_WIDTH
D_FF = 2816
N_EXPERTS = 8
TOP_K = 2
LN_EPS = 1e-5
HEAD_NORM_EPS = 1e-5

LANES = 128
SUBLANES = 8
VMEM_LIMIT_BYTES = 56 * 1024 * 1024

ROW_TILE = 512
ATTN_TILE = 256
ATTN_HEADS_PER_STEP = 4
ATTN_SUM_ROWS = 16
ATTN_SCORE_SCALE = ATTN_HEAD_DIM ** -0.5 * math.log2(math.e)
MIX_ROWS = 256
FFN_ROWS = 1024
FFN_COLS = 256
FFN_STEPS = D_FF // FFN_COLS
WEIGHT_TILES_AHEAD = 3
WEIGHT_SLOTS = WEIGHT_TILES_AHEAD + 1
FFN_GATHER_ROWS_PER_STEP = -(-FFN_ROWS // ((FFN_STEPS - 1) * SUBLANES)) * SUBLANES
FFN_GATHER_ROWS = FFN_GATHER_ROWS_PER_STEP * FFN_STEPS
MOVE_ROWS = 256
ROUTE_META = 8
SCALAR_LOOP_UNROLL = 8
DMA_ISSUE_UNROLL = 8
NEG_BIG = -1e30
F32_TINY = float(jnp.finfo(jnp.float32).tiny)
assert MIX_ROWS >= max(POOL_WINDOWS)


def _params(*semantics):
    return pltpu.CompilerParams(dimension_semantics=semantics, vmem_limit_bytes=VMEM_LIMIT_BYTES)


def _layer_norm(z, g, b):
    mu = jnp.mean(z, axis=-1, keepdims=True)
    zc = z - mu
    var = jnp.mean(zc * zc, axis=-1, keepdims=True)
    return zc * lax.rsqrt(var + LN_EPS) * g + b


def _inproj_body(x_ref, w_ref, q_ref, k_ref, v_ref, r_ref, wbf_ref):
    @pl.when(pl.program_id(0) == 0)
    def _cast_weights():
        for c in range(0, IN_WIDTH, 256):
            wbf_ref[:, c:c + 256] = w_ref[0, :, c:c + 256].astype(BF16)

    xb = x_ref[...].astype(BF16)

    def proj(c0, c1):
        return jnp.dot(xb, wbf_ref[:, c0:c1], preferred_element_type=F32)

    q_ref[...] = (proj(0, ATTN_WIDTH) * ATTN_SCORE_SCALE).astype(BF16)
    k_ref[...] = proj(ATTN_WIDTH, 2 * ATTN_WIDTH).astype(BF16)
    v_ref[...] = proj(2 * ATTN_WIDTH, 3 * ATTN_WIDTH).astype(BF16)
    r_ref[...] = proj(3 * ATTN_WIDTH, IN_WIDTH)


def _inproj(x, w, layer):
    t = x.shape[0]
    row = lambda i: (i, 0)
    return pl.pallas_call(
        _inproj_body,
        grid=(t // ROW_TILE,),
        in_specs=[pl.BlockSpec((ROW_TILE, D_MODEL), row),
                  pl.BlockSpec((1, D_MODEL, IN_WIDTH), lambda i: (layer, 0, 0))],
        out_specs=[pl.BlockSpec((ROW_TILE, ATTN_WIDTH), row)] * 3
        + [pl.BlockSpec((ROW_TILE, REST_WIDTH), row)],
        out_shape=[jax.ShapeDtypeStruct((t, ATTN_WIDTH), BF16)] * 3
        + [jax.ShapeDtypeStruct((t, REST_WIDTH), F32)],
        scratch_shapes=[pltpu.VMEM((D_MODEL, IN_WIDTH), BF16)],
        compiler_params=_params("arbitrary"),
        name="inproj",
    )(x, w)


def _attn_body(lam_ref, g_ref, q_ref, k_ref, v_ref, o_ref, vt_ref, *, lambda_init, seq):
    tq = ATTN_TILE
    lv = lam_ref[...]
    lam = (jnp.exp(jnp.sum(lv[0:1] * lv[1:2], axis=-1, keepdims=True))
           - jnp.exp(jnp.sum(lv[2:3] * lv[3:4], axis=-1, keepdims=True)) + lambda_init)
    gain = g_ref[...] * (1.0 - lambda_init)

    heads = range(ATTN_HEADS_PER_STEP)
    cols = [slice(g * HEAD_WIDTH, (g + 1) * HEAD_WIDTH) for g in heads]
    for g in heads:
        for c in range(seq // tq):
            vt_ref[g, c, 0:HEAD_WIDTH, :] = v_ref[c * tq:(c + 1) * tq, cols[g]].astype(F32).T.astype(BF16)
            vt_ref[g, c, HEAD_WIDTH:, :] = jnp.ones((ATTN_SUM_ROWS, tq), BF16)

    lane = lax.broadcasted_iota(jnp.int32, (tq, HEAD_WIDTH), 1)
    first_map = lane < ATTN_HEAD_DIM
    key = lax.broadcasted_iota(jnp.int32, (tq, 2 * tq), 0)
    qry = lax.broadcasted_iota(jnp.int32, (tq, 2 * tq), 1)
    q_chunk = jnp.where(qry >= tq, qry - tq, qry) // CHUNK
    visible = (key // CHUNK) <= q_chunk

    def q_block(i, _):
        q0 = pl.multiple_of(i * tq, tq)
        qqs = []
        for g in heads:
            q = q_ref[pl.ds(q0, tq), cols[g]]
            zero = jnp.zeros_like(q)
            qqs.append(jnp.concatenate([jnp.where(first_map, q, zero), jnp.where(first_map, zero, q)], axis=0))

        def step(j, carry, masked):
            k0 = pl.multiple_of(j * tq, tq)
            out = []
            scores = [lax.dot_general(k_ref[pl.ds(k0, tq), cols[g]], qqs[g], (((1,), (1,)), ((), ())),
                                      preferred_element_type=F32) for g in heads]
            for g in heads:
                m, acc = carry[g]
                s = scores[g]
                if masked:
                    s = jnp.where(visible, s, NEG_BIG)
                m_new = jnp.maximum(m, jnp.max(s, axis=0, keepdims=True))
                p = jnp.exp2(s - m_new)
                scale = jnp.exp2(m - m_new)
                pv = jnp.dot(vt_ref[g, j], p.astype(BF16), preferred_element_type=F32)
                out.append((m_new, scale * acc + pv))
            return tuple(out)

        init = tuple((jnp.full((1, 2 * tq), NEG_BIG, F32),
                      jnp.zeros((HEAD_WIDTH + ATTN_SUM_ROWS, 2 * tq), F32)) for _ in heads)
        carry = lax.fori_loop(0, i, lambda j, c: step(j, c, False), init)
        carry = step(i, carry, True)
        for g in heads:
            _, acc = carry[g]
            l = acc[HEAD_WIDTH:HEAD_WIDTH + 1, :]
            acc = acc[:HEAD_WIDTH, :]
            o = acc[:, :tq] / l[:, :tq] - lam * (acc[:, tq:] / l[:, tq:])
            o = o * lax.rsqrt(jnp.mean(o * o, axis=0, keepdims=True) + HEAD_NORM_EPS)
            o_ref[pl.ds(q0, tq), cols[g]] = (o.T * gain).astype(BF16)
        return 0

    lax.fori_loop(0, seq // tq, q_block, 0)


def _attention(q, k, v, lam_params, subln_g, lambda_init, batch, seq):
    t = q.shape[0]
    blk = pl.BlockSpec((seq, ATTN_HEADS_PER_STEP * HEAD_WIDTH), lambda b, h: (b, h))
    return pl.pallas_call(
        functools.partial(_attn_body, lambda_init=lambda_init, seq=seq),
        grid=(batch, ATTN_HEADS // ATTN_HEADS_PER_STEP),
        in_specs=[pl.BlockSpec((4, ATTN_HEAD_DIM), lambda b, h: (0, 0)),
                  pl.BlockSpec((1, HEAD_WIDTH), lambda b, h: (0, 0)),
                  blk, blk, blk],
        out_specs=blk,
        out_shape=jax.ShapeDtypeStruct((t, ATTN_WIDTH), BF16),
        scratch_shapes=[pltpu.VMEM((ATTN_HEADS_PER_STEP, seq // ATTN_TILE, HEAD_WIDTH + ATTN_SUM_ROWS, ATTN_TILE),
                                   BF16)],
        compiler_params=_params("arbitrary", "arbitrary"),
        name="diff_attention",
    )(lam_params, subln_g.reshape(1, HEAD_WIDTH), q, k, v)


def _mixer_body(r_ref, pw_ref, ps_ref, cw_ref, cb_ref, wa_ref, ba_ref, wi_ref, bi_ref, lam_ref, y_ref,
                a_ref, b_ref, h_ref, inv_ref, *, seq):
    rows = MIX_ROWS
    pool_hist = max(POOL_WINDOWS)
    conv_hist = SUBLANES
    pw = pw_ref[...]
    wa = wa_ref[...]
    wi = wi_ref[...]
    ps, cb, ba, bi = ps_ref[...], cb_ref[...], ba_ref[...], bi_ref[...]
    cw = cw_ref[...]
    neg_c_softplus = -LRU_C * jax.nn.softplus(-lam_ref[...])

    lane_e = lax.broadcasted_iota(jnp.int32, (rows + pool_hist, POOL_WIDTH), 1)
    lane = lax.broadcasted_iota(jnp.int32, (rows, POOL_WIDTH), 1)
    row = lax.broadcasted_iota(jnp.int32, (rows, POOL_WIDTH), 0)
    win = jnp.where(lane < 64, 2, jnp.where(lane < 128, 4, jnp.where(lane < 192, 8, 16)))
    inv_win = jnp.where(lane < 64, 1 / 2, jnp.where(lane < 128, 1 / 4, jnp.where(lane < 192, 1 / 8, 1 / 16)))
    inv_ref[...] = 1.0 / (row + 1).astype(F32)
    groups = rows // SUBLANES
    row_in_group = lax.broadcasted_iota(jnp.int32, (groups, SUBLANES, LRU_WIDTH), 1)
    group = lax.broadcasted_iota(jnp.int32, (groups, LRU_WIDTH), 0)

    def chunk(c, carry):
        tail_u, tail_x, h_prev = carry
        r0 = pl.multiple_of(c * rows, rows)
        u = r_ref[pl.ds(r0, rows), 0:POOL_WIDTH]
        xr = r_ref[pl.ds(r0, rows), POOL_WIDTH:POOL_WIDTH + LRU_WIDTH]
        xg = r_ref[pl.ds(r0, rows), POOL_WIDTH + LRU_WIDTH:REST_WIDTH]

        ue = jnp.concatenate([tail_u, u], axis=0)
        w2 = ue + pltpu.roll(ue, 1, 0)
        w4 = w2 + pltpu.roll(w2, 2, 0)
        w8 = w4 + pltpu.roll(w4, 4, 0)
        w16 = w8 + pltpu.roll(w8, 8, 0)
        ws = jnp.where(lane_e < 64, w2, jnp.where(lane_e < 128, w4, jnp.where(lane_e < 192, w8, w16)))
        ws = ws[pool_hist:]
        inv_count = jnp.where(r0 + row + 1 >= win, inv_win, inv_ref[...])
        pooled = ws * inv_count - u
        y_pool = jnp.dot(pooled.astype(BF16), pw, preferred_element_type=F32) * ps

        xe = jnp.concatenate([tail_x, xr], axis=0)
        xc = cb + pltpu.roll(xe, 3, 0)[conv_hist:] * cw[0:1]
        xc = xc + pltpu.roll(xe, 2, 0)[conv_hist:] * cw[1:2]
        xc = xc + pltpu.roll(xe, 1, 0)[conv_hist:] * cw[2:3]
        xc = xc + xr * cw[3:4]
        xcb = xc.astype(BF16)
        r_gate = jax.nn.sigmoid(jnp.dot(xcb, wa, preferred_element_type=F32) + ba)
        i_gate = jax.nn.sigmoid(jnp.dot(xcb, wi, preferred_element_type=F32) + bi)
        log_a = r_gate * neg_c_softplus
        a = jnp.exp(log_a)
        gap = -jnp.tanh(log_a) * (a * a + 1.0)
        b = (gap * lax.rsqrt(jnp.maximum(gap, F32_TINY))) * (i_gate * xc)

        def doubling(a, b, pos, length, axis):
            s = 1
            while s < length:
                keep = pos >= s
                a_prev = jnp.where(keep, pltpu.roll(a, s, axis), 1.0)
                b_prev = jnp.where(keep, pltpu.roll(b, s, axis), 0.0)
                b = a * b_prev + b
                a = a * a_prev
                s *= 2
            return a, b

        a, b = doubling(a.reshape(groups, SUBLANES, LRU_WIDTH), b.reshape(groups, SUBLANES, LRU_WIDTH),
                        row_in_group, SUBLANES, 1)
        a = a.reshape(rows, LRU_WIDTH)
        b = b.reshape(rows, LRU_WIDTH)
        halves = range(LRU_WIDTH // LANES)
        ends = pl.ds(SUBLANES - 1, groups, stride=SUBLANES)
        for k in halves:
            a_ref[k] = a[:, k * LANES:(k + 1) * LANES]
            b_ref[k] = b[:, k * LANES:(k + 1) * LANES]
        a_end = jnp.concatenate([a_ref[k, ends, :] for k in halves], axis=-1)
        b_end = jnp.concatenate([b_ref[k, ends, :] for k in halves], axis=-1)
        a_end, b_end = doubling(a_end, b_end, group, groups, 0)
        h_end = a_end * h_prev + b_end
        h_ref[...] = jnp.where(group == 0, h_prev, pltpu.roll(h_end, 1, 0))
        h_start = jnp.concatenate([jnp.broadcast_to(h_ref[g:g + 1, :], (SUBLANES, LRU_WIDTH))
                                   for g in range(groups)], axis=0)
        h = a * h_start + b
        y_lru = h * jax.nn.gelu(xg)

        y_ref[pl.ds(r0, rows), 0:POOL_WIDTH] = y_pool.astype(BF16)
        y_ref[pl.ds(r0, rows), POOL_WIDTH:MIX_WIDTH] = y_lru.astype(BF16)
        return u[rows - pool_hist:], xr[rows - conv_hist:], h_end[groups - 1:groups]

    init = (jnp.zeros((pool_hist, POOL_WIDTH), F32), jnp.zeros((conv_hist, LRU_WIDTH), F32),
            jnp.zeros((1, LRU_WIDTH), F32))
    lax.fori_loop(0, seq // rows, chunk, init)


def _block_diag(w):
    g, c, d = w.shape
    eye = jnp.eye(g, dtype=w.dtype)
    return (eye[:, None, :, None] * w[:, :, None, :]).reshape(g * c, g * d)


def _mixer(rest, pool_w, pool_scale, conv_w, conv_b, wa, ba, wi, bi, lru_lambda, batch, seq):
    t = rest.shape[0]
    full = lambda shape: pl.BlockSpec(shape, lambda b: (0, 0))
    vec = lambda a: a.reshape(1, -1)
    return pl.pallas_call(
        functools.partial(_mixer_body, seq=seq),
        grid=(batch,),
        in_specs=[pl.BlockSpec((seq, REST_WIDTH), lambda b: (b, 0)),
                  full((POOL_WIDTH, POOL_WIDTH)), full((1, POOL_WIDTH)),
                  full((CONV_WIDTH, LRU_WIDTH)), full((1, LRU_WIDTH)),
                  full((LRU_WIDTH, LRU_WIDTH)), full((1, LRU_WIDTH)),
                  full((LRU_WIDTH, LRU_WIDTH)), full((1, LRU_WIDTH)),
                  full((1, LRU_WIDTH))],
        out_specs=pl.BlockSpec((seq, MIX_WIDTH), lambda b: (b, 0)),
        out_shape=jax.ShapeDtypeStruct((t, MIX_WIDTH), BF16),
        scratch_shapes=[pltpu.VMEM((LRU_WIDTH // LANES, MIX_ROWS, LANES), F32),
                        pltpu.VMEM((LRU_WIDTH // LANES, MIX_ROWS, LANES), F32),
                        pltpu.VMEM((MIX_ROWS // SUBLANES, LRU_WIDTH), F32),
                        pltpu.VMEM((MIX_ROWS, POOL_WIDTH), F32)],
        compiler_params=_params("arbitrary"),
        name="pool_lru_mixer",
    )(rest, _block_diag(pool_w).astype(BF16), vec(pool_scale), conv_w, vec(conv_b),
      _block_diag(wa).astype(BF16), vec(ba), _block_diag(wi).astype(BF16), vec(bi), vec(lru_lambda))


def _outproj_body(ya_ref, ym_ref, x_ref, w_ref, g_ref, b_ref, *rest, alpha, route):
    if route:
        wr_ref, o_ref, ot_ref, meta_ref, gate_ref, cnt_ref, wbf_ref, run_ref = rest
    else:
        o_ref, wbf_ref = rest

    @pl.when(pl.program_id(0) == 0)
    def _cast_weights():
        for c in range(0, D_MODEL, 256):
            wbf_ref[:, c:c + 256] = w_ref[0, :, c:c + 256].astype(BF16)

    mix = jnp.dot(ya_ref[...], wbf_ref[0:ATTN_WIDTH, :], preferred_element_type=F32)
    mix = mix + jnp.dot(ym_ref[...], wbf_ref[ATTN_WIDTH:, :], preferred_element_type=F32)
    x1 = _layer_norm(alpha * x_ref[...] + mix, g_ref[...], b_ref[...])
    o_ref[...] = x1
    if route:
        _store_row_tiles(ot_ref, x1)
        _route(x1, wr_ref, meta_ref, gate_ref, cnt_ref, run_ref)


def _outproj(y_attn, y_mix, x, w, layer, g, b, alpha, w_router=None):
    t = x.shape[0]
    route = w_router is not None
    row = lambda i: (i, 0)
    const = lambda i: (0, 0)
    in_specs = [pl.BlockSpec((ROW_TILE, ATTN_WIDTH), row), pl.BlockSpec((ROW_TILE, MIX_WIDTH), row),
                pl.BlockSpec((ROW_TILE, D_MODEL), row),
                pl.BlockSpec((1, D_MODEL, D_MODEL), lambda i: (layer, 0, 0)),
                pl.BlockSpec((1, D_MODEL), const), pl.BlockSpec((1, D_MODEL), const)]
    args = [y_attn, y_mix, x, w, g.reshape(1, -1), b.reshape(1, -1)]
    out_specs = [pl.BlockSpec((ROW_TILE, D_MODEL), row)]
    out_shape = [jax.ShapeDtypeStruct((t, D_MODEL), F32)]
    scratch = [pltpu.VMEM((D_MODEL, D_MODEL), BF16)]
    if route:
        in_specs.append(pl.BlockSpec((N_EXPERTS, D_MODEL), const))
        args.append(w_router.T)
        out_specs += [pl.BlockSpec((ROW_TILE * SUBLANES, LANES), row),
                      pl.BlockSpec((ROW_TILE, ROUTE_META), row), pl.BlockSpec((ROW_TILE, LANES), row),
                      pl.BlockSpec((SUBLANES, LANES), const)]
        out_shape += [jax.ShapeDtypeStruct((t * SUBLANES, LANES), F32),
                      jax.ShapeDtypeStruct((t, ROUTE_META), jnp.int32), jax.ShapeDtypeStruct((t, LANES), F32),
                      jax.ShapeDtypeStruct((SUBLANES, LANES), F32)]
        scratch.append(pltpu.VMEM((1, LANES), F32))
    return pl.pallas_call(
        functools.partial(_outproj_body, alpha=alpha, route=route),
        grid=(t // ROW_TILE,),
        in_specs=in_specs,
        out_specs=out_specs,
        out_shape=out_shape,
        scratch_shapes=scratch,
        compiler_params=_params("arbitrary"),
        name="outproj_ln_route" if route else "outproj_ln",
    )(*args)


def _dense_ffn_body(x_ref, wg_hbm, wu_hbm, wd_hbm, g_ref, b_ref, o_ref,
                    wg_ref, wu_ref, wd_ref, stage_in_ref, stage_out_ref, xb_ref, acc_ref, sem, *, layer, alpha):
    n_chunks = 3 * FFN_STEPS

    def chunk_copy(k):
        which, c = divmod(k, FFN_STEPS)
        cols = pl.ds(c * FFN_COLS, FFN_COLS)
        if which < 2:
            src, dst = (wg_hbm, wu_hbm)[which].at[layer, :, cols], stage_in_ref.at[k % 2]
        else:
            src, dst = wd_hbm.at[layer, cols, :], stage_out_ref.at[k % 2]
        return pltpu.make_async_copy(src, dst, sem.at[k % 2])

    @pl.when(pl.program_id(0) == 0)
    def _load_weights():
        chunk_copy(0).start()
        for k in range(n_chunks):
            if k + 1 < n_chunks:
                chunk_copy(k + 1).start()
            chunk_copy(k).wait()
            which, c = divmod(k, FFN_STEPS)
            if which == 0:
                wg_ref[c] = stage_in_ref[k % 2].astype(BF16)
            elif which == 1:
                wu_ref[c] = stage_in_ref[k % 2].astype(BF16)
            else:
                wd_ref[c] = stage_out_ref[k % 2].astype(BF16)

    xb_ref[...] = x_ref[...].astype(BF16)
    acc_ref[...] = jnp.zeros_like(acc_ref)

    def tile(c, _):
        xb = xb_ref[...]
        gate = jnp.dot(xb, wg_ref[c], preferred_element_type=F32)
        up = jnp.dot(xb, wu_ref[c], preferred_element_type=F32)
        hidden = (jax.nn.silu(gate) * up).astype(BF16)
        acc_ref[...] += jnp.dot(hidden, wd_ref[c], preferred_element_type=F32)
        return 0
    lax.fori_loop(0, FFN_STEPS, tile, 0)
    o_ref[...] = _layer_norm(alpha * x_ref[...] + acc_ref[...], g_ref[...], b_ref[...])


def _dense_ffn(x, w_gate, w_up, w_down, layer, g, b, alpha):
    t = x.shape[0]
    row = lambda i: (i, 0)
    const = lambda i: (0, 0)
    any_spec = pl.BlockSpec(memory_space=pl.ANY)
    return pl.pallas_call(
        functools.partial(_dense_ffn_body, layer=layer, alpha=alpha),
        grid=(t // FFN_ROWS,),
        in_specs=[pl.BlockSpec((FFN_ROWS, D_MODEL), row), any_spec, any_spec, any_spec,
                  pl.BlockSpec((1, D_MODEL), const), pl.BlockSpec((1, D_MODEL), const)],
        out_specs=pl.BlockSpec((FFN_ROWS, D_MODEL), row),
        out_shape=jax.ShapeDtypeStruct((t, D_MODEL), F32),
        scratch_shapes=[pltpu.VMEM((FFN_STEPS, D_MODEL, FFN_COLS), BF16),
                        pltpu.VMEM((FFN_STEPS, D_MODEL, FFN_COLS), BF16),
                        pltpu.VMEM((FFN_STEPS, FFN_COLS, D_MODEL), BF16),
                        pltpu.VMEM((2, D_MODEL, FFN_COLS), F32), pltpu.VMEM((2, FFN_COLS, D_MODEL), F32),
                        pltpu.VMEM((FFN_ROWS, D_MODEL), BF16), pltpu.VMEM((FFN_ROWS, D_MODEL), F32),
                        pltpu.SemaphoreType.DMA((2,))],
        compiler_params=_params("arbitrary"),
        name="swiglu_ln",
    )(x, w_gate, w_up, w_down, g.reshape(1, -1), b.reshape(1, -1))


def _store_row_tiles(o_ref, val):
    rows = val.shape[0]
    for s in range(SUBLANES):
        o_ref[pl.ds(s, rows, stride=SUBLANES), :] = val[:, s * LANES:(s + 1) * LANES]


def _load_row_tiles(ref, first_row, rows):
    return [ref[pl.ds(first_row * SUBLANES + s, rows, stride=SUBLANES), :] for s in range(SUBLANES)]


def _tile_window(row):
    return pl.ds(pl.multiple_of(row * SUBLANES, SUBLANES), SUBLANES)


def _start_row_copies(n_rows, make_copy, priorities):
    k = len(priorities)

    def body(q, _):
        for p, priority in enumerate(priorities):
            make_copy(q * k + p).start(priority=priority)
        return 0
    lax.fori_loop(0, n_rows // k, body, 0, unroll=DMA_ISSUE_UNROLL // k)


def _gather_copies(src_ref, x_hbm, buf_ref, sem, slot):
    def copy(r):
        return pltpu.make_async_copy(x_hbm.at[_tile_window(src_ref[0, 0, r])], buf_ref.at[slot, _tile_window(r)],
                                     sem.at[slot])
    return copy


def _expert_ffn_body(be_ref, nu_ref, src0_ref, src_next_ref, x_hbm, wg_hbm, wu_hbm, wd_hbm, o_ref,
                     xb_ref, acc_ref, buf_ref, wg_buf, wu_buf, wd_buf, row_sem, w_sem):
    i = pl.program_id(0)
    n_used = nu_ref[0]
    used = i < n_used
    slot = lax.rem(i, 2)
    row_priority = 1

    def wait_rows(which):
        pltpu.make_async_copy(buf_ref.at[which], buf_ref.at[which], row_sem.at[which]).wait()

    def weight_copies(block, c, wslot):
        e = be_ref[block]
        cols = pl.ds(pl.multiple_of(c * FFN_COLS, FFN_COLS), FFN_COLS)
        return (pltpu.make_async_copy(wg_hbm.at[e, :, cols], wg_buf.at[wslot], w_sem.at[wslot]),
                pltpu.make_async_copy(wu_hbm.at[e, :, cols], wu_buf.at[wslot], w_sem.at[wslot]),
                pltpu.make_async_copy(wd_hbm.at[e, cols, :], wd_buf.at[wslot], w_sem.at[wslot]))

    @pl.when(jnp.logical_and(used, i == 0))
    def _first_copies():
        for k in range(WEIGHT_TILES_AHEAD):
            for copy in weight_copies(0, k, k):
                copy.start()
        _start_row_copies(FFN_GATHER_ROWS, _gather_copies(src0_ref, x_hbm, buf_ref, row_sem, 0),
                          priorities=(row_priority,))

    @pl.when(used)
    def _block():
        wait_rows(slot)
        for s, part in enumerate(_load_row_tiles(buf_ref.at[slot], 0, FFN_ROWS)):
            xb_ref[:, s * LANES:(s + 1) * LANES] = part.astype(BF16)
        acc_ref[...] = jnp.zeros_like(acc_ref)
        next_block = jnp.minimum(i + 1, n_used - 1)
        next_row = _gather_copies(src_next_ref, x_hbm, buf_ref, row_sem, 1 - slot)

        def tile(c, _):
            q = i * FFN_STEPS + c
            wslot = lax.rem(q, WEIGHT_SLOTS)
            ahead = c + WEIGHT_TILES_AHEAD
            wraps = ahead >= FFN_STEPS
            for copy in weight_copies(jnp.where(wraps, next_block, i), jnp.where(wraps, ahead - FFN_STEPS, ahead),
                                      lax.rem(q + WEIGHT_TILES_AHEAD, WEIGHT_SLOTS)):
                copy.start()
            for copy in weight_copies(i, c, wslot):
                copy.wait()
            for r in range(FFN_GATHER_ROWS_PER_STEP):
                next_row(c * FFN_GATHER_ROWS_PER_STEP + r).start(priority=row_priority)
            xb = xb_ref[...]
            gate = jnp.dot(xb, wg_buf[wslot].astype(BF16), preferred_element_type=F32)
            up = jnp.dot(xb, wu_buf[wslot].astype(BF16), preferred_element_type=F32)
            hidden = (jax.nn.silu(gate) * up).astype(BF16)
            acc_ref[...] += jnp.dot(hidden, wd_buf[wslot].astype(BF16), preferred_element_type=F32)
            return 0
        lax.fori_loop(0, FFN_STEPS, tile, 0)

        @pl.when(i == n_used - 1)
        def _drain():
            wait_rows(1 - slot)
            for k in range(WEIGHT_TILES_AHEAD):
                for copy in weight_copies(next_block, k, lax.rem((i + 1) * FFN_STEPS + k, WEIGHT_SLOTS)):
                    copy.wait()
        _store_row_tiles(o_ref, acc_ref[...])

    @pl.when(jnp.logical_not(used))
    def _empty():
        o_ref[...] = jnp.zeros_like(o_ref)


def _expert_ffn(block_e, n_used, x_tiles, w_gate, w_up, w_down, src):
    n_blocks = block_e.shape[0]
    dummy = jnp.broadcast_to(jnp.arange(FFN_GATHER_ROWS - FFN_ROWS, dtype=jnp.int32),
                             (n_blocks, 1, FFN_GATHER_ROWS - FFN_ROWS))
    src3 = jnp.concatenate([src.reshape(n_blocks, 1, FFN_ROWS), dummy], axis=-1)
    any_spec = pl.BlockSpec(memory_space=pl.ANY)
    in_specs = [pl.BlockSpec((1, 1, FFN_GATHER_ROWS), lambda i, be, nu: (0, 0, 0), memory_space=pltpu.SMEM),
                pl.BlockSpec((1, 1, FFN_GATHER_ROWS), lambda i, be, nu: (jnp.minimum(i + 1, n_blocks - 1), 0, 0),
                             memory_space=pltpu.SMEM),
                any_spec, any_spec, any_spec, any_spec]
    return pl.pallas_call(
        _expert_ffn_body,
        grid_spec=pltpu.PrefetchScalarGridSpec(
            num_scalar_prefetch=2,
            grid=(n_blocks,),
            in_specs=in_specs,
            out_specs=pl.BlockSpec((FFN_ROWS * SUBLANES, LANES), lambda i, be, nu: (i, 0)),
            scratch_shapes=[pltpu.VMEM((FFN_ROWS, D_MODEL), BF16), pltpu.VMEM((FFN_ROWS, D_MODEL), F32),
                            pltpu.VMEM((2, FFN_GATHER_ROWS * SUBLANES, LANES), F32),
                            pltpu.VMEM((WEIGHT_SLOTS, D_MODEL, FFN_COLS), F32),
                            pltpu.VMEM((WEIGHT_SLOTS, D_MODEL, FFN_COLS), F32),
                            pltpu.VMEM((WEIGHT_SLOTS, FFN_COLS, D_MODEL), F32),
                            pltpu.SemaphoreType.DMA((2,)), pltpu.SemaphoreType.DMA((WEIGHT_SLOTS,))]),
        out_shape=jax.ShapeDtypeStruct((n_blocks * FFN_ROWS * SUBLANES, LANES), F32),
        compiler_params=_params("arbitrary"),
        name="swiglu_experts",
    )(block_e, n_used, src3, src3, x_tiles, w_gate, w_up, w_down)


def _route(x, w_ref, meta_ref, gate_ref, cnt_ref, run_ref):
    tm = x.shape[0]

    @pl.when(pl.program_id(0) == 0)
    def _init():
        run_ref[...] = jnp.zeros_like(run_ref)

    lane_i = lax.broadcasted_iota(jnp.int32, (tm, LANES), 1)
    lane = lane_i.astype(F32)
    logits = jnp.full((tm, LANES), -jnp.inf, F32)
    for e in range(N_EXPERTS):
        logit_e = jnp.sum(x * w_ref[e:e + 1, :], axis=-1, keepdims=True)
        logits = jnp.where(lane_i == e, logit_e, logits)
    m1 = jnp.max(logits, axis=-1, keepdims=True)
    e1 = jnp.min(jnp.where(logits == m1, lane, float(LANES)), axis=-1, keepdims=True)
    rest = jnp.where(lane == e1, -jnp.inf, logits)
    m2 = jnp.max(rest, axis=-1, keepdims=True)
    e2 = jnp.min(jnp.where(rest == m2, lane, float(LANES)), axis=-1, keepdims=True)
    ex = jnp.exp(m2 - m1)
    g1 = 1.0 / (1.0 + ex)
    g2 = ex / (1.0 + ex)

    chosen = jnp.logical_or(lane == e1, lane == e2)
    r_i = lax.broadcasted_iota(jnp.int32, (tm, tm), 0)
    c_i = lax.broadcasted_iota(jnp.int32, (tm, tm), 1)
    earlier = (c_i < r_i).astype(BF16)
    before = jnp.dot(earlier, chosen.astype(BF16), preferred_element_type=F32) + run_ref[...]
    rank1 = jnp.sum(jnp.where(lane == e1, before, 0.0), axis=-1, keepdims=True)
    rank2 = jnp.sum(jnp.where(lane == e2, before, 0.0), axis=-1, keepdims=True)
    total = run_ref[...] + jnp.sum(chosen.astype(F32), axis=0, keepdims=True)
    run_ref[...] = total
    cnt_ref[...] = jnp.broadcast_to(total, cnt_ref.shape)
    meta = jnp.where(lane_i == 0, e1, jnp.where(lane_i == 1, e2, jnp.where(lane_i == 2, rank1, rank2)))
    meta_ref[...] = meta[:, :ROUTE_META].astype(jnp.int32)
    gate_ref[...] = jnp.where(lane_i == 0, g1, g2)


def _combine_body(d0_ref, dn_ref, ys_hbm, gate_ref, x_ref, g_ref, b_ref, o_ref, buf_ref, sem, *, alpha):
    i = pl.program_id(0)
    slot = lax.rem(i, 2)
    n_rows = buf_ref.shape[1] // SUBLANES

    def row_copy(d_ref, to_slot):
        def copy(r):
            return pltpu.make_async_copy(ys_hbm.at[_tile_window(d_ref[0, 0, r])],
                                         buf_ref.at[to_slot, _tile_window(r)], sem.at[to_slot])
        return copy

    def wait_slot(which):
        pltpu.make_async_copy(buf_ref.at[which], buf_ref.at[which], sem.at[which]).wait()

    @pl.when(i == 0)
    def _first_rows():
        _start_row_copies(n_rows, row_copy(d0_ref, 0), priorities=(0, 1))

    wait_slot(slot)
    next_copy = row_copy(dn_ref, 1 - slot)
    for r in range(n_rows):
        next_copy(r).start(priority=r % 2)

    gates = gate_ref[...]
    g1, g2 = gates[:, 0:1], gates[:, 1:2]
    first = _load_row_tiles(buf_ref.at[slot], 0, MOVE_ROWS)
    second = _load_row_tiles(buf_ref.at[slot], MOVE_ROWS, MOVE_ROWS)
    f = jnp.concatenate([g1 * a + g2 * b for a, b in zip(first, second)], axis=-1)
    o_ref[...] = _layer_norm(alpha * x_ref[...] + f, g_ref[...], b_ref[...])

    @pl.when(i == pl.num_programs(0) - 1)
    def _drain():
        wait_slot(1 - slot)


def _combine(ys, dest1, dest2, gates, x, g, b, alpha):
    t = x.shape[0]
    steps = t // MOVE_ROWS
    dest = jnp.concatenate([dest1.reshape(steps, 1, MOVE_ROWS), dest2.reshape(steps, 1, MOVE_ROWS)], axis=-1)
    idx_shape = (1, 1, TOP_K * MOVE_ROWS)
    row = lambda i: (i, 0)
    const = lambda i: (0, 0)
    return pl.pallas_call(
        functools.partial(_combine_body, alpha=alpha),
        grid=(steps,),
        in_specs=[pl.BlockSpec(idx_shape, lambda i: (0, 0, 0), memory_space=pltpu.SMEM),
                  pl.BlockSpec(idx_shape, lambda i: (jnp.minimum(i + 1, steps - 1), 0, 0), memory_space=pltpu.SMEM),
                  pl.BlockSpec(memory_space=pl.ANY),
                  pl.BlockSpec((MOVE_ROWS, LANES), row), pl.BlockSpec((MOVE_ROWS, D_MODEL), row),
                  pl.BlockSpec((1, D_MODEL), const), pl.BlockSpec((1, D_MODEL), const)],
        out_specs=pl.BlockSpec((MOVE_ROWS, D_MODEL), row),
        out_shape=jax.ShapeDtypeStruct((t, D_MODEL), F32),
        scratch_shapes=[pltpu.VMEM((2, TOP_K * MOVE_ROWS * SUBLANES, LANES), F32), pltpu.SemaphoreType.DMA((2,))],
        compiler_params=_params("arbitrary"),
        name="combine_ln",
    )(dest, dest, ys, gates, x, g.reshape(1, -1), b.reshape(1, -1))


def _invert_body(d1_ref, d2_ref, fill_hbm, src_ref, sem):
    fill = pltpu.make_async_copy(fill_hbm, src_ref, sem)
    fill.start()
    fill.wait()

    def place(t, _):
        src_ref[d1_ref[t]] = t
        src_ref[d2_ref[t]] = t
        return 0
    lax.fori_loop(0, d1_ref.shape[0], place, 0, unroll=SCALAR_LOOP_UNROLL)


def _invert_placement(dest1, dest2, n_rows):
    smem = pl.BlockSpec(memory_space=pltpu.SMEM)
    return pl.pallas_call(
        _invert_body,
        in_specs=[smem, smem, pl.BlockSpec(memory_space=pl.ANY)],
        out_specs=smem,
        out_shape=jax.ShapeDtypeStruct((n_rows,), jnp.int32),
        scratch_shapes=[pltpu.SemaphoreType.DMA(())],
        name="invert_placement",
    )(dest1, dest2, jnp.arange(n_rows, dtype=jnp.int32) % dest1.shape[0])


def _moe(x1, x1_tiles, meta, gates, counts, w_gate, w_up, w_down, first_expert, g, b, alpha):
    t = x1.shape[0]
    e1, e2, rank1, rank2 = meta[:, 0], meta[:, 1], meta[:, 2], meta[:, 3]
    sizes = counts[0, :N_EXPERTS].astype(jnp.int32)
    padded = (sizes + FFN_ROWS - 1) // FFN_ROWS * FFN_ROWS
    group_end = jnp.cumsum(padded)
    group_start = group_end - padded
    dest1 = group_start[e1] + rank1
    dest2 = group_start[e2] + rank2
    n_blocks = (t * TOP_K) // FFN_ROWS + N_EXPERTS
    block_start = jnp.arange(n_blocks, dtype=jnp.int32) * FFN_ROWS
    block_e = jnp.minimum(jnp.sum(group_end[None, :] <= block_start[:, None], axis=1), N_EXPERTS - 1)
    n_used = (group_end[-1] // FFN_ROWS).reshape(1)
    src = _invert_placement(dest1, dest2, n_blocks * FFN_ROWS)
    ys = _expert_ffn((block_e + first_expert).astype(jnp.int32), n_used.astype(jnp.int32), x1_tiles, w_gate,
                     w_up, w_down, src)
    return _combine(ys, dest1, dest2, gates, x1, g, b, alpha)


def kernel(x, w_in, w_out, attn_lambda, attn_subln_g, pool_w, pool_scale, conv_w, conv_b, lru_wa, lru_ba,
           lru_wi, lru_bi, lru_lambda, ln1_g, ln1_b, ln2_g, ln2_b, ffn_w_gate, ffn_w_up, ffn_w_down,
           router_w, moe_w_gate, moe_w_up, moe_w_down):
    batch, seq, d = x.shape
    depth = w_in.shape[0]
    assert d == D_MODEL and seq % ATTN_TILE == 0 and seq % MIX_ROWS == 0
    t = batch * seq
    assert t % FFN_ROWS == 0 and t % ROW_TILE == 0
    alpha = (2.0 * depth) ** 0.25
    moe_gate = moe_w_gate.reshape(-1, D_MODEL, D_FF)
    moe_up = moe_w_up.reshape(-1, D_MODEL, D_FF)
    moe_down = moe_w_down.reshape(-1, D_FF, D_MODEL)
    xt = x.reshape(t, d)
    for l in range(depth):
        lambda_init = 0.8 - 0.6 * math.exp(-0.3 * l)
        q, k, v, rest = _inproj(xt, w_in, l)
        y_attn = _attention(q, k, v, attn_lambda[l], attn_subln_g[l], lambda_init, batch, seq)
        y_mix = _mixer(rest, pool_w[l], pool_scale[l], conv_w[l], conv_b[l], lru_wa[l], lru_ba[l],
                       lru_wi[l], lru_bi[l], lru_lambda[l], batch, seq)
        if l % 2 == 0:
            x1, = _outproj(y_attn, y_mix, xt, w_out, l, ln1_g[l], ln1_b[l], alpha)
            xt = _dense_ffn(x1, ffn_w_gate, ffn_w_up, ffn_w_down, l // 2, ln2_g[l], ln2_b[l], alpha)
        else:
            x1, x1_tiles, meta, gates, counts = _outproj(y_attn, y_mix, xt, w_out, l, ln1_g[l], ln1_b[l], alpha,
                                                         w_router=router_w[l // 2])
            xt = _moe(x1, x1_tiles, meta, gates, counts, moe_gate, moe_up, moe_down, (l // 2) * N_EXPERTS,
                      ln2_g[l], ln2_b[l], alpha)
    return xt.reshape(batch, seq, d)
```

```python
import functools
import math

import jax
import jax.numpy as jnp
from jax import lax
from jax.experimental import pallas as pl
from jax.experimental.pallas import tpu as pltpu

F32 = jnp.float32
BF16 = jnp.bfloat16

D_MODEL = 1024
CHUNK = 64
ATTN_HEADS = 4
ATTN_WIDTH = 512
ATTN_HEAD_DIM = 64
HEAD_WIDTH = 2 * ATTN_HEAD_DIM
POOL_WINDOWS = (2, 4, 8, 16)
POOL_WIDTH = 256
POOL_GROUP_DIM = 64
LRU_WIDTH = 256
LRU_C = 8.0
CONV_WIDTH = 4
REST_WIDTH = POOL_WIDTH + 2 * LRU_WIDTH
IN_WIDTH = 3 * ATTN_WIDTH + REST_WIDTH
MIX_WIDTH = POOL_WIDTH + LRU_WIDTH
D_FF = 2816
N_EXPERTS = 8
TOP_K = 2
LN_EPS = 1e-5
HEAD_NORM_EPS = 1e-5

LANES = 128
SUBLANES = 8
VMEM_LIMIT_BYTES = 56 * 1024 * 1024

ROW_TILE = 512
ATTN_TILE = 256
ATTN_HEADS_PER_STEP = 4
ATTN_SUM_ROWS = 16
ATTN_SCORE_SCALE = ATTN_HEAD_DIM ** -0.5 * math.log2(math.e)
MIX_ROWS = 256
FFN_ROWS = 1024
FFN_COLS = 256
FFN_STEPS = D_FF // FFN_COLS
WEIGHT_TILES_AHEAD = 2
WEIGHT_SLOTS = WEIGHT_TILES_AHEAD + 1
FFN_ISSUE_STEPS = FFN_STEPS - 2
FFN_GATHER_ROWS_PER_STEP = -(-FFN_ROWS // (FFN_ISSUE_STEPS * SUBLANES)) * SUBLANES
FFN_GATHER_ROWS = FFN_GATHER_ROWS_PER_STEP * FFN_ISSUE_STEPS
MOVE_ROWS = 256
ROUTE_META = 8
SCALAR_LOOP_UNROLL = 8
DMA_ISSUE_UNROLL = 8
NEG_BIG = -1e30
F32_TINY = float(jnp.finfo(jnp.float32).tiny)
assert MIX_ROWS >= max(POOL_WINDOWS)


def _params(*semantics):
    return pltpu.CompilerParams(dimension_semantics=semantics, vmem_limit_bytes=VMEM_LIMIT_BYTES)


def _layer_norm(z, g, b):
    mu = jnp.mean(z, axis=-1, keepdims=True)
    zc = z - mu
    var = jnp.mean(zc * zc, axis=-1, keepdims=True)
    return zc * lax.rsqrt(var + LN_EPS) * g + b


def _inproj_body(x_ref, w_ref, q_ref, k_ref, v_ref, r_ref, wbf_ref):
    @pl.when(pl.program_id(0) == 0)
    def _cast_weights():
        for c in range(0, IN_WIDTH, 256):
            wbf_ref[:, c:c + 256] = w_ref[0, :, c:c + 256].astype(BF16)

    xb = x_ref[...].astype(BF16)

    def proj(c0, c1):
        return jnp.dot(xb, wbf_ref[:, c0:c1], preferred_element_type=F32)

    q_ref[...] = (proj(0, ATTN_WIDTH) * ATTN_SCORE_SCALE).astype(BF16)
    k_ref[...] = proj(ATTN_WIDTH, 2 * ATTN_WIDTH).astype(BF16)
    v_ref[...] = proj(2 * ATTN_WIDTH, 3 * ATTN_WIDTH).astype(BF16)
    r_ref[...] = proj(3 * ATTN_WIDTH, IN_WIDTH)


def _inproj(x, w, layer):
    t = x.shape[0]
    row = lambda i: (i, 0)
    return pl.pallas_call(
        _inproj_body,
        grid=(t // ROW_TILE,),
        in_specs=[pl.BlockSpec((ROW_TILE, D_MODEL), row),
                  pl.BlockSpec((1, D_MODEL, IN_WIDTH), lambda i: (layer, 0, 0))],
        out_specs=[pl.BlockSpec((ROW_TILE, ATTN_WIDTH), row)] * 3
        + [pl.BlockSpec((ROW_TILE, REST_WIDTH), row)],
        out_shape=[jax.ShapeDtypeStruct((t, ATTN_WIDTH), BF16)] * 3
        + [jax.ShapeDtypeStruct((t, REST_WIDTH), F32)],
        scratch_shapes=[pltpu.VMEM((D_MODEL, IN_WIDTH), BF16)],
        compiler_params=_params("arbitrary"),
        name="inproj",
    )(x, w)


def _attn_body(lam_ref, g_ref, q_ref, k_ref, v_ref, o_ref, vt_ref, *, lambda_init, seq):
    tq = ATTN_TILE
    lv = lam_ref[...]
    lam = (jnp.exp(jnp.sum(lv[0:1] * lv[1:2], axis=-1, keepdims=True))
           - jnp.exp(jnp.sum(lv[2:3] * lv[3:4], axis=-1, keepdims=True)) + lambda_init)
    gain = g_ref[...] * (1.0 - lambda_init)

    heads = range(ATTN_HEADS_PER_STEP)
    cols = [slice(g * HEAD_WIDTH, (g + 1) * HEAD_WIDTH) for g in heads]
    for g in heads:
        for c in range(seq // tq):
            vt_ref[g, c, 0:HEAD_WIDTH, :] = v_ref[c * tq:(c + 1) * tq, cols[g]].astype(F32).T.astype(BF16)
            vt_ref[g, c, HEAD_WIDTH:, :] = jnp.ones((ATTN_SUM_ROWS, tq), BF16)

    lane = lax.broadcasted_iota(jnp.int32, (tq, HEAD_WIDTH), 1)
    first_map = lane < ATTN_HEAD_DIM
    key = lax.broadcasted_iota(jnp.int32, (tq, 2 * tq), 0)
    qry = lax.broadcasted_iota(jnp.int32, (tq, 2 * tq), 1)
    q_chunk = jnp.where(qry >= tq, qry - tq, qry) // CHUNK
    visible = (key // CHUNK) <= q_chunk

    def q_block(i, _):
        q0 = pl.multiple_of(i * tq, tq)
        qqs = []
        for g in heads:
            q = q_ref[pl.ds(q0, tq), cols[g]]
            zero = jnp.zeros_like(q)
            qqs.append(jnp.concatenate([jnp.where(first_map, q, zero), jnp.where(first_map, zero, q)], axis=0))

        def step(j, carry, masked):
            k0 = pl.multiple_of(j * tq, tq)
            out = []
            scores = [lax.dot_general(k_ref[pl.ds(k0, tq), cols[g]], qqs[g], (((1,), (1,)), ((), ())),
                                      preferred_element_type=F32) for g in heads]
            for g in heads:
                m, acc = carry[g]
                s = scores[g]
                if masked:
                    s = jnp.where(visible, s, NEG_BIG)
                m_new = jnp.maximum(m, jnp.max(s, axis=0, keepdims=True))
                p = jnp.exp2(s - m_new)
                scale = jnp.exp2(m - m_new)
                pv = jnp.dot(vt_ref[g, j], p.astype(BF16), preferred_element_type=F32)
                out.append((m_new, scale * acc + pv))
            return tuple(out)

        init = tuple((jnp.full((1, 2 * tq), NEG_BIG, F32),
                      jnp.zeros((HEAD_WIDTH + ATTN_SUM_ROWS, 2 * tq), F32)) for _ in heads)
        carry = lax.fori_loop(0, i, lambda j, c: step(j, c, False), init)
        carry = step(i, carry, True)
        for g in heads:
            _, acc = carry[g]
            l = acc[HEAD_WIDTH:HEAD_WIDTH + 1, :]
            acc = acc[:HEAD_WIDTH, :]
            o = acc[:, :tq] / l[:, :tq] - lam * (acc[:, tq:] / l[:, tq:])
            o = o * lax.rsqrt(jnp.mean(o * o, axis=0, keepdims=True) + HEAD_NORM_EPS)
            o_ref[pl.ds(q0, tq), cols[g]] = (o.T * gain).astype(BF16)
        return 0

    lax.fori_loop(0, seq // tq, q_block, 0)


def _attention(q, k, v, lam_params, subln_g, lambda_init, batch, seq):
    t = q.shape[0]
    blk = pl.BlockSpec((seq, ATTN_HEADS_PER_STEP * HEAD_WIDTH), lambda b, h: (b, h))
    return pl.pallas_call(
        functools.partial(_attn_body, lambda_init=lambda_init, seq=seq),
        grid=(batch, ATTN_HEADS // ATTN_HEADS_PER_STEP),
        in_specs=[pl.BlockSpec((4, ATTN_HEAD_DIM), lambda b, h: (0, 0)),
                  pl.BlockSpec((1, HEAD_WIDTH), lambda b, h: (0, 0)),
                  blk, blk, blk],
        out_specs=blk,
        out_shape=jax.ShapeDtypeStruct((t, ATTN_WIDTH), BF16),
        scratch_shapes=[pltpu.VMEM((ATTN_HEADS_PER_STEP, seq // ATTN_TILE, HEAD_WIDTH + ATTN_SUM_ROWS, ATTN_TILE),
                                   BF16)],
        compiler_params=_params("arbitrary", "arbitrary"),
        name="diff_attention",
    )(lam_params, subln_g.reshape(1, HEAD_WIDTH), q, k, v)


def _mixer_body(r_ref, pw_ref, ps_ref, cw_ref, cb_ref, wa_ref, ba_ref, wi_ref, bi_ref, lam_ref, y_ref,
                a_ref, b_ref, h_ref, inv_ref, *, seq):
    rows = MIX_ROWS
    pool_hist = max(POOL_WINDOWS)
    conv_hist = SUBLANES
    pw = pw_ref[...]
    wa = wa_ref[...]
    wi = wi_ref[...]
    ps, cb, ba, bi = ps_ref[...], cb_ref[...], ba_ref[...], bi_ref[...]
    cw = cw_ref[...]
    neg_c_softplus = -LRU_C * jax.nn.softplus(-lam_ref[...])

    lane_e = lax.broadcasted_iota(jnp.int32, (rows + pool_hist, POOL_WIDTH), 1)
    lane = lax.broadcasted_iota(jnp.int32, (rows, POOL_WIDTH), 1)
    row = lax.broadcasted_iota(jnp.int32, (rows, POOL_WIDTH), 0)
    win = jnp.where(lane < 64, 2, jnp.where(lane < 128, 4, jnp.where(lane < 192, 8, 16)))
    inv_win = jnp.where(lane < 64, 1 / 2, jnp.where(lane < 128, 1 / 4, jnp.where(lane < 192, 1 / 8, 1 / 16)))
    inv_ref[...] = 1.0 / (row + 1).astype(F32)
    groups = rows // SUBLANES
    row_in_group = lax.broadcasted_iota(jnp.int32, (groups, SUBLANES, LRU_WIDTH), 1)
    group = lax.broadcasted_iota(jnp.int32, (groups, LRU_WIDTH), 0)

    def chunk(c, carry):
        tail_u, tail_x, h_prev = carry
        r0 = pl.multiple_of(c * rows, rows)
        u = r_ref[pl.ds(r0, rows), 0:POOL_WIDTH]
        xr = r_ref[pl.ds(r0, rows), POOL_WIDTH:POOL_WIDTH + LRU_WIDTH]
        xg = r_ref[pl.ds(r0, rows), POOL_WIDTH + LRU_WIDTH:REST_WIDTH]

        ue = jnp.concatenate([tail_u, u], axis=0)
        w2 = ue + pltpu.roll(ue, 1, 0)
        w4 = w2 + pltpu.roll(w2, 2, 0)
        w8 = w4 + pltpu.roll(w4, 4, 0)
        w16 = w8 + pltpu.roll(w8, 8, 0)
        ws = jnp.where(lane_e < 64, w2, jnp.where(lane_e < 128, w4, jnp.where(lane_e < 192, w8, w16)))
        ws = ws[pool_hist:]
        inv_count = jnp.where(r0 + row + 1 >= win, inv_win, inv_ref[...])
        pooled = ws * inv_count - u
        y_pool = jnp.dot(pooled.astype(BF16), pw, preferred_element_type=F32) * ps

        xe = jnp.concatenate([tail_x, xr], axis=0)
        xc = cb + pltpu.roll(xe, 3, 0)[conv_hist:] * cw[0:1]
        xc = xc + pltpu.roll(xe, 2, 0)[conv_hist:] * cw[1:2]
        xc = xc + pltpu.roll(xe, 1, 0)[conv_hist:] * cw[2:3]
        xc = xc + xr * cw[3:4]
        xcb = xc.astype(BF16)
        r_gate = jax.nn.sigmoid(jnp.dot(xcb, wa, preferred_element_type=F32) + ba)
        i_gate = jax.nn.sigmoid(jnp.dot(xcb, wi, preferred_element_type=F32) + bi)
        log_a = r_gate * neg_c_softplus
        a = jnp.exp(log_a)
        gap = -jnp.tanh(log_a) * (a * a + 1.0)
        b = (gap * lax.rsqrt(jnp.maximum(gap, F32_TINY))) * (i_gate * xc)

        def doubling(a, b, pos, length, axis):
            s = 1
            while s < length:
                keep = pos >= s
                a_prev = jnp.where(keep, pltpu.roll(a, s, axis), 1.0)
                b_prev = jnp.where(keep, pltpu.roll(b, s, axis), 0.0)
                b = a * b_prev + b
                a = a * a_prev
                s *= 2
            return a, b

        a, b = doubling(a.reshape(groups, SUBLANES, LRU_WIDTH), b.reshape(groups, SUBLANES, LRU_WIDTH),
                        row_in_group, SUBLANES, 1)
        a = a.reshape(rows, LRU_WIDTH)
        b = b.reshape(rows, LRU_WIDTH)
        halves = range(LRU_WIDTH // LANES)
        ends = pl.ds(SUBLANES - 1, groups, stride=SUBLANES)
        for k in halves:
            a_ref[k] = a[:, k * LANES:(k + 1) * LANES]
            b_ref[k] = b[:, k * LANES:(k + 1) * LANES]
        a_end = jnp.concatenate([a_ref[k, ends, :] for k in halves], axis=-1)
        b_end = jnp.concatenate([b_ref[k, ends, :] for k in halves], axis=-1)
        a_end, b_end = doubling(a_end, b_end, group, groups, 0)
        h_end = a_end * h_prev + b_end
        h_ref[...] = jnp.where(group == 0, h_prev, pltpu.roll(h_end, 1, 0))
        h_start = jnp.concatenate([jnp.broadcast_to(h_ref[g:g + 1, :], (SUBLANES, LRU_WIDTH))
                                   for g in range(groups)], axis=0)
        h = a * h_start + b
        y_lru = h * jax.nn.gelu(xg)

        y_ref[pl.ds(r0, rows), 0:POOL_WIDTH] = y_pool.astype(BF16)
        y_ref[pl.ds(r0, rows), POOL_WIDTH:MIX_WIDTH] = y_lru.astype(BF16)
        return u[rows - pool_hist:], xr[rows - conv_hist:], h_end[groups - 1:groups]

    init = (jnp.zeros((pool_hist, POOL_WIDTH), F32), jnp.zeros((conv_hist, LRU_WIDTH), F32),
            jnp.zeros((1, LRU_WIDTH), F32))
    lax.fori_loop(0, seq // rows, chunk, init)


def _block_diag(w):
    g, c, d = w.shape
    eye = jnp.eye(g, dtype=w.dtype)
    return (eye[:, None, :, None] * w[:, :, None, :]).reshape(g * c, g * d)


def _mixer(rest, pool_w, pool_scale, conv_w, conv_b, wa, ba, wi, bi, lru_lambda, batch, seq):
    t = rest.shape[0]
    full = lambda shape: pl.BlockSpec(shape, lambda b: (0, 0))
    vec = lambda a: a.reshape(1, -1)
    return pl.pallas_call(
        functools.partial(_mixer_body, seq=seq),
        grid=(batch,),
        in_specs=[pl.BlockSpec((seq, REST_WIDTH), lambda b: (b, 0)),
                  full((POOL_WIDTH, POOL_WIDTH)), full((1, POOL_WIDTH)),
                  full((CONV_WIDTH, LRU_WIDTH)), full((1, LRU_WIDTH)),
                  full((LRU_WIDTH, LRU_WIDTH)), full((1, LRU_WIDTH)),
                  full((LRU_WIDTH, LRU_WIDTH)), full((1, LRU_WIDTH)),
                  full((1, LRU_WIDTH))],
        out_specs=pl.BlockSpec((seq, MIX_WIDTH), lambda b: (b, 0)),
        out_shape=jax.ShapeDtypeStruct((t, MIX_WIDTH), BF16),
        scratch_shapes=[pltpu.VMEM((LRU_WIDTH // LANES, MIX_ROWS, LANES), F32),
                        pltpu.VMEM((LRU_WIDTH // LANES, MIX_ROWS, LANES), F32),
                        pltpu.VMEM((MIX_ROWS // SUBLANES, LRU_WIDTH), F32),
                        pltpu.VMEM((MIX_ROWS, POOL_WIDTH), F32)],
        compiler_params=_params("arbitrary"),
        name="pool_lru_mixer",
    )(rest, _block_diag(pool_w).astype(BF16), vec(pool_scale), conv_w, vec(conv_b),
      _block_diag(wa).astype(BF16), vec(ba), _block_diag(wi).astype(BF16), vec(bi), vec(lru_lambda))


def _outproj_body(ya_ref, ym_ref, x_ref, w_ref, g_ref, b_ref, *rest, alpha, route):
    if route:
        wr_ref, o_ref, ot_ref, meta_ref, gate_ref, cnt_ref, wbf_ref, run_ref = rest
    else:
        o_ref, wbf_ref = rest

    @pl.when(pl.program_id(0) == 0)
    def _cast_weights():
        for c in range(0, D_MODEL, 256):
            wbf_ref[:, c:c + 256] = w_ref[0, :, c:c + 256].astype(BF16)

    mix = jnp.dot(ya_ref[...], wbf_ref[0:ATTN_WIDTH, :], preferred_element_type=F32)
    mix = mix + jnp.dot(ym_ref[...], wbf_ref[ATTN_WIDTH:, :], preferred_element_type=F32)
    x1 = _layer_norm(alpha * x_ref[...] + mix, g_ref[...], b_ref[...])
    o_ref[...] = x1
    if route:
        _store_row_tiles(ot_ref, x1)
        _route(x1, wr_ref, meta_ref, gate_ref, cnt_ref, run_ref)


def _outproj(y_attn, y_mix, x, w, layer, g, b, alpha, w_router=None):
    t = x.shape[0]
    route = w_router is not None
    row = lambda i: (i, 0)
    const = lambda i: (0, 0)
    in_specs = [pl.BlockSpec((ROW_TILE, ATTN_WIDTH), row), pl.BlockSpec((ROW_TILE, MIX_WIDTH), row),
                pl.BlockSpec((ROW_TILE, D_MODEL), row),
                pl.BlockSpec((1, D_MODEL, D_MODEL), lambda i: (layer, 0, 0)),
                pl.BlockSpec((1, D_MODEL), const), pl.BlockSpec((1, D_MODEL), const)]
    args = [y_attn, y_mix, x, w, g.reshape(1, -1), b.reshape(1, -1)]
    out_specs = [pl.BlockSpec((ROW_TILE, D_MODEL), row)]
    out_shape = [jax.ShapeDtypeStruct((t, D_MODEL), F32)]
    scratch = [pltpu.VMEM((D_MODEL, D_MODEL), BF16)]
    if route:
        in_specs.append(pl.BlockSpec((N_EXPERTS, D_MODEL), const))
        args.append(w_router.T)
        out_specs += [pl.BlockSpec((ROW_TILE * SUBLANES, LANES), row),
                      pl.BlockSpec((ROW_TILE, ROUTE_META), row), pl.BlockSpec((ROW_TILE, LANES), row),
                      pl.BlockSpec((SUBLANES, LANES), const)]
        out_shape += [jax.ShapeDtypeStruct((t * SUBLANES, LANES), F32),
                      jax.ShapeDtypeStruct((t, ROUTE_META), jnp.int32), jax.ShapeDtypeStruct((t, LANES), F32),
                      jax.ShapeDtypeStruct((SUBLANES, LANES), F32)]
        scratch.append(pltpu.VMEM((1, LANES), F32))
    return pl.pallas_call(
        functools.partial(_outproj_body, alpha=alpha, route=route),
        grid=(t // ROW_TILE,),
        in_specs=in_specs,
        out_specs=out_specs,
        out_shape=out_shape,
        scratch_shapes=scratch,
        compiler_params=_params("arbitrary"),
        name="outproj_ln_route" if route else "outproj_ln",
    )(*args)


def _dense_ffn_body(x_ref, wg_hbm, wu_hbm, wd_hbm, g_ref, b_ref, o_ref,
                    wg_ref, wu_ref, wd_ref, stage_in_ref, stage_out_ref, xb_ref, acc_ref, sem, *, layer, alpha):
    n_chunks = 3 * FFN_STEPS

    def chunk_copy(k):
        which, c = divmod(k, FFN_STEPS)
        cols = pl.ds(c * FFN_COLS, FFN_COLS)
        if which < 2:
            src, dst = (wg_hbm, wu_hbm)[which].at[layer, :, cols], stage_in_ref.at[k % 2]
        else:
            src, dst = wd_hbm.at[layer, cols, :], stage_out_ref.at[k % 2]
        return pltpu.make_async_copy(src, dst, sem.at[k % 2])

    @pl.when(pl.program_id(0) == 0)
    def _load_weights():
        chunk_copy(0).start()
        for k in range(n_chunks):
            if k + 1 < n_chunks:
                chunk_copy(k + 1).start()
            chunk_copy(k).wait()
            which, c = divmod(k, FFN_STEPS)
            if which == 0:
                wg_ref[c] = stage_in_ref[k % 2].astype(BF16)
            elif which == 1:
                wu_ref[c] = stage_in_ref[k % 2].astype(BF16)
            else:
                wd_ref[c] = stage_out_ref[k % 2].astype(BF16)

    xb_ref[...] = x_ref[...].astype(BF16)
    acc_ref[...] = jnp.zeros_like(acc_ref)

    def tile(c, _):
        xb = xb_ref[...]
        gate = jnp.dot(xb, wg_ref[c], preferred_element_type=F32)
        up = jnp.dot(xb, wu_ref[c], preferred_element_type=F32)
        hidden = (jax.nn.silu(gate) * up).astype(BF16)
        acc_ref[...] += jnp.dot(hidden, wd_ref[c], preferred_element_type=F32)
        return 0
    lax.fori_loop(0, FFN_STEPS, tile, 0)
    o_ref[...] = _layer_norm(alpha * x_ref[...] + acc_ref[...], g_ref[...], b_ref[...])


def _dense_ffn(x, w_gate, w_up, w_down, layer, g, b, alpha):
    t = x.shape[0]
    row = lambda i: (i, 0)
    const = lambda i: (0, 0)
    any_spec = pl.BlockSpec(memory_space=pl.ANY)
    return pl.pallas_call(
        functools.partial(_dense_ffn_body, layer=layer, alpha=alpha),
        grid=(t // FFN_ROWS,),
        in_specs=[pl.BlockSpec((FFN_ROWS, D_MODEL), row), any_spec, any_spec, any_spec,
                  pl.BlockSpec((1, D_MODEL), const), pl.BlockSpec((1, D_MODEL), const)],
        out_specs=pl.BlockSpec((FFN_ROWS, D_MODEL), row),
        out_shape=jax.ShapeDtypeStruct((t, D_MODEL), F32),
        scratch_shapes=[pltpu.VMEM((FFN_STEPS, D_MODEL, FFN_COLS), BF16),
                        pltpu.VMEM((FFN_STEPS, D_MODEL, FFN_COLS), BF16),
                        pltpu.VMEM((FFN_STEPS, FFN_COLS, D_MODEL), BF16),
                        pltpu.VMEM((2, D_MODEL, FFN_COLS), F32), pltpu.VMEM((2, FFN_COLS, D_MODEL), F32),
                        pltpu.VMEM((FFN_ROWS, D_MODEL), BF16), pltpu.VMEM((FFN_ROWS, D_MODEL), F32),
                        pltpu.SemaphoreType.DMA((2,))],
        compiler_params=_params("arbitrary"),
        name="swiglu_ln",
    )(x, w_gate, w_up, w_down, g.reshape(1, -1), b.reshape(1, -1))


def _store_row_tiles(o_ref, val):
    rows = val.shape[0]
    for s in range(SUBLANES):
        o_ref[pl.ds(s, rows, stride=SUBLANES), :] = val[:, s * LANES:(s + 1) * LANES]


def _load_row_tiles(ref, first_row, rows):
    return [ref[pl.ds(first_row * SUBLANES + s, rows, stride=SUBLANES), :] for s in range(SUBLANES)]


def _tile_window(row):
    return pl.ds(pl.multiple_of(row * SUBLANES, SUBLANES), SUBLANES)


def _start_row_copies(n_rows, make_copy, priorities):
    k = len(priorities)

    def body(q, _):
        for p, priority in enumerate(priorities):
            make_copy(q * k + p).start(priority=priority)
        return 0
    lax.fori_loop(0, n_rows // k, body, 0, unroll=DMA_ISSUE_UNROLL // k)


def _gather_copies(src_ref, x_hbm, buf_ref, sem, slot):
    def copy(r):
        return pltpu.make_async_copy(x_hbm.at[_tile_window(src_ref[0, 0, r])], buf_ref.at[slot, _tile_window(r)],
                                     sem.at[slot])
    return copy


def _expert_ffn_body(be_ref, nu_ref, src0_ref, src_next_ref, x_hbm, wg_hbm, wu_hbm, wd_hbm, o_ref,
                     xb_ref, acc_ref, buf_ref, wg_buf, wu_buf, wd_buf, row_sem, w_sem):
    i = pl.program_id(0)
    n_used = nu_ref[0]
    used = i < n_used
    slot = lax.rem(i, 2)
    row_priority = 1

    def wait_rows(which):
        pltpu.make_async_copy(buf_ref.at[which], buf_ref.at[which], row_sem.at[which]).wait()

    def weight_copies(block, c, wslot):
        e = be_ref[block]
        cols = pl.ds(pl.multiple_of(c * FFN_COLS, FFN_COLS), FFN_COLS)
        return (pltpu.make_async_copy(wg_hbm.at[e, :, cols], wg_buf.at[wslot], w_sem.at[wslot]),
                pltpu.make_async_copy(wu_hbm.at[e, :, cols], wu_buf.at[wslot], w_sem.at[wslot]),
                pltpu.make_async_copy(wd_hbm.at[e, cols, :], wd_buf.at[wslot], w_sem.at[wslot]))

    @pl.when(jnp.logical_and(used, i == 0))
    def _first_copies():
        for k in range(WEIGHT_TILES_AHEAD):
            for copy in weight_copies(0, k, k):
                copy.start()
        _start_row_copies(FFN_GATHER_ROWS, _gather_copies(src0_ref, x_hbm, buf_ref, row_sem, 0),
                          priorities=(row_priority,))

    @pl.when(used)
    def _block():
        def rows_to_matrix(which):
            for s, part in enumerate(_load_row_tiles(buf_ref.at[which], 0, FFN_ROWS)):
                xb_ref[which, :, s * LANES:(s + 1) * LANES] = part.astype(BF16)

        @pl.when(i == 0)
        def _first_block_rows():
            wait_rows(0)
            rows_to_matrix(0)

        acc_ref[...] = jnp.zeros_like(acc_ref)
        next_block = jnp.minimum(i + 1, n_used - 1)
        next_row = _gather_copies(src_next_ref, x_hbm, buf_ref, row_sem, 1 - slot)

        def tile(c, start_rows, finish_rows):
            q = i * FFN_STEPS + c
            wslot = lax.rem(q, WEIGHT_SLOTS)
            ahead = c + WEIGHT_TILES_AHEAD
            wraps = ahead >= FFN_STEPS
            for copy in weight_copies(jnp.where(wraps, next_block, i), jnp.where(wraps, ahead - FFN_STEPS, ahead),
                                      lax.rem(q + WEIGHT_TILES_AHEAD, WEIGHT_SLOTS)):
                copy.start()
            for copy in weight_copies(i, c, wslot):
                copy.wait()
            if start_rows:
                for r in range(FFN_GATHER_ROWS_PER_STEP):
                    next_row(c * FFN_GATHER_ROWS_PER_STEP + r).start(priority=row_priority)
            if finish_rows:
                wait_rows(1 - slot)
                rows_to_matrix(1 - slot)
            xb = xb_ref[slot]
            gate = jnp.dot(xb, wg_buf[wslot].astype(BF16), preferred_element_type=F32)
            up = jnp.dot(xb, wu_buf[wslot].astype(BF16), preferred_element_type=F32)
            hidden = (jax.nn.silu(gate) * up).astype(BF16)
            acc_ref[...] += jnp.dot(hidden, wd_buf[wslot].astype(BF16), preferred_element_type=F32)

        def issuing_tile(c, _):
            tile(c, True, False)
            return 0
        lax.fori_loop(0, FFN_ISSUE_STEPS, issuing_tile, 0)
        for c in range(FFN_ISSUE_STEPS, FFN_STEPS):
            tile(c, False, c == FFN_STEPS - 1)

        @pl.when(i == n_used - 1)
        def _drain():
            for k in range(WEIGHT_TILES_AHEAD):
                for copy in weight_copies(next_block, k, lax.rem((i + 1) * FFN_STEPS + k, WEIGHT_SLOTS)):
                    copy.wait()
        _store_row_tiles(o_ref, acc_ref[...])

    @pl.when(jnp.logical_not(used))
    def _empty():
        o_ref[...] = jnp.zeros_like(o_ref)


def _expert_ffn(block_e, n_used, x_tiles, w_gate, w_up, w_down, src):
    n_blocks = block_e.shape[0]
    dummy = jnp.broadcast_to(jnp.arange(FFN_GATHER_ROWS - FFN_ROWS, dtype=jnp.int32),
                             (n_blocks, 1, FFN_GATHER_ROWS - FFN_ROWS))
    src3 = jnp.concatenate([src.reshape(n_blocks, 1, FFN_ROWS), dummy], axis=-1)
    any_spec = pl.BlockSpec(memory_space=pl.ANY)
    in_specs = [pl.BlockSpec((1, 1, FFN_GATHER_ROWS), lambda i, be, nu: (0, 0, 0), memory_space=pltpu.SMEM),
                pl.BlockSpec((1, 1, FFN_GATHER_ROWS), lambda i, be, nu: (jnp.minimum(i + 1, n_blocks - 1), 0, 0),
                             memory_space=pltpu.SMEM),
                any_spec, any_spec, any_spec, any_spec]
    return pl.pallas_call(
        _expert_ffn_body,
        grid_spec=pltpu.PrefetchScalarGridSpec(
            num_scalar_prefetch=2,
            grid=(n_blocks,),
            in_specs=in_specs,
            out_specs=pl.BlockSpec((FFN_ROWS * SUBLANES, LANES), lambda i, be, nu: (i, 0)),
            scratch_shapes=[pltpu.VMEM((2, FFN_ROWS, D_MODEL), BF16), pltpu.VMEM((FFN_ROWS, D_MODEL), F32),
                            pltpu.VMEM((2, FFN_GATHER_ROWS * SUBLANES, LANES), F32),
                            pltpu.VMEM((WEIGHT_SLOTS, D_MODEL, FFN_COLS), F32),
                            pltpu.VMEM((WEIGHT_SLOTS, D_MODEL, FFN_COLS), F32),
                            pltpu.VMEM((WEIGHT_SLOTS, FFN_COLS, D_MODEL), F32),
                            pltpu.SemaphoreType.DMA((2,)), pltpu.SemaphoreType.DMA((WEIGHT_SLOTS,))]),
        out_shape=jax.ShapeDtypeStruct((n_blocks * FFN_ROWS * SUBLANES, LANES), F32),
        compiler_params=_params("arbitrary"),
        name="swiglu_experts",
    )(block_e, n_used, src3, src3, x_tiles, w_gate, w_up, w_down)


def _route(x, w_ref, meta_ref, gate_ref, cnt_ref, run_ref):
    tm = x.shape[0]

    @pl.when(pl.program_id(0) == 0)
    def _init():
        run_ref[...] = jnp.zeros_like(run_ref)

    lane_i = lax.broadcasted_iota(jnp.int32, (tm, LANES), 1)
    lane = lane_i.astype(F32)
    logits = jnp.full((tm, LANES), -jnp.inf, F32)
    for e in range(N_EXPERTS):
        logit_e = jnp.sum(x * w_ref[e:e + 1, :], axis=-1, keepdims=True)
        logits = jnp.where(lane_i == e, logit_e, logits)
    m1 = jnp.max(logits, axis=-1, keepdims=True)
    e1 = jnp.min(jnp.where(logits == m1, lane, float(LANES)), axis=-1, keepdims=True)
    rest = jnp.where(lane == e1, -jnp.inf, logits)
    m2 = jnp.max(rest, axis=-1, keepdims=True)
    e2 = jnp.min(jnp.where(rest == m2, lane, float(LANES)), axis=-1, keepdims=True)
    ex = jnp.exp(m2 - m1)
    g1 = 1.0 / (1.0 + ex)
    g2 = ex / (1.0 + ex)

    chosen = jnp.logical_or(lane == e1, lane == e2)
    r_i = lax.broadcasted_iota(jnp.int32, (tm, tm), 0)
    c_i = lax.broadcasted_iota(jnp.int32, (tm, tm), 1)
    earlier = (c_i < r_i).astype(BF16)
    before = jnp.dot(earlier, chosen.astype(BF16), preferred_element_type=F32) + run_ref[...]
    rank1 = jnp.sum(jnp.where(lane == e1, before, 0.0), axis=-1, keepdims=True)
    rank2 = jnp.sum(jnp.where(lane == e2, before, 0.0), axis=-1, keepdims=True)
    total = run_ref[...] + jnp.sum(chosen.astype(F32), axis=0, keepdims=True)
    run_ref[...] = total
    cnt_ref[...] = jnp.broadcast_to(total, cnt_ref.shape)
    meta = jnp.where(lane_i == 0, e1, jnp.where(lane_i == 1, e2, jnp.where(lane_i == 2, rank1, rank2)))
    meta_ref[...] = meta[:, :ROUTE_META].astype(jnp.int32)
    gate_ref[...] = jnp.where(lane_i == 0, g1, g2)


def _combine_body(d0_ref, dn_ref, ys_hbm, gate_ref, x_ref, g_ref, b_ref, o_ref, buf_ref, sem, *, alpha):
    i = pl.program_id(0)
    slot = lax.rem(i, 2)
    n_rows = buf_ref.shape[1] // SUBLANES

    def row_copy(d_ref, to_slot):
        def copy(r):
            return pltpu.make_async_copy(ys_hbm.at[_tile_window(d_ref[0, 0, r])],
                                         buf_ref.at[to_slot, _tile_window(r)], sem.at[to_slot])
        return copy

    def wait_slot(which):
        pltpu.make_async_copy(buf_ref.at[which], buf_ref.at[which], sem.at[which]).wait()

    @pl.when(i == 0)
    def _first_rows():
        _start_row_copies(n_rows, row_copy(d0_ref, 0), priorities=(0, 1))

    wait_slot(slot)
    next_copy = row_copy(dn_ref, 1 - slot)
    for r in range(n_rows):
        next_copy(r).start(priority=r % 2)

    gates = gate_ref[...]
    g1, g2 = gates[:, 0:1], gates[:, 1:2]
    first = _load_row_tiles(buf_ref.at[slot], 0, MOVE_ROWS)
    second = _load_row_tiles(buf_ref.at[slot], MOVE_ROWS, MOVE_ROWS)
    f = jnp.concatenate([g1 * a + g2 * b for a, b in zip(first, second)], axis=-1)
    o_ref[...] = _layer_norm(alpha * x_ref[...] + f, g_ref[...], b_ref[...])

    @pl.when(i == pl.num_programs(0) - 1)
    def _drain():
        wait_slot(1 - slot)


def _combine(ys, dest1, dest2, gates, x, g, b, alpha):
    t = x.shape[0]
    steps = t // MOVE_ROWS
    dest = jnp.concatenate([dest1.reshape(steps, 1, MOVE_ROWS), dest2.reshape(steps, 1, MOVE_ROWS)], axis=-1)
    idx_shape = (1, 1, TOP_K * MOVE_ROWS)
    row = lambda i: (i, 0)
    const = lambda i: (0, 0)
    return pl.pallas_call(
        functools.partial(_combine_body, alpha=alpha),
        grid=(steps,),
        in_specs=[pl.BlockSpec(idx_shape, lambda i: (0, 0, 0), memory_space=pltpu.SMEM),
                  pl.BlockSpec(idx_shape, lambda i: (jnp.minimum(i + 1, steps - 1), 0, 0), memory_space=pltpu.SMEM),
                  pl.BlockSpec(memory_space=pl.ANY),
                  pl.BlockSpec((MOVE_ROWS, LANES), row), pl.BlockSpec((MOVE_ROWS, D_MODEL), row),
                  pl.BlockSpec((1, D_MODEL), const), pl.BlockSpec((1, D_MODEL), const)],
        out_specs=pl.BlockSpec((MOVE_ROWS, D_MODEL), row),
        out_shape=jax.ShapeDtypeStruct((t, D_MODEL), F32),
        scratch_shapes=[pltpu.VMEM((2, TOP_K * MOVE_ROWS * SUBLANES, LANES), F32), pltpu.SemaphoreType.DMA((2,))],
        compiler_params=_params("arbitrary"),
        name="combine_ln",
    )(dest, dest, ys, gates, x, g.reshape(1, -1), b.reshape(1, -1))


def _invert_body(d1_ref, d2_ref, fill_hbm, src_ref, sem):
    fill = pltpu.make_async_copy(fill_hbm, src_ref, sem)
    fill.start()
    fill.wait()

    def place(t, _):
        src_ref[d1_ref[t]] = t
        src_ref[d2_ref[t]] = t
        return 0
    lax.fori_loop(0, d1_ref.shape[0], place, 0, unroll=SCALAR_LOOP_UNROLL)


def _invert_placement(dest1, dest2, n_rows):
    smem = pl.BlockSpec(memory_space=pltpu.SMEM)
    return pl.pallas_call(
        _invert_body,
        in_specs=[smem, smem, pl.BlockSpec(memory_space=pl.ANY)],
        out_specs=smem,
        out_shape=jax.ShapeDtypeStruct((n_rows,), jnp.int32),
        scratch_shapes=[pltpu.SemaphoreType.DMA(())],
        name="invert_placement",
    )(dest1, dest2, jnp.arange(n_rows, dtype=jnp.int32) % dest1.shape[0])


def _moe(x1, x1_tiles, meta, gates, counts, w_gate, w_up, w_down, first_expert, g, b, alpha):
    t = x1.shape[0]
    e1, e2, rank1, rank2 = meta[:, 0], meta[:, 1], meta[:, 2], meta[:, 3]
    sizes = counts[0, :N_EXPERTS].astype(jnp.int32)
    padded = (sizes + FFN_ROWS - 1) // FFN_ROWS * FFN_ROWS
    group_end = jnp.cumsum(padded)
    group_start = group_end - padded
    dest1 = group_start[e1] + rank1
    dest2 = group_start[e2] + rank2
    n_blocks = (t * TOP_K) // FFN_ROWS + N_EXPERTS
    block_start = jnp.arange(n_blocks, dtype=jnp.int32) * FFN_ROWS
    block_e = jnp.minimum(jnp.sum(group_end[None, :] <= block_start[:, None], axis=1), N_EXPERTS - 1)
    n_used = (group_end[-1] // FFN_ROWS).reshape(1)
    src = _invert_placement(dest1, dest2, n_blocks * FFN_ROWS)
    ys = _expert_ffn((block_e + first_expert).astype(jnp.int32), n_used.astype(jnp.int32), x1_tiles, w_gate,
                     w_up, w_down, src)
    return _combine(ys, dest1, dest2, gates, x1, g, b, alpha)


def kernel(x, w_in, w_out, attn_lambda, attn_subln_g, pool_w, pool_scale, conv_w, conv_b, lru_wa, lru_ba,
           lru_wi, lru_bi, lru_lambda, ln1_g, ln1_b, ln2_g, ln2_b, ffn_w_gate, ffn_w_up, ffn_w_down,
           router_w, moe_w_gate, moe_w_up, moe_w_down):
    batch, seq, d = x.shape
    depth = w_in.shape[0]
    assert d == D_MODEL and seq % ATTN_TILE == 0 and seq % MIX_ROWS == 0
    t = batch * seq
    assert t % FFN_ROWS == 0 and t % ROW_TILE == 0
    alpha = (2.0 * depth) ** 0.25
    moe_gate = moe_w_gate.reshape(-1, D_MODEL, D_FF)
    moe_up = moe_w_up.reshape(-1, D_MODEL, D_FF)
    moe_down = moe_w_down.reshape(-1, D_FF, D_MODEL)
    xt = x.reshape(t, d)
    for l in range(depth):
        lambda_init = 0.8 - 0.6 * math.exp(-0.3 * l)
        q, k, v, rest = _inproj(xt, w_in, l)
        y_attn = _attention(q, k, v, attn_lambda[l], attn_subln_g[l], lambda_init, batch, seq)
        y_mix = _mixer(rest, pool_w[l], pool_scale[l], conv_w[l], conv_b[l], lru_wa[l], lru_ba[l],
                       lru_wi[l], lru_bi[l], lru_lambda[l], batch, seq)
        if l % 2 == 0:
            x1, = _outproj(y_attn, y_mix, xt, w_out, l, ln1_g[l], ln1_b[l], alpha)
            xt = _dense_ffn(x1, ffn_w_gate, ffn_w_up, ffn_w_down, l // 2, ln2_g[l], ln2_b[l], alpha)
        else:
            x1, x1_tiles, meta, gates, counts = _outproj(y_attn, y_mix, xt, w_out, l, ln1_g[l], ln1_b[l], alpha,
                                                         w_router=router_w[l // 2])
            xt = _moe(x1, x1_tiles, meta, gates, counts, moe_gate, moe_up, moe_down, (l // 2) * N_EXPERTS,
                      ln2_g[l], ln2_b[l], alpha)
    return xt.reshape(batch, seq, d)
```

```python
import functools
import math

import jax
import jax.numpy as jnp
from jax import lax
from jax.experimental import pallas as pl
from jax.experimental.pallas import tpu as pltpu

F32 = jnp.float32
BF16 = jnp.bfloat16

D_MODEL = 1024
CHUNK = 64
ATTN_HEADS = 4
ATTN_WIDTH = 512
ATTN_HEAD_DIM = 64
HEAD_WIDTH = 2 * ATTN_HEAD_DIM
POOL_WINDOWS = (2, 4, 8, 16)
POOL_WIDTH = 256
POOL_GROUP_DIM = 64
LRU_WIDTH = 256
LRU_C = 8.0
CONV_WIDTH = 4
REST_WIDTH = POOL_WIDTH + 2 * LRU_WIDTH
IN_WIDTH = 3 * ATTN_WIDTH + REST_WIDTH
MIX_WIDTH = POOL_WIDTH + LRU_WIDTH
D_FF = 2816
N_EXPERTS = 8
TOP_K = 2
LN_EPS = 1e-5
HEAD_NORM_EPS = 1e-5

LANES = 128
SUBLANES = 8
VMEM_LIMIT_BYTES = 56 * 1024 * 1024

ROW_TILE = 1024
ATTN_TILE = 256
ATTN_HEADS_PER_STEP = 4
ATTN_SUM_ROWS = 16
ATTN_SCORE_SCALE = ATTN_HEAD_DIM ** -0.5 * math.log2(math.e)
MIX_ROWS = 256
FFN_ROWS = 1024
FFN_COLS = 256
FFN_STEPS = D_FF // FFN_COLS
WEIGHT_TILES_AHEAD = 2
WEIGHT_SLOTS = WEIGHT_TILES_AHEAD + 1
FFN_ISSUE_STEPS = FFN_STEPS - 2
FFN_GATHER_ROWS_PER_STEP = -(-FFN_ROWS // (FFN_ISSUE_STEPS * SUBLANES)) * SUBLANES
FFN_GATHER_ROWS = FFN_GATHER_ROWS_PER_STEP * FFN_ISSUE_STEPS
MOVE_ROWS = 256
ROUTE_META = 8
SCALAR_LOOP_UNROLL = 8
DMA_ISSUE_UNROLL = 8
NEG_BIG = -1e30
F32_TINY = float(jnp.finfo(jnp.float32).tiny)
assert MIX_ROWS >= max(POOL_WINDOWS)


def _params(*semantics):
    return pltpu.CompilerParams(dimension_semantics=semantics, vmem_limit_bytes=VMEM_LIMIT_BYTES)


def _layer_norm(z, g, b):
    mu = jnp.mean(z, axis=-1, keepdims=True)
    zc = z - mu
    var = jnp.mean(zc * zc, axis=-1, keepdims=True)
    return zc * lax.rsqrt(var + LN_EPS) * g + b


def _inproj_body(x_ref, w_ref, q_ref, k_ref, v_ref, r_ref, wbf_ref):
    @pl.when(pl.program_id(0) == 0)
    def _cast_weights():
        for c in range(0, IN_WIDTH, 256):
            wbf_ref[:, c:c + 256] = w_ref[0, :, c:c + 256].astype(BF16)

    xb = x_ref[...].astype(BF16)

    def proj(c0, c1):
        return jnp.dot(xb, wbf_ref[:, c0:c1], preferred_element_type=F32)

    q_ref[...] = (proj(0, ATTN_WIDTH) * ATTN_SCORE_SCALE).astype(BF16)
    k_ref[...] = proj(ATTN_WIDTH, 2 * ATTN_WIDTH).astype(BF16)
    v_ref[...] = proj(2 * ATTN_WIDTH, 3 * ATTN_WIDTH).astype(BF16)
    r_ref[...] = proj(3 * ATTN_WIDTH, IN_WIDTH)


def _inproj(x, w, layer):
    t = x.shape[0]
    row = lambda i: (i, 0)
    return pl.pallas_call(
        _inproj_body,
        grid=(t // ROW_TILE,),
        in_specs=[pl.BlockSpec((ROW_TILE, D_MODEL), row),
                  pl.BlockSpec((1, D_MODEL, IN_WIDTH), lambda i: (layer, 0, 0))],
        out_specs=[pl.BlockSpec((ROW_TILE, ATTN_WIDTH), row)] * 3
        + [pl.BlockSpec((ROW_TILE, REST_WIDTH), row)],
        out_shape=[jax.ShapeDtypeStruct((t, ATTN_WIDTH), BF16)] * 3
        + [jax.ShapeDtypeStruct((t, REST_WIDTH), F32)],
        scratch_shapes=[pltpu.VMEM((D_MODEL, IN_WIDTH), BF16)],
        compiler_params=_params("arbitrary"),
        name="inproj",
    )(x, w)


def _attn_body(lam_ref, g_ref, q_ref, k_ref, v_ref, o_ref, vt_ref, *, lambda_init, seq):
    tq = ATTN_TILE
    lv = lam_ref[...]
    lam = (jnp.exp(jnp.sum(lv[0:1] * lv[1:2], axis=-1, keepdims=True))
           - jnp.exp(jnp.sum(lv[2:3] * lv[3:4], axis=-1, keepdims=True)) + lambda_init)
    gain = g_ref[...] * (1.0 - lambda_init)

    heads = range(ATTN_HEADS_PER_STEP)
    cols = [slice(g * HEAD_WIDTH, (g + 1) * HEAD_WIDTH) for g in heads]
    for g in heads:
        for c in range(seq // tq):
            vt_ref[g, c, 0:HEAD_WIDTH, :] = v_ref[c * tq:(c + 1) * tq, cols[g]].astype(F32).T.astype(BF16)
            vt_ref[g, c, HEAD_WIDTH:, :] = jnp.ones((ATTN_SUM_ROWS, tq), BF16)

    lane = lax.broadcasted_iota(jnp.int32, (tq, HEAD_WIDTH), 1)
    first_map = lane < ATTN_HEAD_DIM
    key = lax.broadcasted_iota(jnp.int32, (tq, 2 * tq), 0)
    qry = lax.broadcasted_iota(jnp.int32, (tq, 2 * tq), 1)
    q_chunk = jnp.where(qry >= tq, qry - tq, qry) // CHUNK
    visible = (key // CHUNK) <= q_chunk

    def q_block(i, _):
        q0 = pl.multiple_of(i * tq, tq)
        qqs = []
        for g in heads:
            q = q_ref[pl.ds(q0, tq), cols[g]]
            zero = jnp.zeros_like(q)
            qqs.append(jnp.concatenate([jnp.where(first_map, q, zero), jnp.where(first_map, zero, q)], axis=0))

        def step(j, carry, masked):
            k0 = pl.multiple_of(j * tq, tq)
            out = []
            scores = [lax.dot_general(k_ref[pl.ds(k0, tq), cols[g]], qqs[g], (((1,), (1,)), ((), ())),
                                      preferred_element_type=F32) for g in heads]
            for g in heads:
                m, acc = carry[g]
                s = scores[g]
                if masked:
                    s = jnp.where(visible, s, NEG_BIG)
                m_new = jnp.maximum(m, jnp.max(s, axis=0, keepdims=True))
                p = jnp.exp2(s - m_new)
                scale = jnp.exp2(m - m_new)
                pv = jnp.dot(vt_ref[g, j], p.astype(BF16), preferred_element_type=F32)
                out.append((m_new, scale * acc + pv))
            return tuple(out)

        init = tuple((jnp.full((1, 2 * tq), NEG_BIG, F32),
                      jnp.zeros((HEAD_WIDTH + ATTN_SUM_ROWS, 2 * tq), F32)) for _ in heads)
        carry = lax.fori_loop(0, i, lambda j, c: step(j, c, False), init)
        carry = step(i, carry, True)
        for g in heads:
            _, acc = carry[g]
            l = acc[HEAD_WIDTH:HEAD_WIDTH + 1, :]
            acc = acc[:HEAD_WIDTH, :]
            o = acc[:, :tq] / l[:, :tq] - lam * (acc[:, tq:] / l[:, tq:])
            o = o * lax.rsqrt(jnp.mean(o * o, axis=0, keepdims=True) + HEAD_NORM_EPS)
            o_ref[pl.ds(q0, tq), cols[g]] = (o.T * gain).astype(BF16)
        return 0

    lax.fori_loop(0, seq // tq, q_block, 0)


def _attention(q, k, v, lam_params, subln_g, lambda_init, batch, seq):
    t = q.shape[0]
    blk = pl.BlockSpec((seq, ATTN_HEADS_PER_STEP * HEAD_WIDTH), lambda b, h: (b, h))
    return pl.pallas_call(
        functools.partial(_attn_body, lambda_init=lambda_init, seq=seq),
        grid=(batch, ATTN_HEADS // ATTN_HEADS_PER_STEP),
        in_specs=[pl.BlockSpec((4, ATTN_HEAD_DIM), lambda b, h: (0, 0)),
                  pl.BlockSpec((1, HEAD_WIDTH), lambda b, h: (0, 0)),
                  blk, blk, blk],
        out_specs=blk,
        out_shape=jax.ShapeDtypeStruct((t, ATTN_WIDTH), BF16),
        scratch_shapes=[pltpu.VMEM((ATTN_HEADS_PER_STEP, seq // ATTN_TILE, HEAD_WIDTH + ATTN_SUM_ROWS, ATTN_TILE),
                                   BF16)],
        compiler_params=_params("arbitrary", "arbitrary"),
        name="diff_attention",
    )(lam_params, subln_g.reshape(1, HEAD_WIDTH), q, k, v)


def _mixer_body(r_ref, pw_ref, ps_ref, cw_ref, cb_ref, wa_ref, ba_ref, wi_ref, bi_ref, lam_ref, y_ref,
                a_ref, b_ref, h_ref, inv_ref, *, seq):
    rows = MIX_ROWS
    pool_hist = max(POOL_WINDOWS)
    conv_hist = SUBLANES
    pw = pw_ref[...]
    wa = wa_ref[...]
    wi = wi_ref[...]
    ps, cb, ba, bi = ps_ref[...], cb_ref[...], ba_ref[...], bi_ref[...]
    cw = cw_ref[...]
    neg_c_softplus = -LRU_C * jax.nn.softplus(-lam_ref[...])

    lane_e = lax.broadcasted_iota(jnp.int32, (rows + pool_hist, POOL_WIDTH), 1)
    lane = lax.broadcasted_iota(jnp.int32, (rows, POOL_WIDTH), 1)
    row = lax.broadcasted_iota(jnp.int32, (rows, POOL_WIDTH), 0)
    win = jnp.where(lane < 64, 2, jnp.where(lane < 128, 4, jnp.where(lane < 192, 8, 16)))
    inv_win = jnp.where(lane < 64, 1 / 2, jnp.where(lane < 128, 1 / 4, jnp.where(lane < 192, 1 / 8, 1 / 16)))
    inv_ref[...] = 1.0 / (row + 1).astype(F32)
    groups = rows // SUBLANES
    row_in_group = lax.broadcasted_iota(jnp.int32, (groups, SUBLANES, LRU_WIDTH), 1)
    group = lax.broadcasted_iota(jnp.int32, (groups, LRU_WIDTH), 0)

    def chunk(c, carry):
        tail_u, tail_x, h_prev = carry
        r0 = pl.multiple_of(c * rows, rows)
        u = r_ref[pl.ds(r0, rows), 0:POOL_WIDTH]
        xr = r_ref[pl.ds(r0, rows), POOL_WIDTH:POOL_WIDTH + LRU_WIDTH]
        xg = r_ref[pl.ds(r0, rows), POOL_WIDTH + LRU_WIDTH:REST_WIDTH]

        ue = jnp.concatenate([tail_u, u], axis=0)
        w2 = ue + pltpu.roll(ue, 1, 0)
        w4 = w2 + pltpu.roll(w2, 2, 0)
        w8 = w4 + pltpu.roll(w4, 4, 0)
        w16 = w8 + pltpu.roll(w8, 8, 0)
        ws = jnp.where(lane_e < 64, w2, jnp.where(lane_e < 128, w4, jnp.where(lane_e < 192, w8, w16)))
        ws = ws[pool_hist:]
        inv_count = jnp.where(r0 + row + 1 >= win, inv_win, inv_ref[...])
        pooled = ws * inv_count - u
        y_pool = jnp.dot(pooled.astype(BF16), pw, preferred_element_type=F32) * ps

        xe = jnp.concatenate([tail_x, xr], axis=0)
        xc = cb + pltpu.roll(xe, 3, 0)[conv_hist:] * cw[0:1]
        xc = xc + pltpu.roll(xe, 2, 0)[conv_hist:] * cw[1:2]
        xc = xc + pltpu.roll(xe, 1, 0)[conv_hist:] * cw[2:3]
        xc = xc + xr * cw[3:4]
        xcb = xc.astype(BF16)
        r_gate = jax.nn.sigmoid(jnp.dot(xcb, wa, preferred_element_type=F32) + ba)
        i_gate = jax.nn.sigmoid(jnp.dot(xcb, wi, preferred_element_type=F32) + bi)
        log_a = r_gate * neg_c_softplus
        a = jnp.exp(log_a)
        gap = -jnp.tanh(log_a) * (a * a + 1.0)
        b = (gap * lax.rsqrt(jnp.maximum(gap, F32_TINY))) * (i_gate * xc)

        def doubling(a, b, pos, length, axis):
            s = 1
            while s < length:
                keep = pos >= s
                a_prev = jnp.where(keep, pltpu.roll(a, s, axis), 1.0)
                b_prev = jnp.where(keep, pltpu.roll(b, s, axis), 0.0)
                b = a * b_prev + b
                a = a * a_prev
                s *= 2
            return a, b

        a, b = doubling(a.reshape(groups, SUBLANES, LRU_WIDTH), b.reshape(groups, SUBLANES, LRU_WIDTH),
                        row_in_group, SUBLANES, 1)
        a = a.reshape(rows, LRU_WIDTH)
        b = b.reshape(rows, LRU_WIDTH)
        halves = range(LRU_WIDTH // LANES)
        ends = pl.ds(SUBLANES - 1, groups, stride=SUBLANES)
        for k in halves:
            a_ref[k] = a[:, k * LANES:(k + 1) * LANES]
            b_ref[k] = b[:, k * LANES:(k + 1) * LANES]
        a_end = jnp.concatenate([a_ref[k, ends, :] for k in halves], axis=-1)
        b_end = jnp.concatenate([b_ref[k, ends, :] for k in halves], axis=-1)
        a_end, b_end = doubling(a_end, b_end, group, groups, 0)
        h_end = a_end * h_prev + b_end
        h_ref[...] = jnp.where(group == 0, h_prev, pltpu.roll(h_end, 1, 0))
        h_start = jnp.concatenate([jnp.broadcast_to(h_ref[g:g + 1, :], (SUBLANES, LRU_WIDTH))
                                   for g in range(groups)], axis=0)
        h = a * h_start + b
        y_lru = h * jax.nn.gelu(xg)

        y_ref[pl.ds(r0, rows), 0:POOL_WIDTH] = y_pool.astype(BF16)
        y_ref[pl.ds(r0, rows), POOL_WIDTH:MIX_WIDTH] = y_lru.astype(BF16)
        return u[rows - pool_hist:], xr[rows - conv_hist:], h_end[groups - 1:groups]

    init = (jnp.zeros((pool_hist, POOL_WIDTH), F32), jnp.zeros((conv_hist, LRU_WIDTH), F32),
            jnp.zeros((1, LRU_WIDTH), F32))
    lax.fori_loop(0, seq // rows, chunk, init)


def _block_diag(w):
    g, c, d = w.shape
    eye = jnp.eye(g, dtype=w.dtype)
    return (eye[:, None, :, None] * w[:, :, None, :]).reshape(g * c, g * d)


def _mixer(rest, pool_w, pool_scale, conv_w, conv_b, wa, ba, wi, bi, lru_lambda, batch, seq):
    t = rest.shape[0]
    full = lambda shape: pl.BlockSpec(shape, lambda b: (0, 0))
    vec = lambda a: a.reshape(1, -1)
    return pl.pallas_call(
        functools.partial(_mixer_body, seq=seq),
        grid=(batch,),
        in_specs=[pl.BlockSpec((seq, REST_WIDTH), lambda b: (b, 0)),
                  full((POOL_WIDTH, POOL_WIDTH)), full((1, POOL_WIDTH)),
                  full((CONV_WIDTH, LRU_WIDTH)), full((1, LRU_WIDTH)),
                  full((LRU_WIDTH, LRU_WIDTH)), full((1, LRU_WIDTH)),
                  full((LRU_WIDTH, LRU_WIDTH)), full((1, LRU_WIDTH)),
                  full((1, LRU_WIDTH))],
        out_specs=pl.BlockSpec((seq, MIX_WIDTH), lambda b: (b, 0)),
        out_shape=jax.ShapeDtypeStruct((t, MIX_WIDTH), BF16),
        scratch_shapes=[pltpu.VMEM((LRU_WIDTH // LANES, MIX_ROWS, LANES), F32),
                        pltpu.VMEM((LRU_WIDTH // LANES, MIX_ROWS, LANES), F32),
                        pltpu.VMEM((MIX_ROWS // SUBLANES, LRU_WIDTH), F32),
                        pltpu.VMEM((MIX_ROWS, POOL_WIDTH), F32)],
        compiler_params=_params("arbitrary"),
        name="pool_lru_mixer",
    )(rest, _block_diag(pool_w).astype(BF16), vec(pool_scale), conv_w, vec(conv_b),
      _block_diag(wa).astype(BF16), vec(ba), _block_diag(wi).astype(BF16), vec(bi), vec(lru_lambda))


def _outproj_body(ya_ref, ym_ref, x_ref, w_ref, g_ref, b_ref, *rest, alpha, route):
    if route:
        wr_ref, o_ref, ot_ref, meta_ref, gate_ref, cnt_ref, wbf_ref, run_ref = rest
    else:
        o_ref, wbf_ref = rest

    @pl.when(pl.program_id(0) == 0)
    def _cast_weights():
        for c in range(0, D_MODEL, 256):
            wbf_ref[:, c:c + 256] = w_ref[0, :, c:c + 256].astype(BF16)

    mix = jnp.dot(ya_ref[...], wbf_ref[0:ATTN_WIDTH, :], preferred_element_type=F32)
    mix = mix + jnp.dot(ym_ref[...], wbf_ref[ATTN_WIDTH:, :], preferred_element_type=F32)
    x1 = _layer_norm(alpha * x_ref[...] + mix, g_ref[...], b_ref[...])
    o_ref[...] = x1
    if route:
        _store_row_tiles(ot_ref, x1)
        _route(x1, wr_ref, meta_ref, gate_ref, cnt_ref, run_ref)


def _outproj(y_attn, y_mix, x, w, layer, g, b, alpha, w_router=None):
    t = x.shape[0]
    route = w_router is not None
    row = lambda i: (i, 0)
    const = lambda i: (0, 0)
    in_specs = [pl.BlockSpec((ROW_TILE, ATTN_WIDTH), row), pl.BlockSpec((ROW_TILE, MIX_WIDTH), row),
                pl.BlockSpec((ROW_TILE, D_MODEL), row),
                pl.BlockSpec((1, D_MODEL, D_MODEL), lambda i: (layer, 0, 0)),
                pl.BlockSpec((1, D_MODEL), const), pl.BlockSpec((1, D_MODEL), const)]
    args = [y_attn, y_mix, x, w, g.reshape(1, -1), b.reshape(1, -1)]
    out_specs = [pl.BlockSpec((ROW_TILE, D_MODEL), row)]
    out_shape = [jax.ShapeDtypeStruct((t, D_MODEL), F32)]
    scratch = [pltpu.VMEM((D_MODEL, D_MODEL), BF16)]
    if route:
        in_specs.append(pl.BlockSpec((N_EXPERTS, D_MODEL), const))
        args.append(w_router.T)
        out_specs += [pl.BlockSpec((ROW_TILE * SUBLANES, LANES), row),
                      pl.BlockSpec((ROW_TILE, ROUTE_META), row), pl.BlockSpec((ROW_TILE, LANES), row),
                      pl.BlockSpec((SUBLANES, LANES), const)]
        out_shape += [jax.ShapeDtypeStruct((t * SUBLANES, LANES), F32),
                      jax.ShapeDtypeStruct((t, ROUTE_META), jnp.int32), jax.ShapeDtypeStruct((t, LANES), F32),
                      jax.ShapeDtypeStruct((SUBLANES, LANES), F32)]
        scratch.append(pltpu.VMEM((1, LANES), F32))
    return pl.pallas_call(
        functools.partial(_outproj_body, alpha=alpha, route=route),
        grid=(t // ROW_TILE,),
        in_specs=in_specs,
        out_specs=out_specs,
        out_shape=out_shape,
        scratch_shapes=scratch,
        compiler_params=_params("arbitrary"),
        name="outproj_ln_route" if route else "outproj_ln",
    )(*args)


def _dense_ffn_body(x_ref, wg_hbm, wu_hbm, wd_hbm, g_ref, b_ref, o_ref,
                    wg_ref, wu_ref, wd_ref, stage_in_ref, stage_out_ref, xb_ref, acc_ref, sem, *, layer, alpha):
    n_chunks = 3 * FFN_STEPS

    def chunk_copy(k):
        which, c = divmod(k, FFN_STEPS)
        cols = pl.ds(c * FFN_COLS, FFN_COLS)
        if which < 2:
            src, dst = (wg_hbm, wu_hbm)[which].at[layer, :, cols], stage_in_ref.at[k % 2]
        else:
            src, dst = wd_hbm.at[layer, cols, :], stage_out_ref.at[k % 2]
        return pltpu.make_async_copy(src, dst, sem.at[k % 2])

    @pl.when(pl.program_id(0) == 0)
    def _load_weights():
        chunk_copy(0).start()
        for k in range(n_chunks):
            if k + 1 < n_chunks:
                chunk_copy(k + 1).start()
            chunk_copy(k).wait()
            which, c = divmod(k, FFN_STEPS)
            if which == 0:
                wg_ref[c] = stage_in_ref[k % 2].astype(BF16)
            elif which == 1:
                wu_ref[c] = stage_in_ref[k % 2].astype(BF16)
            else:
                wd_ref[c] = stage_out_ref[k % 2].astype(BF16)

    xb_ref[...] = x_ref[...].astype(BF16)
    acc_ref[...] = jnp.zeros_like(acc_ref)

    def tile(c, _):
        xb = xb_ref[...]
        gate = jnp.dot(xb, wg_ref[c], preferred_element_type=F32)
        up = jnp.dot(xb, wu_ref[c], preferred_element_type=F32)
        hidden = (jax.nn.silu(gate) * up).astype(BF16)
        acc_ref[...] += jnp.dot(hidden, wd_ref[c], preferred_element_type=F32)
        return 0
    lax.fori_loop(0, FFN_STEPS, tile, 0)
    o_ref[...] = _layer_norm(alpha * x_ref[...] + acc_ref[...], g_ref[...], b_ref[...])


def _dense_ffn(x, w_gate, w_up, w_down, layer, g, b, alpha):
    t = x.shape[0]
    row = lambda i: (i, 0)
    const = lambda i: (0, 0)
    any_spec = pl.BlockSpec(memory_space=pl.ANY)
    return pl.pallas_call(
        functools.partial(_dense_ffn_body, layer=layer, alpha=alpha),
        grid=(t // FFN_ROWS,),
        in_specs=[pl.BlockSpec((FFN_ROWS, D_MODEL), row), any_spec, any_spec, any_spec,
                  pl.BlockSpec((1, D_MODEL), const), pl.BlockSpec((1, D_MODEL), const)],
        out_specs=pl.BlockSpec((FFN_ROWS, D_MODEL), row),
        out_shape=jax.ShapeDtypeStruct((t, D_MODEL), F32),
        scratch_shapes=[pltpu.VMEM((FFN_STEPS, D_MODEL, FFN_COLS), BF16),
                        pltpu.VMEM((FFN_STEPS, D_MODEL, FFN_COLS), BF16),
                        pltpu.VMEM((FFN_STEPS, FFN_COLS, D_MODEL), BF16),
                        pltpu.VMEM((2, D_MODEL, FFN_COLS), F32), pltpu.VMEM((2, FFN_COLS, D_MODEL), F32),
                        pltpu.VMEM((FFN_ROWS, D_MODEL), BF16), pltpu.VMEM((FFN_ROWS, D_MODEL), F32),
                        pltpu.SemaphoreType.DMA((2,))],
        compiler_params=_params("arbitrary"),
        name="swiglu_ln",
    )(x, w_gate, w_up, w_down, g.reshape(1, -1), b.reshape(1, -1))


def _store_row_tiles(o_ref, val):
    rows = val.shape[0]
    for s in range(SUBLANES):
        o_ref[pl.ds(s, rows, stride=SUBLANES), :] = val[:, s * LANES:(s + 1) * LANES]


def _load_row_tiles(ref, first_row, rows):
    return [ref[pl.ds(first_row * SUBLANES + s, rows, stride=SUBLANES), :] for s in range(SUBLANES)]


def _tile_window(row):
    return pl.ds(pl.multiple_of(row * SUBLANES, SUBLANES), SUBLANES)


def _start_row_copies(n_rows, make_copy, priorities):
    k = len(priorities)

    def body(q, _):
        for p, priority in enumerate(priorities):
            make_copy(q * k + p).start(priority=priority)
        return 0
    lax.fori_loop(0, n_rows // k, body, 0, unroll=DMA_ISSUE_UNROLL // k)


def _gather_copies(src_ref, x_hbm, buf_ref, sem, slot):
    def copy(r):
        return pltpu.make_async_copy(x_hbm.at[_tile_window(src_ref[0, 0, r])], buf_ref.at[slot, _tile_window(r)],
                                     sem.at[slot])
    return copy


def _expert_ffn_body(be_ref, nu_ref, src0_ref, src_next_ref, x_hbm, wg_hbm, wu_hbm, wd_hbm, o_ref,
                     xb_ref, acc_ref, buf_ref, wg_buf, wu_buf, wd_buf, row_sem, w_sem):
    i = pl.program_id(0)
    n_used = nu_ref[0]
    used = i < n_used
    slot = lax.rem(i, 2)
    row_priority = 1

    def wait_rows(which):
        pltpu.make_async_copy(buf_ref.at[which], buf_ref.at[which], row_sem.at[which]).wait()

    def weight_copies(block, c, wslot):
        e = be_ref[block]
        cols = pl.ds(pl.multiple_of(c * FFN_COLS, FFN_COLS), FFN_COLS)
        return (pltpu.make_async_copy(wg_hbm.at[e, :, cols], wg_buf.at[wslot], w_sem.at[wslot]),
                pltpu.make_async_copy(wu_hbm.at[e, :, cols], wu_buf.at[wslot], w_sem.at[wslot]),
                pltpu.make_async_copy(wd_hbm.at[e, cols, :], wd_buf.at[wslot], w_sem.at[wslot]))

    @pl.when(jnp.logical_and(used, i == 0))
    def _first_copies():
        for k in range(WEIGHT_TILES_AHEAD):
            for copy in weight_copies(0, k, k):
                copy.start()
        _start_row_copies(FFN_GATHER_ROWS, _gather_copies(src0_ref, x_hbm, buf_ref, row_sem, 0),
                          priorities=(row_priority,))

    @pl.when(used)
    def _block():
        def rows_to_matrix(which):
            for s, part in enumerate(_load_row_tiles(buf_ref.at[which], 0, FFN_ROWS)):
                xb_ref[which, :, s * LANES:(s + 1) * LANES] = part.astype(BF16)

        @pl.when(i == 0)
        def _first_block_rows():
            wait_rows(0)
            rows_to_matrix(0)

        acc_ref[...] = jnp.zeros_like(acc_ref)
        next_block = jnp.minimum(i + 1, n_used - 1)
        next_row = _gather_copies(src_next_ref, x_hbm, buf_ref, row_sem, 1 - slot)

        def tile(c, start_rows, finish_rows):
            q = i * FFN_STEPS + c
            wslot = lax.rem(q, WEIGHT_SLOTS)
            ahead = c + WEIGHT_TILES_AHEAD
            wraps = ahead >= FFN_STEPS
            for copy in weight_copies(jnp.where(wraps, next_block, i), jnp.where(wraps, ahead - FFN_STEPS, ahead),
                                      lax.rem(q + WEIGHT_TILES_AHEAD, WEIGHT_SLOTS)):
                copy.start()
            for copy in weight_copies(i, c, wslot):
                copy.wait()
            if start_rows:
                for r in range(FFN_GATHER_ROWS_PER_STEP):
                    next_row(c * FFN_GATHER_ROWS_PER_STEP + r).start(priority=row_priority)
            if finish_rows:
                wait_rows(1 - slot)
                rows_to_matrix(1 - slot)
            xb = xb_ref[slot]
            gate = jnp.dot(xb, wg_buf[wslot].astype(BF16), preferred_element_type=F32)
            up = jnp.dot(xb, wu_buf[wslot].astype(BF16), preferred_element_type=F32)
            hidden = (jax.nn.silu(gate) * up).astype(BF16)
            acc_ref[...] += jnp.dot(hidden, wd_buf[wslot].astype(BF16), preferred_element_type=F32)

        def issuing_tile(c, _):
            tile(c, True, False)
            return 0
        lax.fori_loop(0, FFN_ISSUE_STEPS, issuing_tile, 0)
        for c in range(FFN_ISSUE_STEPS, FFN_STEPS):
            tile(c, False, c == FFN_STEPS - 1)

        @pl.when(i == n_used - 1)
        def _drain():
            for k in range(WEIGHT_TILES_AHEAD):
                for copy in weight_copies(next_block, k, lax.rem((i + 1) * FFN_STEPS + k, WEIGHT_SLOTS)):
                    copy.wait()
        _store_row_tiles(o_ref, acc_ref[...])

    @pl.when(jnp.logical_not(used))
    def _empty():
        o_ref[...] = jnp.zeros_like(o_ref)


def _expert_ffn(block_e, n_used, x_tiles, w_gate, w_up, w_down, src):
    n_blocks = block_e.shape[0]
    dummy = jnp.broadcast_to(jnp.arange(FFN_GATHER_ROWS - FFN_ROWS, dtype=jnp.int32),
                             (n_blocks, 1, FFN_GATHER_ROWS - FFN_ROWS))
    src3 = jnp.concatenate([src.reshape(n_blocks, 1, FFN_ROWS), dummy], axis=-1)
    any_spec = pl.BlockSpec(memory_space=pl.ANY)
    in_specs = [pl.BlockSpec((1, 1, FFN_GATHER_ROWS), lambda i, be, nu: (0, 0, 0), memory_space=pltpu.SMEM),
                pl.BlockSpec((1, 1, FFN_GATHER_ROWS), lambda i, be, nu: (jnp.minimum(i + 1, n_blocks - 1), 0, 0),
                             memory_space=pltpu.SMEM),
                any_spec, any_spec, any_spec, any_spec]
    return pl.pallas_call(
        _expert_ffn_body,
        grid_spec=pltpu.PrefetchScalarGridSpec(
            num_scalar_prefetch=2,
            grid=(n_blocks,),
            in_specs=in_specs,
            out_specs=pl.BlockSpec((FFN_ROWS * SUBLANES, LANES), lambda i, be, nu: (i, 0)),
            scratch_shapes=[pltpu.VMEM((2, FFN_ROWS, D_MODEL), BF16), pltpu.VMEM((FFN_ROWS, D_MODEL), F32),
                            pltpu.VMEM((2, FFN_GATHER_ROWS * SUBLANES, LANES), F32),
                            pltpu.VMEM((WEIGHT_SLOTS, D_MODEL, FFN_COLS), F32),
                            pltpu.VMEM((WEIGHT_SLOTS, D_MODEL, FFN_COLS), F32),
                            pltpu.VMEM((WEIGHT_SLOTS, FFN_COLS, D_MODEL), F32),
                            pltpu.SemaphoreType.DMA((2,)), pltpu.SemaphoreType.DMA((WEIGHT_SLOTS,))]),
        out_shape=jax.ShapeDtypeStruct((n_blocks * FFN_ROWS * SUBLANES, LANES), F32),
        compiler_params=_params("arbitrary"),
        name="swiglu_experts",
    )(block_e, n_used, src3, src3, x_tiles, w_gate, w_up, w_down)


def _route(x, w_ref, meta_ref, gate_ref, cnt_ref, run_ref):
    tm = x.shape[0]

    @pl.when(pl.program_id(0) == 0)
    def _init():
        run_ref[...] = jnp.zeros_like(run_ref)

    lane_i = lax.broadcasted_iota(jnp.int32, (tm, LANES), 1)
    lane = lane_i.astype(F32)
    logits = jnp.full((tm, LANES), -jnp.inf, F32)
    for e in range(N_EXPERTS):
        logit_e = jnp.sum(x * w_ref[e:e + 1, :], axis=-1, keepdims=True)
        logits = jnp.where(lane_i == e, logit_e, logits)
    m1 = jnp.max(logits, axis=-1, keepdims=True)
    e1 = jnp.min(jnp.where(logits == m1, lane, float(LANES)), axis=-1, keepdims=True)
    rest = jnp.where(lane == e1, -jnp.inf, logits)
    m2 = jnp.max(rest, axis=-1, keepdims=True)
    e2 = jnp.min(jnp.where(rest == m2, lane, float(LANES)), axis=-1, keepdims=True)
    ex = jnp.exp(m2 - m1)
    g1 = 1.0 / (1.0 + ex)
    g2 = ex / (1.0 + ex)

    chosen = jnp.logical_or(lane == e1, lane == e2)
    r_i = lax.broadcasted_iota(jnp.int32, (tm, tm), 0)
    c_i = lax.broadcasted_iota(jnp.int32, (tm, tm), 1)
    earlier = (c_i < r_i).astype(BF16)
    before = jnp.dot(earlier, chosen.astype(BF16), preferred_element_type=F32) + run_ref[...]
    rank1 = jnp.sum(jnp.where(lane == e1, before, 0.0), axis=-1, keepdims=True)
    rank2 = jnp.sum(jnp.where(lane == e2, before, 0.0), axis=-1, keepdims=True)
    total = run_ref[...] + jnp.sum(chosen.astype(F32), axis=0, keepdims=True)
    run_ref[...] = total
    cnt_ref[...] = jnp.broadcast_to(total, cnt_ref.shape)
    meta = jnp.where(lane_i == 0, e1, jnp.where(lane_i == 1, e2, jnp.where(lane_i == 2, rank1, rank2)))
    meta_ref[...] = meta[:, :ROUTE_META].astype(jnp.int32)
    gate_ref[...] = jnp.where(lane_i == 0, g1, g2)


def _combine_body(d0_ref, dn_ref, ys_hbm, gate_ref, x_ref, g_ref, b_ref, o_ref, buf_ref, sem, *, alpha):
    i = pl.program_id(0)
    slot = lax.rem(i, 2)
    n_rows = buf_ref.shape[1] // SUBLANES

    def row_copy(d_ref, to_slot):
        def copy(r):
            return pltpu.make_async_copy(ys_hbm.at[_tile_window(d_ref[0, 0, r])],
                                         buf_ref.at[to_slot, _tile_window(r)], sem.at[to_slot])
        return copy

    def wait_slot(which):
        pltpu.make_async_copy(buf_ref.at[which], buf_ref.at[which], sem.at[which]).wait()

    @pl.when(i == 0)
    def _first_rows():
        _start_row_copies(n_rows, row_copy(d0_ref, 0), priorities=(0, 1))

    wait_slot(slot)
    next_copy = row_copy(dn_ref, 1 - slot)
    for r in range(n_rows):
        next_copy(r).start(priority=r % 2)

    gates = gate_ref[...]
    g1, g2 = gates[:, 0:1], gates[:, 1:2]
    first = _load_row_tiles(buf_ref.at[slot], 0, MOVE_ROWS)
    second = _load_row_tiles(buf_ref.at[slot], MOVE_ROWS, MOVE_ROWS)
    f = jnp.concatenate([g1 * a + g2 * b for a, b in zip(first, second)], axis=-1)
    o_ref[...] = _layer_norm(alpha * x_ref[...] + f, g_ref[...], b_ref[...])

    @pl.when(i == pl.num_programs(0) - 1)
    def _drain():
        wait_slot(1 - slot)


def _combine(ys, dest1, dest2, gates, x, g, b, alpha):
    t = x.shape[0]
    steps = t // MOVE_ROWS
    dest = jnp.concatenate([dest1.reshape(steps, 1, MOVE_ROWS), dest2.reshape(steps, 1, MOVE_ROWS)], axis=-1)
    idx_shape = (1, 1, TOP_K * MOVE_ROWS)
    row = lambda i: (i, 0)
    const = lambda i: (0, 0)
    return pl.pallas_call(
        functools.partial(_combine_body, alpha=alpha),
        grid=(steps,),
        in_specs=[pl.BlockSpec(idx_shape, lambda i: (0, 0, 0), memory_space=pltpu.SMEM),
                  pl.BlockSpec(idx_shape, lambda i: (jnp.minimum(i + 1, steps - 1), 0, 0), memory_space=pltpu.SMEM),
                  pl.BlockSpec(memory_space=pl.ANY),
                  pl.BlockSpec((MOVE_ROWS, LANES), row), pl.BlockSpec((MOVE_ROWS, D_MODEL), row),
                  pl.BlockSpec((1, D_MODEL), const), pl.BlockSpec((1, D_MODEL), const)],
        out_specs=pl.BlockSpec((MOVE_ROWS, D_MODEL), row),
        out_shape=jax.ShapeDtypeStruct((t, D_MODEL), F32),
        scratch_shapes=[pltpu.VMEM((2, TOP_K * MOVE_ROWS * SUBLANES, LANES), F32), pltpu.SemaphoreType.DMA((2,))],
        compiler_params=_params("arbitrary"),
        name="combine_ln",
    )(dest, dest, ys, gates, x, g.reshape(1, -1), b.reshape(1, -1))


def _invert_body(d1_ref, d2_ref, fill_hbm, src_ref, sem):
    fill = pltpu.make_async_copy(fill_hbm, src_ref, sem)
    fill.start()
    fill.wait()

    def place(t, _):
        src_ref[d1_ref[t]] = t
        src_ref[d2_ref[t]] = t
        return 0
    lax.fori_loop(0, d1_ref.shape[0], place, 0, unroll=SCALAR_LOOP_UNROLL)


def _invert_placement(dest1, dest2, n_rows):
    smem = pl.BlockSpec(memory_space=pltpu.SMEM)
    return pl.pallas_call(
        _invert_body,
        in_specs=[smem, smem, pl.BlockSpec(memory_space=pl.ANY)],
        out_specs=smem,
        out_shape=jax.ShapeDtypeStruct((n_rows,), jnp.int32),
        scratch_shapes=[pltpu.SemaphoreType.DMA(())],
        name="invert_placement",
    )(dest1, dest2, jnp.arange(n_rows, dtype=jnp.int32) % dest1.shape[0])


def _moe(x1, x1_tiles, meta, gates, counts, w_gate, w_up, w_down, first_expert, g, b, alpha):
    t = x1.shape[0]
    e1, e2, rank1, rank2 = meta[:, 0], meta[:, 1], meta[:, 2], meta[:, 3]
    sizes = counts[0, :N_EXPERTS].astype(jnp.int32)
    padded = (sizes + FFN_ROWS - 1) // FFN_ROWS * FFN_ROWS
    group_end = jnp.cumsum(padded)
    group_start = group_end - padded
    dest1 = group_start[e1] + rank1
    dest2 = group_start[e2] + rank2
    n_blocks = (t * TOP_K) // FFN_ROWS + N_EXPERTS
    block_start = jnp.arange(n_blocks, dtype=jnp.int32) * FFN_ROWS
    block_e = jnp.minimum(jnp.sum(group_end[None, :] <= block_start[:, None], axis=1), N_EXPERTS - 1)
    n_used = (group_end[-1] // FFN_ROWS).reshape(1)
    src = _invert_placement(dest1, dest2, n_blocks * FFN_ROWS)
    ys = _expert_ffn((block_e + first_expert).astype(jnp.int32), n_used.astype(jnp.int32), x1_tiles, w_gate,
                     w_up, w_down, src)
    return _combine(ys, dest1, dest2, gates, x1, g, b, alpha)


def kernel(x, w_in, w_out, attn_lambda, attn_subln_g, pool_w, pool_scale, conv_w, conv_b, lru_wa, lru_ba,
           lru_wi, lru_bi, lru_lambda, ln1_g, ln1_b, ln2_g, ln2_b, ffn_w_gate, ffn_w_up, ffn_w_down,
           router_w, moe_w_gate, moe_w_up, moe_w_down):
    batch, seq, d = x.shape
    depth = w_in.shape[0]
    assert d == D_MODEL and seq % ATTN_TILE == 0 and seq % MIX_ROWS == 0
    t = batch * seq
    assert t % FFN_ROWS == 0 and t % ROW_TILE == 0
    alpha = (2.0 * depth) ** 0.25
    moe_gate = moe_w_gate.reshape(-1, D_MODEL, D_FF)
    moe_up = moe_w_up.reshape(-1, D_MODEL, D_FF)
    moe_down = moe_w_down.reshape(-1, D_FF, D_MODEL)
    xt = x.reshape(t, d)
    for l in range(depth):
        lambda_init = 0.8 - 0.6 * math.exp(-0.3 * l)
        q, k, v, rest = _inproj(xt, w_in, l)
        y_attn = _attention(q, k, v, attn_lambda[l], attn_subln_g[l], lambda_init, batch, seq)
        y_mix = _mixer(rest, pool_w[l], pool_scale[l], conv_w[l], conv_b[l], lru_wa[l], lru_ba[l],
                       lru_wi[l], lru_bi[l], lru_lambda[l], batch, seq)
        if l % 2 == 0:
            x1, = _outproj(y_attn, y_mix, xt, w_out, l, ln1_g[l], ln1_b[l], alpha)
            xt = _dense_ffn(x1, ffn_w_gate, ffn_w_up, ffn_w_down, l // 2, ln2_g[l], ln2_b[l], alpha)
        else:
            x1, x1_tiles, meta, gates, counts = _outproj(y_attn, y_mix, xt, w_out, l, ln1_g[l], ln1_b[l], alpha,
                                                         w_router=router_w[l // 2])
            xt = _moe(x1, x1_tiles, meta, gates, counts, moe_gate, moe_up, moe_down, (l // 2) * N_EXPERTS,
                      ln2_g[l], ln2_b[l], alpha)
    return xt.reshape(batch, seq, d)
```

```python
import functools
import math

import jax
import jax.numpy as jnp
from jax import lax
from jax.experimental import pallas as pl
from jax.experimental.pallas import tpu as pltpu

F32 = jnp.float32
BF16 = jnp.bfloat16

D_MODEL = 1024
CHUNK = 64
ATTN_HEADS = 4
ATTN_WIDTH = 512
ATTN_HEAD_DIM = 64
HEAD_WIDTH = 2 * ATTN_HEAD_DIM
POOL_WINDOWS = (2, 4, 8, 16)
POOL_WIDTH = 256
POOL_GROUP_DIM = 64
LRU_WIDTH = 256
LRU_C = 8.0
CONV_WIDTH = 4
REST_WIDTH = POOL_WIDTH + 2 * LRU_WIDTH
IN_WIDTH = 3 * ATTN_WIDTH + REST_WIDTH
MIX_WIDTH = POOL_WIDTH + LRU_WIDTH
D_FF = 2816
N_EXPERTS = 8
TOP_K = 2
LN_EPS = 1e-5
HEAD_NORM_EPS = 1e-5

LANES = 128
SUBLANES = 8
VMEM_LIMIT_BYTES = 56 * 1024 * 1024

ROW_TILE = 1024
ATTN_TILE = 256
ATTN_HEADS_PER_STEP = 4
ATTN_SUM_ROWS = 16
ATTN_SCORE_SCALE = ATTN_HEAD_DIM ** -0.5 * math.log2(math.e)
MIX_ROWS = 256
FFN_ROWS = 1024
FFN_COLS = 256
FFN_STEPS = D_FF // FFN_COLS
WEIGHT_TILES_AHEAD = 2
WEIGHT_SLOTS = WEIGHT_TILES_AHEAD + 1
FFN_ISSUE_STEPS = FFN_STEPS - 2
FFN_GATHER_ROWS_PER_STEP = -(-FFN_ROWS // (FFN_ISSUE_STEPS * SUBLANES)) * SUBLANES
FFN_GATHER_ROWS = FFN_GATHER_ROWS_PER_STEP * FFN_ISSUE_STEPS
MOVE_ROWS = 512
ROUTE_META = 8
SCALAR_LOOP_UNROLL = 8
DMA_ISSUE_UNROLL = 8
NEG_BIG = -1e30
F32_TINY = float(jnp.finfo(jnp.float32).tiny)
assert MIX_ROWS >= max(POOL_WINDOWS)


def _params(*semantics):
    return pltpu.CompilerParams(dimension_semantics=semantics, vmem_limit_bytes=VMEM_LIMIT_BYTES)


def _layer_norm(z, g, b):
    mu = jnp.mean(z, axis=-1, keepdims=True)
    zc = z - mu
    var = jnp.mean(zc * zc, axis=-1, keepdims=True)
    return zc * lax.rsqrt(var + LN_EPS) * g + b


def _inproj_body(x_ref, w_ref, q_ref, k_ref, v_ref, r_ref, wbf_ref):
    @pl.when(pl.program_id(0) == 0)
    def _cast_weights():
        for c in range(0, IN_WIDTH, 256):
            wbf_ref[:, c:c + 256] = w_ref[0, :, c:c + 256].astype(BF16)

    xb = x_ref[...].astype(BF16)

    def proj(c0, c1):
        return jnp.dot(xb, wbf_ref[:, c0:c1], preferred_element_type=F32)

    q_ref[...] = (proj(0, ATTN_WIDTH) * ATTN_SCORE_SCALE).astype(BF16)
    k_ref[...] = proj(ATTN_WIDTH, 2 * ATTN_WIDTH).astype(BF16)
    v_ref[...] = proj(2 * ATTN_WIDTH, 3 * ATTN_WIDTH).astype(BF16)
    r_ref[...] = proj(3 * ATTN_WIDTH, IN_WIDTH)


def _inproj(x, w, layer):
    t = x.shape[0]
    row = lambda i: (i, 0)
    return pl.pallas_call(
        _inproj_body,
        grid=(t // ROW_TILE,),
        in_specs=[pl.BlockSpec((ROW_TILE, D_MODEL), row),
                  pl.BlockSpec((1, D_MODEL, IN_WIDTH), lambda i: (layer, 0, 0))],
        out_specs=[pl.BlockSpec((ROW_TILE, ATTN_WIDTH), row)] * 3
        + [pl.BlockSpec((ROW_TILE, REST_WIDTH), row)],
        out_shape=[jax.ShapeDtypeStruct((t, ATTN_WIDTH), BF16)] * 3
        + [jax.ShapeDtypeStruct((t, REST_WIDTH), F32)],
        scratch_shapes=[pltpu.VMEM((D_MODEL, IN_WIDTH), BF16)],
        compiler_params=_params("arbitrary"),
        name="inproj",
    )(x, w)


def _attn_body(lam_ref, g_ref, q_ref, k_ref, v_ref, o_ref, vt_ref, *, lambda_init, seq):
    tq = ATTN_TILE
    lv = lam_ref[...]
    lam = (jnp.exp(jnp.sum(lv[0:1] * lv[1:2], axis=-1, keepdims=True))
           - jnp.exp(jnp.sum(lv[2:3] * lv[3:4], axis=-1, keepdims=True)) + lambda_init)
    gain = g_ref[...] * (1.0 - lambda_init)

    heads = range(ATTN_HEADS_PER_STEP)
    cols = [slice(g * HEAD_WIDTH, (g + 1) * HEAD_WIDTH) for g in heads]
    for g in heads:
        for c in range(seq // tq):
            vt_ref[g, c, 0:HEAD_WIDTH, :] = v_ref[c * tq:(c + 1) * tq, cols[g]].astype(F32).T.astype(BF16)
            vt_ref[g, c, HEAD_WIDTH:, :] = jnp.ones((ATTN_SUM_ROWS, tq), BF16)

    lane = lax.broadcasted_iota(jnp.int32, (tq, HEAD_WIDTH), 1)
    first_map = lane < ATTN_HEAD_DIM
    key = lax.broadcasted_iota(jnp.int32, (tq, 2 * tq), 0)
    qry = lax.broadcasted_iota(jnp.int32, (tq, 2 * tq), 1)
    q_chunk = jnp.where(qry >= tq, qry - tq, qry) // CHUNK
    visible = (key // CHUNK) <= q_chunk

    def q_block(i, _):
        q0 = pl.multiple_of(i * tq, tq)
        qqs = []
        for g in heads:
            q = q_ref[pl.ds(q0, tq), cols[g]]
            zero = jnp.zeros_like(q)
            qqs.append(jnp.concatenate([jnp.where(first_map, q, zero), jnp.where(first_map, zero, q)], axis=0))

        def step(j, carry, masked):
            k0 = pl.multiple_of(j * tq, tq)
            out = []
            scores = [lax.dot_general(k_ref[pl.ds(k0, tq), cols[g]], qqs[g], (((1,), (1,)), ((), ())),
                                      preferred_element_type=F32) for g in heads]
            for g in heads:
                m, acc = carry[g]
                s = scores[g]
                if masked:
                    s = jnp.where(visible, s, NEG_BIG)
                m_new = jnp.maximum(m, jnp.max(s, axis=0, keepdims=True))
                p = jnp.exp2(s - m_new)
                scale = jnp.exp2(m - m_new)
                pv = jnp.dot(vt_ref[g, j], p.astype(BF16), preferred_element_type=F32)
                out.append((m_new, scale * acc + pv))
            return tuple(out)

        init = tuple((jnp.full((1, 2 * tq), NEG_BIG, F32),
                      jnp.zeros((HEAD_WIDTH + ATTN_SUM_ROWS, 2 * tq), F32)) for _ in heads)
        carry = lax.fori_loop(0, i, lambda j, c: step(j, c, False), init)
        carry = step(i, carry, True)
        for g in heads:
            _, acc = carry[g]
            l = acc[HEAD_WIDTH:HEAD_WIDTH + 1, :]
            acc = acc[:HEAD_WIDTH, :]
            o = acc[:, :tq] / l[:, :tq] - lam * (acc[:, tq:] / l[:, tq:])
            o = o * lax.rsqrt(jnp.mean(o * o, axis=0, keepdims=True) + HEAD_NORM_EPS)
            o_ref[pl.ds(q0, tq), cols[g]] = (o.T * gain).astype(BF16)
        return 0

    lax.fori_loop(0, seq // tq, q_block, 0)


def _attention(q, k, v, lam_params, subln_g, lambda_init, batch, seq):
    t = q.shape[0]
    blk = pl.BlockSpec((seq, ATTN_HEADS_PER_STEP * HEAD_WIDTH), lambda b, h: (b, h))
    return pl.pallas_call(
        functools.partial(_attn_body, lambda_init=lambda_init, seq=seq),
        grid=(batch, ATTN_HEADS // ATTN_HEADS_PER_STEP),
        in_specs=[pl.BlockSpec((4, ATTN_HEAD_DIM), lambda b, h: (0, 0)),
                  pl.BlockSpec((1, HEAD_WIDTH), lambda b, h: (0, 0)),
                  blk, blk, blk],
        out_specs=blk,
        out_shape=jax.ShapeDtypeStruct((t, ATTN_WIDTH), BF16),
        scratch_shapes=[pltpu.VMEM((ATTN_HEADS_PER_STEP, seq // ATTN_TILE, HEAD_WIDTH + ATTN_SUM_ROWS, ATTN_TILE),
                                   BF16)],
        compiler_params=_params("arbitrary", "arbitrary"),
        name="diff_attention",
    )(lam_params, subln_g.reshape(1, HEAD_WIDTH), q, k, v)


def _mixer_body(r_ref, pw_ref, ps_ref, cw_ref, cb_ref, wa_ref, ba_ref, wi_ref, bi_ref, lam_ref, y_ref,
                a_ref, b_ref, h_ref, inv_ref, *, seq):
    rows = MIX_ROWS
    pool_hist = max(POOL_WINDOWS)
    conv_hist = SUBLANES
    pw = pw_ref[...]
    wa = wa_ref[...]
    wi = wi_ref[...]
    ps, cb, ba, bi = ps_ref[...], cb_ref[...], ba_ref[...], bi_ref[...]
    cw = cw_ref[...]
    neg_c_softplus = -LRU_C * jax.nn.softplus(-lam_ref[...])

    lane_e = lax.broadcasted_iota(jnp.int32, (rows + pool_hist, POOL_WIDTH), 1)
    lane = lax.broadcasted_iota(jnp.int32, (rows, POOL_WIDTH), 1)
    row = lax.broadcasted_iota(jnp.int32, (rows, POOL_WIDTH), 0)
    win = jnp.where(lane < 64, 2, jnp.where(lane < 128, 4, jnp.where(lane < 192, 8, 16)))
    inv_win = jnp.where(lane < 64, 1 / 2, jnp.where(lane < 128, 1 / 4, jnp.where(lane < 192, 1 / 8, 1 / 16)))
    inv_ref[...] = 1.0 / (row + 1).astype(F32)
    groups = rows // SUBLANES
    row_in_group = lax.broadcasted_iota(jnp.int32, (groups, SUBLANES, LRU_WIDTH), 1)
    group = lax.broadcasted_iota(jnp.int32, (groups, LRU_WIDTH), 0)

    def chunk(c, carry):
        tail_u, tail_x, h_prev = carry
        r0 = pl.multiple_of(c * rows, rows)
        u = r_ref[pl.ds(r0, rows), 0:POOL_WIDTH]
        xr = r_ref[pl.ds(r0, rows), POOL_WIDTH:POOL_WIDTH + LRU_WIDTH]
        xg = r_ref[pl.ds(r0, rows), POOL_WIDTH + LRU_WIDTH:REST_WIDTH]

        ue = jnp.concatenate([tail_u, u], axis=0)
        w2 = ue + pltpu.roll(ue, 1, 0)
        w4 = w2 + pltpu.roll(w2, 2, 0)
        w8 = w4 + pltpu.roll(w4, 4, 0)
        w16 = w8 + pltpu.roll(w8, 8, 0)
        ws = jnp.where(lane_e < 64, w2, jnp.where(lane_e < 128, w4, jnp.where(lane_e < 192, w8, w16)))
        ws = ws[pool_hist:]
        inv_count = jnp.where(r0 + row + 1 >= win, inv_win, inv_ref[...])
        pooled = ws * inv_count - u
        y_pool = jnp.dot(pooled.astype(BF16), pw, preferred_element_type=F32) * ps

        xe = jnp.concatenate([tail_x, xr], axis=0)
        xc = cb + pltpu.roll(xe, 3, 0)[conv_hist:] * cw[0:1]
        xc = xc + pltpu.roll(xe, 2, 0)[conv_hist:] * cw[1:2]
        xc = xc + pltpu.roll(xe, 1, 0)[conv_hist:] * cw[2:3]
        xc = xc + xr * cw[3:4]
        xcb = xc.astype(BF16)
        r_gate = jax.nn.sigmoid(jnp.dot(xcb, wa, preferred_element_type=F32) + ba)
        i_gate = jax.nn.sigmoid(jnp.dot(xcb, wi, preferred_element_type=F32) + bi)
        log_a = r_gate * neg_c_softplus
        a = jnp.exp(log_a)
        gap = -jnp.tanh(log_a) * (a * a + 1.0)
        b = (gap * lax.rsqrt(jnp.maximum(gap, F32_TINY))) * (i_gate * xc)

        def doubling(a, b, pos, length, axis):
            s = 1
            while s < length:
                keep = pos >= s
                a_prev = jnp.where(keep, pltpu.roll(a, s, axis), 1.0)
                b_prev = jnp.where(keep, pltpu.roll(b, s, axis), 0.0)
                b = a * b_prev + b
                a = a * a_prev
                s *= 2
            return a, b

        a, b = doubling(a.reshape(groups, SUBLANES, LRU_WIDTH), b.reshape(groups, SUBLANES, LRU_WIDTH),
                        row_in_group, SUBLANES, 1)
        a = a.reshape(rows, LRU_WIDTH)
        b = b.reshape(rows, LRU_WIDTH)
        halves = range(LRU_WIDTH // LANES)
        ends = pl.ds(SUBLANES - 1, groups, stride=SUBLANES)
        for k in halves:
            a_ref[k] = a[:, k * LANES:(k + 1) * LANES]
            b_ref[k] = b[:, k * LANES:(k + 1) * LANES]
        a_end = jnp.concatenate([a_ref[k, ends, :] for k in halves], axis=-1)
        b_end = jnp.concatenate([b_ref[k, ends, :] for k in halves], axis=-1)
        a_end, b_end = doubling(a_end, b_end, group, groups, 0)
        h_end = a_end * h_prev + b_end
        h_ref[...] = jnp.where(group == 0, h_prev, pltpu.roll(h_end, 1, 0))
        h_start = jnp.concatenate([jnp.broadcast_to(h_ref[g:g + 1, :], (SUBLANES, LRU_WIDTH))
                                   for g in range(groups)], axis=0)
        h = a * h_start + b
        y_lru = h * jax.nn.gelu(xg)

        y_ref[pl.ds(r0, rows), 0:POOL_WIDTH] = y_pool.astype(BF16)
        y_ref[pl.ds(r0, rows), POOL_WIDTH:MIX_WIDTH] = y_lru.astype(BF16)
        return u[rows - pool_hist:], xr[rows - conv_hist:], h_end[groups - 1:groups]

    init = (jnp.zeros((pool_hist, POOL_WIDTH), F32), jnp.zeros((conv_hist, LRU_WIDTH), F32),
            jnp.zeros((1, LRU_WIDTH), F32))
    lax.fori_loop(0, seq // rows, chunk, init)


def _block_diag(w):
    g, c, d = w.shape
    eye = jnp.eye(g, dtype=w.dtype)
    return (eye[:, None, :, None] * w[:, :, None, :]).reshape(g * c, g * d)


def _mixer(rest, pool_w, pool_scale, conv_w, conv_b, wa, ba, wi, bi, lru_lambda, batch, seq):
    t = rest.shape[0]
    full = lambda shape: pl.BlockSpec(shape, lambda b: (0, 0))
    vec = lambda a: a.reshape(1, -1)
    return pl.pallas_call(
        functools.partial(_mixer_body, seq=seq),
        grid=(batch,),
        in_specs=[pl.BlockSpec((seq, REST_WIDTH), lambda b: (b, 0)),
                  full((POOL_WIDTH, POOL_WIDTH)), full((1, POOL_WIDTH)),
                  full((CONV_WIDTH, LRU_WIDTH)), full((1, LRU_WIDTH)),
                  full((LRU_WIDTH, LRU_WIDTH)), full((1, LRU_WIDTH)),
                  full((LRU_WIDTH, LRU_WIDTH)), full((1, LRU_WIDTH)),
                  full((1, LRU_WIDTH))],
        out_specs=pl.BlockSpec((seq, MIX_WIDTH), lambda b: (b, 0)),
        out_shape=jax.ShapeDtypeStruct((t, MIX_WIDTH), BF16),
        scratch_shapes=[pltpu.VMEM((LRU_WIDTH // LANES, MIX_ROWS, LANES), F32),
                        pltpu.VMEM((LRU_WIDTH // LANES, MIX_ROWS, LANES), F32),
                        pltpu.VMEM((MIX_ROWS // SUBLANES, LRU_WIDTH), F32),
                        pltpu.VMEM((MIX_ROWS, POOL_WIDTH), F32)],
        compiler_params=_params("arbitrary"),
        name="pool_lru_mixer",
    )(rest, _block_diag(pool_w).astype(BF16), vec(pool_scale), conv_w, vec(conv_b),
      _block_diag(wa).astype(BF16), vec(ba), _block_diag(wi).astype(BF16), vec(bi), vec(lru_lambda))


def _outproj_body(ya_ref, ym_ref, x_ref, w_ref, g_ref, b_ref, *rest, alpha, route):
    if route:
        wr_ref, o_ref, ot_ref, meta_ref, gate_ref, cnt_ref, wbf_ref, run_ref = rest
    else:
        o_ref, wbf_ref = rest

    @pl.when(pl.program_id(0) == 0)
    def _cast_weights():
        for c in range(0, D_MODEL, 256):
            wbf_ref[:, c:c + 256] = w_ref[0, :, c:c + 256].astype(BF16)

    mix = jnp.dot(ya_ref[...], wbf_ref[0:ATTN_WIDTH, :], preferred_element_type=F32)
    mix = mix + jnp.dot(ym_ref[...], wbf_ref[ATTN_WIDTH:, :], preferred_element_type=F32)
    x1 = _layer_norm(alpha * x_ref[...] + mix, g_ref[...], b_ref[...])
    o_ref[...] = x1
    if route:
        _store_row_tiles(ot_ref, x1)
        _route(x1, wr_ref, meta_ref, gate_ref, cnt_ref, run_ref)


def _outproj(y_attn, y_mix, x, w, layer, g, b, alpha, w_router=None):
    t = x.shape[0]
    route = w_router is not None
    row = lambda i: (i, 0)
    const = lambda i: (0, 0)
    in_specs = [pl.BlockSpec((ROW_TILE, ATTN_WIDTH), row), pl.BlockSpec((ROW_TILE, MIX_WIDTH), row),
                pl.BlockSpec((ROW_TILE, D_MODEL), row),
                pl.BlockSpec((1, D_MODEL, D_MODEL), lambda i: (layer, 0, 0)),
                pl.BlockSpec((1, D_MODEL), const), pl.BlockSpec((1, D_MODEL), const)]
    args = [y_attn, y_mix, x, w, g.reshape(1, -1), b.reshape(1, -1)]
    out_specs = [pl.BlockSpec((ROW_TILE, D_MODEL), row)]
    out_shape = [jax.ShapeDtypeStruct((t, D_MODEL), F32)]
    scratch = [pltpu.VMEM((D_MODEL, D_MODEL), BF16)]
    if route:
        in_specs.append(pl.BlockSpec((N_EXPERTS, D_MODEL), const))
        args.append(w_router.T)
        out_specs += [pl.BlockSpec((ROW_TILE * SUBLANES, LANES), row),
                      pl.BlockSpec((ROW_TILE, ROUTE_META), row), pl.BlockSpec((ROW_TILE, LANES), row),
                      pl.BlockSpec((SUBLANES, LANES), const)]
        out_shape += [jax.ShapeDtypeStruct((t * SUBLANES, LANES), F32),
                      jax.ShapeDtypeStruct((t, ROUTE_META), jnp.int32), jax.ShapeDtypeStruct((t, LANES), F32),
                      jax.ShapeDtypeStruct((SUBLANES, LANES), F32)]
        scratch.append(pltpu.VMEM((1, LANES), F32))
    return pl.pallas_call(
        functools.partial(_outproj_body, alpha=alpha, route=route),
        grid=(t // ROW_TILE,),
        in_specs=in_specs,
        out_specs=out_specs,
        out_shape=out_shape,
        scratch_shapes=scratch,
        compiler_params=_params("arbitrary"),
        name="outproj_ln_route" if route else "outproj_ln",
    )(*args)


def _dense_ffn_body(x_ref, wg_hbm, wu_hbm, wd_hbm, g_ref, b_ref, o_ref,
                    wg_ref, wu_ref, wd_ref, stage_in_ref, stage_out_ref, xb_ref, acc_ref, sem, *, layer, alpha):
    n_chunks = 3 * FFN_STEPS

    def chunk_copy(k):
        which, c = divmod(k, FFN_STEPS)
        cols = pl.ds(c * FFN_COLS, FFN_COLS)
        if which < 2:
            src, dst = (wg_hbm, wu_hbm)[which].at[layer, :, cols], stage_in_ref.at[k % 2]
        else:
            src, dst = wd_hbm.at[layer, cols, :], stage_out_ref.at[k % 2]
        return pltpu.make_async_copy(src, dst, sem.at[k % 2])

    @pl.when(pl.program_id(0) == 0)
    def _load_weights():
        chunk_copy(0).start()
        for k in range(n_chunks):
            if k + 1 < n_chunks:
                chunk_copy(k + 1).start()
            chunk_copy(k).wait()
            which, c = divmod(k, FFN_STEPS)
            if which == 0:
                wg_ref[c] = stage_in_ref[k % 2].astype(BF16)
            elif which == 1:
                wu_ref[c] = stage_in_ref[k % 2].astype(BF16)
            else:
                wd_ref[c] = stage_out_ref[k % 2].astype(BF16)

    xb_ref[...] = x_ref[...].astype(BF16)
    acc_ref[...] = jnp.zeros_like(acc_ref)

    def tile(c, _):
        xb = xb_ref[...]
        gate = jnp.dot(xb, wg_ref[c], preferred_element_type=F32)
        up = jnp.dot(xb, wu_ref[c], preferred_element_type=F32)
        hidden = (jax.nn.silu(gate) * up).astype(BF16)
        acc_ref[...] += jnp.dot(hidden, wd_ref[c], preferred_element_type=F32)
        return 0
    lax.fori_loop(0, FFN_STEPS, tile, 0)
    o_ref[...] = _layer_norm(alpha * x_ref[...] + acc_ref[...], g_ref[...], b_ref[...])


def _dense_ffn(x, w_gate, w_up, w_down, layer, g, b, alpha):
    t = x.shape[0]
    row = lambda i: (i, 0)
    const = lambda i: (0, 0)
    any_spec = pl.BlockSpec(memory_space=pl.ANY)
    return pl.pallas_call(
        functools.partial(_dense_ffn_body, layer=layer, alpha=alpha),
        grid=(t // FFN_ROWS,),
        in_specs=[pl.BlockSpec((FFN_ROWS, D_MODEL), row), any_spec, any_spec, any_spec,
                  pl.BlockSpec((1, D_MODEL), const), pl.BlockSpec((1, D_MODEL), const)],
        out_specs=pl.BlockSpec((FFN_ROWS, D_MODEL), row),
        out_shape=jax.ShapeDtypeStruct((t, D_MODEL), F32),
        scratch_shapes=[pltpu.VMEM((FFN_STEPS, D_MODEL, FFN_COLS), BF16),
                        pltpu.VMEM((FFN_STEPS, D_MODEL, FFN_COLS), BF16),
                        pltpu.VMEM((FFN_STEPS, FFN_COLS, D_MODEL), BF16),
                        pltpu.VMEM((2, D_MODEL, FFN_COLS), F32), pltpu.VMEM((2, FFN_COLS, D_MODEL), F32),
                        pltpu.VMEM((FFN_ROWS, D_MODEL), BF16), pltpu.VMEM((FFN_ROWS, D_MODEL), F32),
                        pltpu.SemaphoreType.DMA((2,))],
        compiler_params=_params("arbitrary"),
        name="swiglu_ln",
    )(x, w_gate, w_up, w_down, g.reshape(1, -1), b.reshape(1, -1))


def _store_row_tiles(o_ref, val):
    rows = val.shape[0]
    for s in range(SUBLANES):
        o_ref[pl.ds(s, rows, stride=SUBLANES), :] = val[:, s * LANES:(s + 1) * LANES]


def _load_row_tiles(ref, first_row, rows):
    return [ref[pl.ds(first_row * SUBLANES + s, rows, stride=SUBLANES), :] for s in range(SUBLANES)]


def _tile_window(row):
    return pl.ds(pl.multiple_of(row * SUBLANES, SUBLANES), SUBLANES)


def _start_row_copies(n_rows, make_copy, priorities):
    k = len(priorities)

    def body(q, _):
        for p, priority in enumerate(priorities):
            make_copy(q * k + p).start(priority=priority)
        return 0
    lax.fori_loop(0, n_rows // k, body, 0, unroll=DMA_ISSUE_UNROLL // k)


def _gather_copies(src_ref, x_hbm, buf_ref, sem, slot):
    def copy(r):
        return pltpu.make_async_copy(x_hbm.at[_tile_window(src_ref[0, 0, r])], buf_ref.at[slot, _tile_window(r)],
                                     sem.at[slot])
    return copy


def _expert_ffn_body(be_ref, nu_ref, src0_ref, src_next_ref, x_hbm, wg_hbm, wu_hbm, wd_hbm, o_ref,
                     xb_ref, acc_ref, buf_ref, wg_buf, wu_buf, wd_buf, row_sem, w_sem):
    i = pl.program_id(0)
    n_used = nu_ref[0]
    used = i < n_used
    slot = lax.rem(i, 2)
    row_priority = 1

    def wait_rows(which):
        pltpu.make_async_copy(buf_ref.at[which], buf_ref.at[which], row_sem.at[which]).wait()

    def weight_copies(block, c, wslot):
        e = be_ref[block]
        cols = pl.ds(pl.multiple_of(c * FFN_COLS, FFN_COLS), FFN_COLS)
        return (pltpu.make_async_copy(wg_hbm.at[e, :, cols], wg_buf.at[wslot], w_sem.at[wslot]),
                pltpu.make_async_copy(wu_hbm.at[e, :, cols], wu_buf.at[wslot], w_sem.at[wslot]),
                pltpu.make_async_copy(wd_hbm.at[e, cols, :], wd_buf.at[wslot], w_sem.at[wslot]))

    @pl.when(jnp.logical_and(used, i == 0))
    def _first_copies():
        for k in range(WEIGHT_TILES_AHEAD):
            for copy in weight_copies(0, k, k):
                copy.start()
        _start_row_copies(FFN_GATHER_ROWS, _gather_copies(src0_ref, x_hbm, buf_ref, row_sem, 0),
                          priorities=(row_priority,))

    @pl.when(used)
    def _block():
        def rows_to_matrix(which):
            for s, part in enumerate(_load_row_tiles(buf_ref.at[which], 0, FFN_ROWS)):
                xb_ref[which, :, s * LANES:(s + 1) * LANES] = part.astype(BF16)

        @pl.when(i == 0)
        def _first_block_rows():
            wait_rows(0)
            rows_to_matrix(0)

        acc_ref[...] = jnp.zeros_like(acc_ref)
        next_block = jnp.minimum(i + 1, n_used - 1)
        next_row = _gather_copies(src_next_ref, x_hbm, buf_ref, row_sem, 1 - slot)

        def tile(c, start_rows, finish_rows):
            q = i * FFN_STEPS + c
            wslot = lax.rem(q, WEIGHT_SLOTS)
            ahead = c + WEIGHT_TILES_AHEAD
            wraps = ahead >= FFN_STEPS
            for copy in weight_copies(jnp.where(wraps, next_block, i), jnp.where(wraps, ahead - FFN_STEPS, ahead),
                                      lax.rem(q + WEIGHT_TILES_AHEAD, WEIGHT_SLOTS)):
                copy.start()
            for copy in weight_copies(i, c, wslot):
                copy.wait()
            if start_rows:
                for r in range(FFN_GATHER_ROWS_PER_STEP):
                    next_row(c * FFN_GATHER_ROWS_PER_STEP + r).start(priority=row_priority)
            if finish_rows:
                wait_rows(1 - slot)
                rows_to_matrix(1 - slot)
            xb = xb_ref[slot]
            gate = jnp.dot(xb, wg_buf[wslot].astype(BF16), preferred_element_type=F32)
            up = jnp.dot(xb, wu_buf[wslot].astype(BF16), preferred_element_type=F32)
            hidden = (jax.nn.silu(gate) * up).astype(BF16)
            acc_ref[...] += jnp.dot(hidden, wd_buf[wslot].astype(BF16), preferred_element_type=F32)

        def issuing_tile(c, _):
            tile(c, True, False)
            return 0
        lax.fori_loop(0, FFN_ISSUE_STEPS, issuing_tile, 0)
        for c in range(FFN_ISSUE_STEPS, FFN_STEPS):
            tile(c, False, c == FFN_STEPS - 1)

        @pl.when(i == n_used - 1)
        def _drain():
            for k in range(WEIGHT_TILES_AHEAD):
                for copy in weight_copies(next_block, k, lax.rem((i + 1) * FFN_STEPS + k, WEIGHT_SLOTS)):
                    copy.wait()
        _store_row_tiles(o_ref, acc_ref[...])

    @pl.when(jnp.logical_not(used))
    def _empty():
        o_ref[...] = jnp.zeros_like(o_ref)


def _expert_ffn(block_e, n_used, x_tiles, w_gate, w_up, w_down, src):
    n_blocks = block_e.shape[0]
    dummy = jnp.broadcast_to(jnp.arange(FFN_GATHER_ROWS - FFN_ROWS, dtype=jnp.int32),
                             (n_blocks, 1, FFN_GATHER_ROWS - FFN_ROWS))
    src3 = jnp.concatenate([src.reshape(n_blocks, 1, FFN_ROWS), dummy], axis=-1)
    any_spec = pl.BlockSpec(memory_space=pl.ANY)
    in_specs = [pl.BlockSpec((1, 1, FFN_GATHER_ROWS), lambda i, be, nu: (0, 0, 0), memory_space=pltpu.SMEM),
                pl.BlockSpec((1, 1, FFN_GATHER_ROWS), lambda i, be, nu: (jnp.minimum(i + 1, n_blocks - 1), 0, 0),
                             memory_space=pltpu.SMEM),
                any_spec, any_spec, any_spec, any_spec]
    return pl.pallas_call(
        _expert_ffn_body,
        grid_spec=pltpu.PrefetchScalarGridSpec(
            num_scalar_prefetch=2,
            grid=(n_blocks,),
            in_specs=in_specs,
            out_specs=pl.BlockSpec((FFN_ROWS * SUBLANES, LANES), lambda i, be, nu: (i, 0)),
            scratch_shapes=[pltpu.VMEM((2, FFN_ROWS, D_MODEL), BF16), pltpu.VMEM((FFN_ROWS, D_MODEL), F32),
                            pltpu.VMEM((2, FFN_GATHER_ROWS * SUBLANES, LANES), F32),
                            pltpu.VMEM((WEIGHT_SLOTS, D_MODEL, FFN_COLS), F32),
                            pltpu.VMEM((WEIGHT_SLOTS, D_MODEL, FFN_COLS), F32),
                            pltpu.VMEM((WEIGHT_SLOTS, FFN_COLS, D_MODEL), F32),
                            pltpu.SemaphoreType.DMA((2,)), pltpu.SemaphoreType.DMA((WEIGHT_SLOTS,))]),
        out_shape=jax.ShapeDtypeStruct((n_blocks * FFN_ROWS * SUBLANES, LANES), F32),
        compiler_params=_params("arbitrary"),
        name="swiglu_experts",
    )(block_e, n_used, src3, src3, x_tiles, w_gate, w_up, w_down)


def _route(x, w_ref, meta_ref, gate_ref, cnt_ref, run_ref):
    tm = x.shape[0]

    @pl.when(pl.program_id(0) == 0)
    def _init():
        run_ref[...] = jnp.zeros_like(run_ref)

    lane_i = lax.broadcasted_iota(jnp.int32, (tm, LANES), 1)
    lane = lane_i.astype(F32)
    logits = jnp.full((tm, LANES), -jnp.inf, F32)
    for e in range(N_EXPERTS):
        logit_e = jnp.sum(x * w_ref[e:e + 1, :], axis=-1, keepdims=True)
        logits = jnp.where(lane_i == e, logit_e, logits)
    m1 = jnp.max(logits, axis=-1, keepdims=True)
    e1 = jnp.min(jnp.where(logits == m1, lane, float(LANES)), axis=-1, keepdims=True)
    rest = jnp.where(lane == e1, -jnp.inf, logits)
    m2 = jnp.max(rest, axis=-1, keepdims=True)
    e2 = jnp.min(jnp.where(rest == m2, lane, float(LANES)), axis=-1, keepdims=True)
    ex = jnp.exp(m2 - m1)
    g1 = 1.0 / (1.0 + ex)
    g2 = ex / (1.0 + ex)

    chosen = jnp.logical_or(lane == e1, lane == e2)
    r_i = lax.broadcasted_iota(jnp.int32, (tm, tm), 0)
    c_i = lax.broadcasted_iota(jnp.int32, (tm, tm), 1)
    earlier = (c_i < r_i).astype(BF16)
    before = jnp.dot(earlier, chosen.astype(BF16), preferred_element_type=F32) + run_ref[...]
    rank1 = jnp.sum(jnp.where(lane == e1, before, 0.0), axis=-1, keepdims=True)
    rank2 = jnp.sum(jnp.where(lane == e2, before, 0.0), axis=-1, keepdims=True)
    total = run_ref[...] + jnp.sum(chosen.astype(F32), axis=0, keepdims=True)
    run_ref[...] = total
    cnt_ref[...] = jnp.broadcast_to(total, cnt_ref.shape)
    meta = jnp.where(lane_i == 0, e1, jnp.where(lane_i == 1, e2, jnp.where(lane_i == 2, rank1, rank2)))
    meta_ref[...] = meta[:, :ROUTE_META].astype(jnp.int32)
    gate_ref[...] = jnp.where(lane_i == 0, g1, g2)


def _combine_body(d0_ref, dn_ref, ys_hbm, gate_ref, x_ref, g_ref, b_ref, o_ref, buf_ref, sem, *, alpha):
    i = pl.program_id(0)
    slot = lax.rem(i, 2)
    n_rows = buf_ref.shape[1] // SUBLANES

    def row_copy(d_ref, to_slot):
        def copy(r):
            return pltpu.make_async_copy(ys_hbm.at[_tile_window(d_ref[0, 0, r])],
                                         buf_ref.at[to_slot, _tile_window(r)], sem.at[to_slot])
        return copy

    def wait_slot(which):
        pltpu.make_async_copy(buf_ref.at[which], buf_ref.at[which], sem.at[which]).wait()

    @pl.when(i == 0)
    def _first_rows():
        _start_row_copies(n_rows, row_copy(d0_ref, 0), priorities=(0, 1))

    wait_slot(slot)
    next_copy = row_copy(dn_ref, 1 - slot)
    for r in range(n_rows):
        next_copy(r).start(priority=r % 2)

    gates = gate_ref[...]
    g1, g2 = gates[:, 0:1], gates[:, 1:2]
    first = _load_row_tiles(buf_ref.at[slot], 0, MOVE_ROWS)
    second = _load_row_tiles(buf_ref.at[slot], MOVE_ROWS, MOVE_ROWS)
    f = jnp.concatenate([g1 * a + g2 * b for a, b in zip(first, second)], axis=-1)
    o_ref[...] = _layer_norm(alpha * x_ref[...] + f, g_ref[...], b_ref[...])

    @pl.when(i == pl.num_programs(0) - 1)
    def _drain():
        wait_slot(1 - slot)


def _combine(ys, dest1, dest2, gates, x, g, b, alpha):
    t = x.shape[0]
    steps = t // MOVE_ROWS
    dest = jnp.concatenate([dest1.reshape(steps, 1, MOVE_ROWS), dest2.reshape(steps, 1, MOVE_ROWS)], axis=-1)
    idx_shape = (1, 1, TOP_K * MOVE_ROWS)
    row = lambda i: (i, 0)
    const = lambda i: (0, 0)
    return pl.pallas_call(
        functools.partial(_combine_body, alpha=alpha),
        grid=(steps,),
        in_specs=[pl.BlockSpec(idx_shape, lambda i: (0, 0, 0), memory_space=pltpu.SMEM),
                  pl.BlockSpec(idx_shape, lambda i: (jnp.minimum(i + 1, steps - 1), 0, 0), memory_space=pltpu.SMEM),
                  pl.BlockSpec(memory_space=pl.ANY),
                  pl.BlockSpec((MOVE_ROWS, LANES), row), pl.BlockSpec((MOVE_ROWS, D_MODEL), row),
                  pl.BlockSpec((1, D_MODEL), const), pl.BlockSpec((1, D_MODEL), const)],
        out_specs=pl.BlockSpec((MOVE_ROWS, D_MODEL), row),
        out_shape=jax.ShapeDtypeStruct((t, D_MODEL), F32),
        scratch_shapes=[pltpu.VMEM((2, TOP_K * MOVE_ROWS * SUBLANES, LANES), F32), pltpu.SemaphoreType.DMA((2,))],
        compiler_params=_params("arbitrary"),
        name="combine_ln",
    )(dest, dest, ys, gates, x, g.reshape(1, -1), b.reshape(1, -1))


def _invert_body(d1_ref, d2_ref, fill_hbm, src_ref, sem):
    fill = pltpu.make_async_copy(fill_hbm, src_ref, sem)
    fill.start()
    fill.wait()

    def place(t, _):
        src_ref[d1_ref[t]] = t
        src_ref[d2_ref[t]] = t
        return 0
    lax.fori_loop(0, d1_ref.shape[0], place, 0, unroll=SCALAR_LOOP_UNROLL)


def _invert_placement(dest1, dest2, n_rows):
    smem = pl.BlockSpec(memory_space=pltpu.SMEM)
    return pl.pallas_call(
        _invert_body,
        in_specs=[smem, smem, pl.BlockSpec(memory_space=pl.ANY)],
        out_specs=smem,
        out_shape=jax.ShapeDtypeStruct((n_rows,), jnp.int32),
        scratch_shapes=[pltpu.SemaphoreType.DMA(())],
        name="invert_placement",
    )(dest1, dest2, jnp.arange(n_rows, dtype=jnp.int32) % dest1.shape[0])


def _moe(x1, x1_tiles, meta, gates, counts, w_gate, w_up, w_down, first_expert, g, b, alpha):
    t = x1.shape[0]
    e1, e2, rank1, rank2 = meta[:, 0], meta[:, 1], meta[:, 2], meta[:, 3]
    sizes = counts[0, :N_EXPERTS].astype(jnp.int32)
    padded = (sizes + FFN_ROWS - 1) // FFN_ROWS * FFN_ROWS
    group_end = jnp.cumsum(padded)
    group_start = group_end - padded
    dest1 = group_start[e1] + rank1
    dest2 = group_start[e2] + rank2
    n_blocks = (t * TOP_K) // FFN_ROWS + N_EXPERTS
    block_start = jnp.arange(n_blocks, dtype=jnp.int32) * FFN_ROWS
    block_e = jnp.minimum(jnp.sum(group_end[None, :] <= block_start[:, None], axis=1), N_EXPERTS - 1)
    n_used = (group_end[-1] // FFN_ROWS).reshape(1)
    src = _invert_placement(dest1, dest2, n_blocks * FFN_ROWS)
    ys = _expert_ffn((block_e + first_expert).astype(jnp.int32), n_used.astype(jnp.int32), x1_tiles, w_gate,
                     w_up, w_down, src)
    return _combine(ys, dest1, dest2, gates, x1, g, b, alpha)


def kernel(x, w_in, w_out, attn_lambda, attn_subln_g, pool_w, pool_scale, conv_w, conv_b, lru_wa, lru_ba,
           lru_wi, lru_bi, lru_lambda, ln1_g, ln1_b, ln2_g, ln2_b, ffn_w_gate, ffn_w_up, ffn_w_down,
           router_w, moe_w_gate, moe_w_up, moe_w_down):
    batch, seq, d = x.shape
    depth = w_in.shape[0]
    assert d == D_MODEL and seq % ATTN_TILE == 0 and seq % MIX_ROWS == 0
    t = batch * seq
    assert t % FFN_ROWS == 0 and t % ROW_TILE == 0
    alpha = (2.0 * depth) ** 0.25
    moe_gate = moe_w_gate.reshape(-1, D_MODEL, D_FF)
    moe_up = moe_w_up.reshape(-1, D_MODEL, D_FF)
    moe_down = moe_w_down.reshape(-1, D_FF, D_MODEL)
    xt = x.reshape(t, d)
    for l in range(depth):
        lambda_init = 0.8 - 0.6 * math.exp(-0.3 * l)
        q, k, v, rest = _inproj(xt, w_in, l)
        y_attn = _attention(q, k, v, attn_lambda[l], attn_subln_g[l], lambda_init, batch, seq)
        y_mix = _mixer(rest, pool_w[l], pool_scale[l], conv_w[l], conv_b[l], lru_wa[l], lru_ba[l],
                       lru_wi[l], lru_bi[l], lru_lambda[l], batch, seq)
        if l % 2 == 0:
            x1, = _outproj(y_attn, y_mix, xt, w_out, l, ln1_g[l], ln1_b[l], alpha)
            xt = _dense_ffn(x1, ffn_w_gate, ffn_w_up, ffn_w_down, l // 2, ln2_g[l], ln2_b[l], alpha)
        else:
            x1, x1_tiles, meta, gates, counts = _outproj(y_attn, y_mix, xt, w_out, l, ln1_g[l], ln1_b[l], alpha,
                                                         w_router=router_w[l // 2])
            xt = _moe(x1, x1_tiles, meta, gates, counts, moe_gate, moe_up, moe_down, (l // 2) * N_EXPERTS,
                      ln2_g[l], ln2_b[l], alpha)
    return xt.reshape(batch, seq, d)
```
